```python
import math
import jax, jax.numpy as jnp
from jax import lax
import numpy as np


D_MODEL = 1024
BATCH = 32
SEQ = 2048
DEPTH = 4

D_MIX = D_MODEL
D_S5 = D_MIX // 2
S5_GROUP = 16
S5_GROUPS = D_S5 // S5_GROUP
S5_STATE = 64
DT_MIN = 0.001
DT_MAX = 0.1
D_GLA = D_MIX - D_S5
GLA_HEADS = 4
GLA_DV = D_GLA // GLA_HEADS
GLA_DK = GLA_DV // 2
GLA_KDIM = GLA_HEADS * GLA_DK
GLA_GATE_RANK = 16
GLA_GATE_NORM = 16.0
GLA_CHUNK = 64
D_IN = D_S5 + 2 * GLA_KDIM + 2 * D_GLA + GLA_GATE_RANK
D_FF = 256 * (-(-(8 * D_MODEL) // (3 * 256)))
EPS = 1e-6

kernel_name = "hymba_s5_gla_hybrid_trunk"


def _rmsnorm(x, gain):
    xf = x.astype(jnp.float32)
    y = xf * lax.rsqrt(jnp.mean(xf * xf, axis=-1, keepdims=True) + EPS)
    return (y * gain.astype(jnp.float32)).astype(x.dtype)


def _complex_linear_combine(e1, e2):
    ar1, ai1, br1, bi1 = e1
    ar2, ai2, br2, bi2 = e2
    return (ar1 * ar2 - ai1 * ai2,
            ar1 * ai2 + ai1 * ar2,
            ar2 * br1 - ai2 * bi1 + br2,
            ar2 * bi1 + ai2 * br1 + bi2)


def _s5_mixer(u, lam_re, lam_im, b_re, b_im, c_re, c_im, d_skip, log_step):
    f32 = jnp.float32
    bsz, seq, _ = u.shape
    uf = u.astype(f32).reshape(bsz, seq, S5_GROUPS, S5_GROUP)
    step = jnp.exp(log_step.astype(f32))[:, None]
    lr = lam_re.astype(f32)
    li = lam_im.astype(f32)
    mag = jnp.exp(lr * step)
    ar = mag * jnp.cos(li * step)
    ai = mag * jnp.sin(li * step)
    den = lr * lr + li * li
    fr = ((ar - 1.0) * lr + ai * li) / den
    fi = (ai * lr - (ar - 1.0) * li) / den
    br = b_re.astype(f32)
    bi = b_im.astype(f32)
    bbar_re = fr[..., None] * br - fi[..., None] * bi
    bbar_im = fr[..., None] * bi + fi[..., None] * br
    xr = jnp.einsum('gph,blgh->lbgp', bbar_re, uf)
    xi = jnp.einsum('gph,blgh->lbgp', bbar_im, uf)
    a_re = jnp.broadcast_to(ar, (seq, 1, S5_GROUPS, S5_STATE))
    a_im = jnp.broadcast_to(ai, (seq, 1, S5_GROUPS, S5_STATE))
    _, _, sr, si = lax.associative_scan(_complex_linear_combine, (a_re, a_im, xr, xi), axis=0)
    y = (jnp.einsum('ghp,lbgp->blgh', c_re.astype(f32), sr)
         - jnp.einsum('ghp,lbgp->blgh', c_im.astype(f32), si))
    y = y + d_skip.astype(f32).reshape(S5_GROUPS, S5_GROUP) * uf
    return y.reshape(bsz, seq, D_S5)


def _gla_mixer(q, k, v, g, gate_lr, w_gate, b_gate, out_norm):
    f32 = jnp.float32
    bsz, seq, _ = q.shape
    n_chunks = seq // GLA_CHUNK
    log_a = jax.nn.log_sigmoid(gate_lr.astype(f32) @ w_gate.astype(f32) + b_gate.astype(f32)) / GLA_GATE_NORM

    def chunked(t, dh):
        return t.astype(f32).reshape(bsz, n_chunks, GLA_CHUNK, GLA_HEADS, dh).transpose(0, 3, 1, 2, 4)

    qc = chunked(q, GLA_DK) * (GLA_DK ** -0.5)
    kc = chunked(k, GLA_DK)
    vc = chunked(v, GLA_DV)
    cum = jnp.cumsum(chunked(log_a, GLA_DK), axis=3)
    last = cum[:, :, :, -1:, :]
    q_dec = qc * jnp.exp(cum)
    k_inv = kc * jnp.exp(-cum)
    k_end = kc * jnp.exp(last - cum)
    causal = jnp.tril(jnp.ones((GLA_CHUNK, GLA_CHUNK), dtype=bool))
    scores = jnp.where(causal, jnp.einsum('bhntk,bhnsk->bhnts', q_dec, k_inv), 0.0)
    o_intra = jnp.einsum('bhnts,bhnsv->bhntv', scores, vc)
    upd = jnp.einsum('bhnsk,bhnsv->bhnkv', k_end, vc)
    decay = jnp.exp(last[:, :, :, 0, :])

    def step(state, inp):
        dec, u = inp
        return dec[..., None] * state + u, state

    s0 = jnp.zeros((bsz, GLA_HEADS, GLA_DK, GLA_DV), f32)
    _, s_prev = lax.scan(step, s0, (jnp.moveaxis(decay, 2, 0), jnp.moveaxis(upd, 2, 0)))
    s_prev = jnp.moveaxis(s_prev, 0, 2)
    o = o_intra + jnp.einsum('bhntk,bhnkv->bhntv', q_dec, s_prev)
    o = o.transpose(0, 2, 3, 1, 4).reshape(bsz, seq, GLA_HEADS, GLA_DV)
    o = o * lax.rsqrt(jnp.mean(o * o, axis=-1, keepdims=True) + EPS)
    o = o.reshape(bsz, seq, D_GLA) * out_norm.astype(f32)
    return jax.nn.silu(g.astype(f32)) * o


def _normal(k, shape, scale):
    return scale * jax.random.normal(k, shape, jnp.float32)


def setup_inputs(seed: int = 0) -> dict:
    key = jax.random.key(seed)
    ks = jax.random.split(key, 22)
    n_idx = jnp.arange(S5_STATE, dtype=jnp.float32)
    lam_im = jnp.broadcast_to(jnp.pi * n_idx, (DEPTH, S5_GROUPS, S5_STATE))
    return {
        'x': _normal(ks[0], (BATCH, SEQ, D_MODEL), 1.0),
        'norm_mix': 1.0 + _normal(ks[1], (DEPTH, D_MODEL), 0.02),
        'w_in': _normal(ks[2], (DEPTH, D_MODEL, D_IN), D_MODEL ** -0.5),
        's5_lam_re': -0.5 + _normal(ks[3], (DEPTH, S5_GROUPS, S5_STATE), 0.01),
        's5_lam_im': lam_im + _normal(ks[4], (DEPTH, S5_GROUPS, S5_STATE), 0.01),
        's5_b_re': _normal(ks[5], (DEPTH, S5_GROUPS, S5_STATE, S5_GROUP), (2 * S5_GROUP) ** -0.5),
        's5_b_im': _normal(ks[6], (DEPTH, S5_GROUPS, S5_STATE, S5_GROUP), (2 * S5_GROUP) ** -0.5),
        's5_c_re': _normal(ks[7], (DEPTH, S5_GROUPS, S5_GROUP, S5_STATE), S5_STATE ** -0.5),
        's5_c_im': _normal(ks[8], (DEPTH, S5_GROUPS, S5_GROUP, S5_STATE), S5_STATE ** -0.5),
        's5_d': _normal(ks[9], (DEPTH, D_S5), 1.0),
        's5_log_step': jax.random.uniform(ks[10], (DEPTH, S5_GROUPS), jnp.float32,
                                          minval=math.log(DT_MIN), maxval=math.log(DT_MAX)),
        's5_w_glu': _normal(ks[11], (DEPTH, D_S5, D_S5), D_S5 ** -0.5),
        's5_b_glu': _normal(ks[12], (DEPTH, D_S5), 0.02),
        's5_out_norm': 1.0 + _normal(ks[13], (DEPTH, D_S5), 0.02),
        'gla_w_gate': _normal(ks[14], (DEPTH, GLA_GATE_RANK, GLA_KDIM), GLA_GATE_RANK ** -0.5),
        'gla_b_gate': _normal(ks[15], (DEPTH, GLA_KDIM), 0.1),
        'gla_out_norm': 1.0 + _normal(ks[16], (DEPTH, D_GLA), 0.02),
        'w_out': _normal(ks[17], (DEPTH, D_MIX, D_MODEL), D_MIX ** -0.5),
        'norm_ffn': 1.0 + _normal(ks[18], (DEPTH, D_MODEL), 0.02),
        'w_ffn_in': _normal(ks[19], (DEPTH, D_MODEL, 2 * D_FF), D_MODEL ** -0.5),
        'w_ffn_out': _normal(ks[20], (DEPTH, D_FF, D_MODEL), D_FF ** -0.5),
        'norm_final': 1.0 + _normal(ks[21], (D_MODEL,), 0.02),
    }


def reference(x, norm_mix, w_in, s5_lam_re, s5_lam_im, s5_b_re, s5_b_im, s5_c_re, s5_c_im,
              s5_d, s5_log_step, s5_w_glu, s5_b_glu, s5_out_norm, gla_w_gate, gla_b_gate,
              gla_out_norm, w_out, norm_ffn, w_ffn_in, w_ffn_out, norm_final):
    o_q = D_S5
    o_k = o_q + GLA_KDIM
    o_v = o_k + GLA_KDIM
    o_g = o_v + D_GLA
    o_r = o_g + D_GLA
    for i in range(DEPTH):
        h = _rmsnorm(x, norm_mix[i])
        z = h @ w_in[i]
        u_s5 = z[..., :o_q]
        q = z[..., o_q:o_k]
        k = z[..., o_k:o_v]
        v = z[..., o_v:o_g]
        g = z[..., o_g:o_r]
        gate_lr = z[..., o_r:]
        y = _s5_mixer(u_s5, s5_lam_re[i], s5_lam_im[i], s5_b_re[i], s5_b_im[i],
                      s5_c_re[i], s5_c_im[i], s5_d[i], s5_log_step[i])
        y = jax.nn.gelu(y)
        y = y * jax.nn.sigmoid(y @ s5_w_glu[i].astype(jnp.float32) + s5_b_glu[i].astype(jnp.float32))
        y_s5 = _rmsnorm(y, s5_out_norm[i]).astype(x.dtype)
        y_gla = _gla_mixer(q, k, v, g, gate_lr, gla_w_gate[i], gla_b_gate[i], gla_out_norm[i]).astype(x.dtype)
        x = x + jnp.concatenate([y_s5, y_gla], axis=-1) @ w_out[i]
        h = _rmsnorm(x, norm_ffn[i])
        gu = h @ w_ffn_in[i]
        x = x + (jax.nn.silu(gu[..., :D_FF]) * gu[..., D_FF:]) @ w_ffn_out[i]
    return _rmsnorm(x, norm_final)
```

```python
import functools

import jax
import jax.numpy as jnp
from jax import lax
from jax.experimental import pallas as pl
from jax.experimental.pallas import tpu as pltpu

S5_GROUP = 16
S5_STATE = 64
GLA_HEADS = 4
GLA_CHUNK = 64
GLA_GATE_NORM = 16.0
EPS = 1e-6

LANES = 128
MXU_DIM = 256
VMEM_LIMIT_BYTES = 56 * 1024 * 1024

DENSE_ROWS = 256
S5_STEPS = 8
GLA_ROWS = 512

BF16 = jnp.bfloat16
F32 = jnp.float32


def _dot(a, b):
    return jnp.dot(a, b, preferred_element_type=F32)


def _dot_nt(a, b):
    return lax.dot_general(a, b, (((1,), (1,)), ((), ())), preferred_element_type=F32)


def _dot_tn(a, b):
    return lax.dot_general(a, b, (((0,), (0,)), ((), ())), preferred_element_type=F32)


def _rms(x, gain):
    return x * lax.rsqrt(jnp.mean(x * x, axis=-1, keepdims=True) + EPS) * gain


def _const_spec(shape):
    return pl.BlockSpec(shape, lambda *_: (0,) * len(shape), pipeline_mode=pl.Buffered(1))


def _s5_prep_kernel(lam_re_ref, lam_im_ref, log_step_ref, b_re_ref, b_im_ref, c_re_ref, c_im_ref,
                    win_gate_ref, w_gate_ref, wb_ref, wc_ref, acoef_ref, weff_ref):
    n_state = lam_re_ref.shape[-1]
    d_s5 = b_re_ref.shape[1]
    lr = lam_re_ref[0]
    li = lam_im_ref[0]
    step = jnp.exp(log_step_ref[0])
    mag = jnp.exp(lr * step)
    ar = mag * jnp.cos(li * step)
    ai = mag * jnp.sin(li * step)
    den = lr * lr + li * li
    fr = ((ar - 1.0) * lr + ai * li) / den
    fi = (ai * lr - (ar - 1.0) * li) / den
    acoef_ref[0, 0:1, :] = ar
    acoef_ref[0, 1:2, :] = ai

    row_g = lax.broadcasted_iota(jnp.int32, (d_s5, n_state), 0) // S5_GROUP
    col_g = lax.broadcasted_iota(jnp.int32, (d_s5, n_state), 1) // S5_STATE
    diag = row_g == col_g
    br = b_re_ref[0]
    bi = b_im_ref[0]
    wb_ref[0, :, 0:n_state] = jnp.where(diag, fr * br - fi * bi, 0.0).astype(BF16)
    wb_ref[0, :, n_state:2 * n_state] = jnp.where(diag, fr * bi + fi * br, 0.0).astype(BF16)

    row_g = lax.broadcasted_iota(jnp.int32, (n_state, d_s5), 0) // S5_STATE
    col_g = lax.broadcasted_iota(jnp.int32, (n_state, d_s5), 1) // S5_GROUP
    diag = row_g == col_g
    wc_ref[0, 0:n_state, :] = jnp.where(diag, c_re_ref[0], 0.0).astype(BF16)
    wc_ref[0, n_state:2 * n_state, :] = jnp.where(diag, -c_im_ref[0], 0.0).astype(BF16)

    a = win_gate_ref[0]
    b = w_gate_ref[0]
    a_hi = a.astype(BF16)
    a_lo = (a - a_hi.astype(F32)).astype(BF16)
    b_hi = b.astype(BF16)
    b_lo = (b - b_hi.astype(F32)).astype(BF16)
    weff_ref[0] = (_dot(a_hi, b_hi) + _dot(a_hi, b_lo) + _dot(a_lo, b_hi)).astype(BF16)


def _s5_prep(lam_re, lam_im, log_step, b_re, b_im, c_re, c_im, win_gate, w_gate):
    depth, groups, states = lam_re.shape
    n_state = groups * states
    d_s5 = groups * S5_GROUP
    d_model = win_gate.shape[1]
    rank = win_gate.shape[2]
    kdim = w_gate.shape[2]
    row = lambda t: t.reshape(depth, 1, n_state)
    step = row(jnp.broadcast_to(log_step[:, :, None], (depth, groups, states)))
    tile_b = lambda t: jnp.tile(t.transpose(0, 3, 1, 2).reshape(depth, S5_GROUP, n_state), (1, groups, 1))
    tile_c = lambda t: jnp.tile(t.transpose(0, 1, 3, 2).reshape(depth, n_state, S5_GROUP), (1, 1, groups))
    per_layer = lambda *s: pl.BlockSpec((1,) + s, lambda i: (i,) + (0,) * len(s))
    return pl.pallas_call(
        _s5_prep_kernel,
        grid=(depth,),
        in_specs=[per_layer(1, n_state)] * 3 + [per_layer(d_s5, n_state)] * 2 + [per_layer(n_state, d_s5)] * 2
        + [per_layer(d_model, rank), per_layer(rank, kdim)],
        out_specs=[per_layer(d_s5, 2 * n_state), per_layer(2 * n_state, d_s5), per_layer(2, n_state),
                   per_layer(d_model, kdim)],
        out_shape=[jax.ShapeDtypeStruct((depth, d_s5, 2 * n_state), BF16),
                   jax.ShapeDtypeStruct((depth, 2 * n_state, d_s5), BF16),
                   jax.ShapeDtypeStruct((depth, 2, n_state), F32),
                   jax.ShapeDtypeStruct((depth, d_model, kdim), BF16)],
        compiler_params=pltpu.CompilerParams(dimension_semantics=("arbitrary",),
                                             vmem_limit_bytes=VMEM_LIMIT_BYTES),
        name="s5_prep",
    )(row(lam_re), row(lam_im), step, tile_b(b_re), tile_b(b_im), tile_c(c_re), tile_c(c_im),
      win_gate, w_gate)


def _s5_kernel(u_ref, wb_ref, wc_ref, acoef_ref, d_ref, wglu_ref, bglu_ref, onorm_ref, y_ref,
               xs_ref, carry_ref, *, batch):
    rows, d_s5 = u_ref.shape
    n_state = acoef_ref.shape[1]
    steps = rows // batch
    half_c = d_s5 // 2
    half_s = n_state // 2

    @pl.when(pl.program_id(0) == 0)
    def _():
        carry_ref[...] = jnp.zeros_like(carry_ref)

    u = u_ref[...]
    ub = u.astype(BF16)
    for part in range(2):
        for half in range(2):
            cols = pl.ds(part * n_state + half * half_s, half_s)
            xs_ref[:, cols] = _dot(ub[:, half * half_c:(half + 1) * half_c],
                                   wb_ref[pl.ds(half * half_c, half_c), cols])

    for j in range(n_state // LANES):
        re = pl.ds(j * LANES, LANES)
        im = pl.ds(n_state + j * LANES, LANES)
        ar = jnp.broadcast_to(acoef_ref[0:1, re], (batch, LANES))
        ai = jnp.broadcast_to(acoef_ref[1:2, re], (batch, LANES))
        sr = carry_ref[:, re]
        si = carry_ref[:, im]
        for t in range(steps):
            r = pl.ds(t * batch, batch)
            sr, si = (ar * sr - ai * si + xs_ref[r, re],
                      ar * si + ai * sr + xs_ref[r, im])
            xs_ref[r, re] = sr
            xs_ref[r, im] = si
        carry_ref[:, re] = sr
        carry_ref[:, im] = si

    ys = []
    for half in range(2):
        cols = pl.ds(half * half_c, half_c)
        acc = None
        for part in range(2):
            srows = pl.ds(part * n_state + half * half_s, half_s)
            term = _dot(xs_ref[:, srows].astype(BF16), wc_ref[srows, cols])
            acc = term if acc is None else acc + term
        ys.append(acc)
    y = jnp.concatenate(ys, axis=-1) + d_ref[...] * u
    y = jax.nn.gelu(y)
    y = y * jax.nn.sigmoid(_dot(y.astype(BF16), wglu_ref[...]) + bglu_ref[...])
    y_ref[...] = _rms(y, onorm_ref[...])


def _s5_mixer(u, wb, wc, acoef, d_skip, wglu, bglu, onorm, *, batch):
    rows_total, d_s5 = u.shape
    n_state2 = wb.shape[1]
    rows = S5_STEPS * batch
    return pl.pallas_call(
        functools.partial(_s5_kernel, batch=batch),
        grid=(rows_total // rows,),
        in_specs=[pl.BlockSpec((rows, d_s5), lambda t: (t, 0)),
                  _const_spec(wb.shape), _const_spec(wc.shape), _const_spec(acoef.shape),
                  _const_spec(d_skip.shape), _const_spec(wglu.shape), _const_spec(bglu.shape),
                  _const_spec(onorm.shape)],
        out_specs=pl.BlockSpec((rows, d_s5), lambda t: (t, 0)),
        out_shape=jax.ShapeDtypeStruct((rows_total, d_s5), F32),
        scratch_shapes=[pltpu.VMEM((rows, n_state2), F32), pltpu.VMEM((batch, n_state2), F32)],
        compiler_params=pltpu.CompilerParams(dimension_semantics=("arbitrary",),
                                             vmem_limit_bytes=VMEM_LIMIT_BYTES),
        name="s5_mixer",
    )(u, wb, wc, acoef, d_skip, wglu, bglu, onorm)


def _log_sigmoid(x):
    return jnp.minimum(x, 0.0) - jnp.log1p(jnp.exp(-jnp.abs(x)))


def _gla_kernel(zg_ref, onorm_ref, y_ref, st_ref):
    rows = zg_ref.shape[0]
    d_gla = y_ref.shape[1]
    dv = d_gla // GLA_HEADS
    dk = dv // 2
    kdim = GLA_HEADS * dk
    o_k, o_v, o_g, o_a = kdim, 2 * kdim, 2 * kdim + d_gla, 2 * kdim + 2 * d_gla
    c = GLA_CHUNK
    assert 2 * dk == LANES and dv == LANES

    @pl.when(pl.program_id(1) == 0)
    def _():
        st_ref[...] = jnp.zeros_like(st_ref)

    r_i = lax.broadcasted_iota(jnp.int32, (c, 3 * c), 0)
    c_i = lax.broadcasted_iota(jnp.int32, (c, 3 * c), 1)
    tri3 = jnp.where((c_i % c) <= r_i, 1.0, 0.0).astype(BF16)
    causal = lax.broadcasted_iota(jnp.int32, (c, c), 0) >= lax.broadcasted_iota(jnp.int32, (c, c), 1)
    low_head = lax.broadcasted_iota(jnp.int32, (c, LANES), 1) < dk

    def chunk(n, _):
        r = pl.ds(pl.multiple_of(n * c, c), c)
        la = zg_ref[r, o_a:o_a + kdim]
        la_hi = la.astype(BF16)
        rem = la - la_hi.astype(F32)
        la_mid = rem.astype(BF16)
        la_lo = (rem - la_mid.astype(F32)).astype(BF16)
        cum = _dot(tri3, jnp.concatenate([la_hi, la_mid, la_lo], axis=0))
        total = cum[c - 1:c, :]
        q_dec = zg_ref[r, 0:kdim] * (dk ** -0.5) * jnp.exp(cum)
        k = zg_ref[r, o_k:o_k + kdim]
        k_inv = (k * jnp.exp(-cum)).astype(BF16)
        k_end = k * jnp.exp(total - cum)
        decay = jnp.exp(total)
        for pair in range(GLA_HEADS // 2):
            lanes = slice(pair * LANES, (pair + 1) * LANES)
            st = st_ref[pair]
            st_b = st.astype(BF16)
            upd = None
            for sub in range(2):
                h = 2 * pair + sub
                mine = low_head if sub == 0 else jnp.logical_not(low_head)
                q_h = jnp.where(mine, q_dec[:, lanes], 0.0).astype(BF16)
                ke_h = jnp.where(mine, k_end[:, lanes], 0.0).astype(BF16)
                v_h = zg_ref[r, o_v + h * dv:o_v + (h + 1) * dv].astype(BF16)
                s = jnp.where(causal, _dot_nt(q_h, k_inv[:, lanes]), 0.0)
                o = _dot(s.astype(BF16), v_h) + _dot_nt(q_h, st_b)
                o = o * lax.rsqrt(jnp.mean(o * o, axis=-1, keepdims=True) + EPS)
                g = zg_ref[r, o_g + h * dv:o_g + (h + 1) * dv]
                y_ref[r, h * dv:(h + 1) * dv] = jax.nn.silu(g) * (o * onorm_ref[:, h * dv:(h + 1) * dv])
                term = _dot_tn(v_h, ke_h)
                upd = term if upd is None else upd + term
            st_ref[pair] = st * decay[:, lanes] + upd
        return 0

    lax.fori_loop(0, rows // c, chunk, 0)


def _gla_mixer(zg, onorm, *, batch, d_gla):
    rows_total, width = zg.shape
    seq = rows_total // batch
    rows = min(GLA_ROWS, seq)
    n_t = seq // rows
    return pl.pallas_call(
        _gla_kernel,
        grid=(batch, n_t),
        in_specs=[pl.BlockSpec((rows, width), lambda b, t: (b * n_t + t, 0)), _const_spec(onorm.shape)],
        out_specs=pl.BlockSpec((rows, d_gla), lambda b, t: (b * n_t + t, 0)),
        out_shape=jax.ShapeDtypeStruct((rows_total, d_gla), F32),
        scratch_shapes=[pltpu.VMEM((GLA_HEADS // 2, d_gla // GLA_HEADS, LANES), F32)],
        compiler_params=pltpu.CompilerParams(dimension_semantics=("arbitrary", "arbitrary"),
                                             vmem_limit_bytes=VMEM_LIMIT_BYTES),
        name="gla_mixer",
    )(zg, onorm)


def _dense_kernel(*refs, mix, proj, d_s5, d_ff):
    refs = list(refs)
    x_ref = refs.pop(0)
    x = x_ref[...]
    if mix:
        ys5_ref, ygla_ref, wout_ref, nffn_ref, wfi_ref, wfo_ref = refs[:6]
        refs = refs[6:]
        x = x + _dot(ys5_ref[...].astype(BF16), wout_ref[0:d_s5, :]) \
              + _dot(ygla_ref[...].astype(BF16), wout_ref[d_s5:, :])
        h = _rms(x, nffn_ref[...]).astype(BF16)
        gu = _dot(h, wfi_ref[...])
        act = (jax.nn.silu(gu[:, :d_ff]) * gu[:, d_ff:]).astype(BF16)
        x = x + _dot(act, wfo_ref[...])
    norm_ref = refs.pop(0)
    h = _rms(x, norm_ref[...])
    if not proj:
        (out_ref,) = refs
        out_ref[...] = h
        return
    win_ref, bgate_ref = refs[:2]
    u_ref, zg_ref = refs[-2:]
    if mix:
        refs[2][...] = x
    z = _dot(h.astype(BF16), win_ref[...])
    u_ref[...] = z[:, :d_s5]
    o_a = zg_ref.shape[1] - bgate_ref.shape[1]
    zg_ref[:, :o_a] = z[:, d_s5:d_s5 + o_a]
    zg_ref[:, o_a:] = _log_sigmoid(z[:, d_s5 + o_a:] + bgate_ref[...]) * (1.0 / GLA_GATE_NORM)


def _dense(x, mix_args, norm, proj_args, *, batch, d_s5):
    rows_total, d_model = x.shape
    seq = rows_total // batch
    tm = min(DENSE_ROWS, seq)
    n_t = seq // tm
    token_rows = lambda w: pl.BlockSpec((tm, w), lambda b, t: (b * n_t + t, 0))
    time_major = pl.BlockSpec((tm, d_s5), lambda b, t: (t, b))
    mix, proj = mix_args is not None, proj_args is not None
    args, in_specs = [x], [token_rows(d_model)]
    d_ff = 0
    if mix:
        y_s5, y_gla, w_out, norm_ffn, w_fi, w_fo = mix_args
        d_ff = w_fo.shape[0]
        args += [y_s5, y_gla, w_out, norm_ffn, w_fi, w_fo]
        in_specs += [time_major, token_rows(y_gla.shape[1]), _const_spec(w_out.shape),
                     _const_spec(norm_ffn.shape), _const_spec(w_fi.shape), _const_spec(w_fo.shape)]
    args.append(norm)
    in_specs.append(_const_spec(norm.shape))
    if proj:
        w_in, b_gate = proj_args
        zg_w = w_in.shape[1] - d_s5
        args += [w_in, b_gate]
        in_specs += [_const_spec(w_in.shape), _const_spec(b_gate.shape)]
        out_specs = [token_rows(d_model), time_major, token_rows(zg_w)]
        out_shape = [jax.ShapeDtypeStruct((rows_total, d_model), F32),
                     jax.ShapeDtypeStruct((seq, batch * d_s5), F32),
                     jax.ShapeDtypeStruct((rows_total, zg_w), F32)]
        if not mix:
            out_specs, out_shape = out_specs[1:], out_shape[1:]
    else:
        out_specs = token_rows(d_model)
        out_shape = jax.ShapeDtypeStruct((rows_total, d_model), F32)
    return pl.pallas_call(
        functools.partial(_dense_kernel, mix=mix, proj=proj, d_s5=d_s5, d_ff=d_ff),
        grid=(batch, n_t),
        in_specs=in_specs,
        out_specs=out_specs,
        out_shape=out_shape,
        compiler_params=pltpu.CompilerParams(dimension_semantics=("arbitrary", "arbitrary"),
                                             vmem_limit_bytes=VMEM_LIMIT_BYTES),
        name="dense_" + ("mix" if mix else "") + ("proj" if proj else "final"),
    )(*args)


def kernel(x, norm_mix, w_in, s5_lam_re, s5_lam_im, s5_b_re, s5_b_im, s5_c_re, s5_c_im, s5_d, s5_log_step,
           s5_w_glu, s5_b_glu, s5_out_norm, gla_w_gate, gla_b_gate, gla_out_norm, w_out, norm_ffn, w_ffn_in,
           w_ffn_out, norm_final):
    batch, seq, d_model = x.shape
    depth = w_in.shape[0]
    d_s5 = s5_d.shape[1]
    d_gla = gla_out_norm.shape[1]
    kdim = gla_w_gate.shape[2]
    o_gate = d_s5 + 2 * kdim + 2 * d_gla
    assert batch % 8 == 0 and seq % DENSE_ROWS == 0 and seq % GLA_CHUNK == 0 and seq % S5_STEPS == 0

    wb, wc, acoef, w_eff = _s5_prep(s5_lam_re, s5_lam_im, s5_log_step, s5_b_re, s5_b_im, s5_c_re, s5_c_im,
                                    w_in[:, :, o_gate:], gla_w_gate)
    w_in_eff = jnp.concatenate([w_in[:, :, :o_gate].astype(BF16), w_eff], axis=-1)
    w_out_b, w_fi_b, w_fo_b, w_glu_b = (w.astype(BF16) for w in (w_out, w_ffn_in, w_ffn_out, s5_w_glu))
    vec = lambda v, i: v[i][None, :]

    xf = x.reshape(batch * seq, d_model)
    u, zg = _dense(xf, None, vec(norm_mix, 0), (w_in_eff[0], vec(gla_b_gate, 0)), batch=batch, d_s5=d_s5)
    for i in range(depth):
        y_s5 = _s5_mixer(u.reshape(seq * batch, d_s5), wb[i], wc[i], acoef[i], vec(s5_d, i), w_glu_b[i],
                         vec(s5_b_glu, i), vec(s5_out_norm, i), batch=batch)
        y_gla = _gla_mixer(zg, vec(gla_out_norm, i), batch=batch, d_gla=d_gla)
        mix_args = (y_s5.reshape(seq, batch * d_s5), y_gla, w_out_b[i], vec(norm_ffn, i), w_fi_b[i], w_fo_b[i])
        if i + 1 < depth:
            xf, u, zg = _dense(xf, mix_args, vec(norm_mix, i + 1), (w_in_eff[i + 1], vec(gla_b_gate, i + 1)),
                               batch=batch, d_s5=d_s5)
        else:
            xf = _dense(xf, mix_args, norm_final[None, :], None, batch=batch, d_s5=d_s5)
    return xf.reshape(batch, seq, d_model)
```

```python
import functools

import jax
import jax.numpy as jnp
from jax import lax
from jax.experimental import pallas as pl
from jax.experimental.pallas import tpu as pltpu

S5_GROUP = 16
S5_STATE = 64
GLA_HEADS = 4
GLA_CHUNK = 64
GLA_GATE_NORM = 16.0
EPS = 1e-6

LANES = 128
MXU_DIM = 256
VMEM_LIMIT_BYTES = 56 * 1024 * 1024

DENSE_ROWS = 256
S5_STEPS = 8
GLA_ROWS = 512

BF16 = jnp.bfloat16
F32 = jnp.float32


def _dot(a, b):
    return jnp.dot(a, b, preferred_element_type=F32)


def _dot_nt(a, b):
    return lax.dot_general(a, b, (((1,), (1,)), ((), ())), preferred_element_type=F32)


def _dot_tn(a, b):
    return lax.dot_general(a, b, (((0,), (0,)), ((), ())), preferred_element_type=F32)


def _rms(x, gain):
    return x * lax.rsqrt(jnp.mean(x * x, axis=-1, keepdims=True) + EPS) * gain


def _const_spec(shape):
    return pl.BlockSpec(shape, lambda *_: (0,) * len(shape), pipeline_mode=pl.Buffered(1))


def _s5_prep_kernel(lam_re_ref, lam_im_ref, log_step_ref, b_re_ref, b_im_ref, c_re_ref, c_im_ref,
                    win_gate_ref, w_gate_ref, wb_ref, wc_ref, acoef_ref, weff_ref):
    n_state = lam_re_ref.shape[-1]
    d_s5 = b_re_ref.shape[1]
    lr = lam_re_ref[0]
    li = lam_im_ref[0]
    step = jnp.exp(log_step_ref[0])
    mag = jnp.exp(lr * step)
    ar = mag * jnp.cos(li * step)
    ai = mag * jnp.sin(li * step)
    den = lr * lr + li * li
    fr = ((ar - 1.0) * lr + ai * li) / den
    fi = (ai * lr - (ar - 1.0) * li) / den
    acoef_ref[0, 0:1, :] = ar
    acoef_ref[0, 1:2, :] = ai

    row_g = lax.broadcasted_iota(jnp.int32, (d_s5, n_state), 0) // S5_GROUP
    col_g = lax.broadcasted_iota(jnp.int32, (d_s5, n_state), 1) // S5_STATE
    diag = row_g == col_g
    br = b_re_ref[0]
    bi = b_im_ref[0]
    wb_ref[0, :, 0:n_state] = jnp.where(diag, fr * br - fi * bi, 0.0).astype(BF16)
    wb_ref[0, :, n_state:2 * n_state] = jnp.where(diag, fr * bi + fi * br, 0.0).astype(BF16)

    row_g = lax.broadcasted_iota(jnp.int32, (n_state, d_s5), 0) // S5_STATE
    col_g = lax.broadcasted_iota(jnp.int32, (n_state, d_s5), 1) // S5_GROUP
    diag = row_g == col_g
    wc_ref[0, 0:n_state, :] = jnp.where(diag, c_re_ref[0], 0.0).astype(BF16)
    wc_ref[0, n_state:2 * n_state, :] = jnp.where(diag, -c_im_ref[0], 0.0).astype(BF16)

    a = win_gate_ref[0]
    b = w_gate_ref[0]
    a_hi = a.astype(BF16)
    a_lo = (a - a_hi.astype(F32)).astype(BF16)
    b_hi = b.astype(BF16)
    b_lo = (b - b_hi.astype(F32)).astype(BF16)
    weff_ref[0] = (_dot(a_hi, b_hi) + _dot(a_hi, b_lo) + _dot(a_lo, b_hi)).astype(BF16)


def _s5_prep(lam_re, lam_im, log_step, b_re, b_im, c_re, c_im, win_gate, w_gate):
    depth, groups, states = lam_re.shape
    n_state = groups * states
    d_s5 = groups * S5_GROUP
    d_model = win_gate.shape[1]
    rank = win_gate.shape[2]
    kdim = w_gate.shape[2]
    row = lambda t: t.reshape(depth, 1, n_state)
    step = row(jnp.broadcast_to(log_step[:, :, None], (depth, groups, states)))
    tile_b = lambda t: jnp.tile(t.transpose(0, 3, 1, 2).reshape(depth, S5_GROUP, n_state), (1, groups, 1))
    tile_c = lambda t: jnp.tile(t.transpose(0, 1, 3, 2).reshape(depth, n_state, S5_GROUP), (1, 1, groups))
    per_layer = lambda *s: pl.BlockSpec((1,) + s, lambda i: (i,) + (0,) * len(s))
    return pl.pallas_call(
        _s5_prep_kernel,
        grid=(depth,),
        in_specs=[per_layer(1, n_state)] * 3 + [per_layer(d_s5, n_state)] * 2 + [per_layer(n_state, d_s5)] * 2
        + [per_layer(d_model, rank), per_layer(rank, kdim)],
        out_specs=[per_layer(d_s5, 2 * n_state), per_layer(2 * n_state, d_s5), per_layer(2, n_state),
                   per_layer(d_model, kdim)],
        out_shape=[jax.ShapeDtypeStruct((depth, d_s5, 2 * n_state), BF16),
                   jax.ShapeDtypeStruct((depth, 2 * n_state, d_s5), BF16),
                   jax.ShapeDtypeStruct((depth, 2, n_state), F32),
                   jax.ShapeDtypeStruct((depth, d_model, kdim), BF16)],
        compiler_params=pltpu.CompilerParams(dimension_semantics=("arbitrary",),
                                             vmem_limit_bytes=VMEM_LIMIT_BYTES),
        name="s5_prep",
    )(row(lam_re), row(lam_im), step, tile_b(b_re), tile_b(b_im), tile_c(c_re), tile_c(c_im),
      win_gate, w_gate)


def _s5_kernel(u_ref, wb_ref, wc_ref, acoef_ref, d_ref, wglu_ref, bglu_ref, onorm_ref, y_ref,
               xs_ref, carry_ref, perm_ref):
    batch, steps, d_s5 = u_ref.shape
    rows = batch * steps
    n_state = acoef_ref.shape[1]
    half_c = d_s5 // 2
    half_s = n_state // 2

    @pl.when(pl.program_id(0) == 0)
    def _():
        carry_ref[...] = jnp.zeros_like(carry_ref)
        dst = lax.broadcasted_iota(jnp.int32, (rows, rows), 0)
        src = lax.broadcasted_iota(jnp.int32, (rows, rows), 1)
        perm_ref[0] = jnp.where(src == (dst % batch) * steps + dst // batch, 1.0, 0.0).astype(BF16)
        perm_ref[1] = jnp.where(src == (dst % steps) * batch + dst // steps, 1.0, 0.0).astype(BF16)

    u = u_ref[...].reshape(rows, d_s5)
    ub = _dot(perm_ref[0], u.astype(BF16)).astype(BF16)
    for part in range(2):
        for half in range(2):
            cols = pl.ds(part * n_state + half * half_s, half_s)
            xs_ref[:, cols] = _dot(ub[:, half * half_c:(half + 1) * half_c],
                                   wb_ref[pl.ds(half * half_c, half_c), cols])

    for j in range(n_state // LANES):
        re = pl.ds(j * LANES, LANES)
        im = pl.ds(n_state + j * LANES, LANES)
        ar = jnp.broadcast_to(acoef_ref[0:1, re], (batch, LANES))
        ai = jnp.broadcast_to(acoef_ref[1:2, re], (batch, LANES))
        sr = carry_ref[:, re]
        si = carry_ref[:, im]
        for t in range(steps):
            r = pl.ds(t * batch, batch)
            sr, si = (ar * sr - ai * si + xs_ref[r, re],
                      ar * si + ai * sr + xs_ref[r, im])
            xs_ref[r, re] = sr
            xs_ref[r, im] = si
        carry_ref[:, re] = sr
        carry_ref[:, im] = si

    ys = []
    for half in range(2):
        cols = pl.ds(half * half_c, half_c)
        acc = None
        for part in range(2):
            srows = pl.ds(part * n_state + half * half_s, half_s)
            term = _dot(xs_ref[:, srows].astype(BF16), wc_ref[srows, cols])
            acc = term if acc is None else acc + term
        ys.append(acc)
    y = jnp.concatenate(ys, axis=-1)
    y_hi = y.astype(BF16)
    y_lo = (y - y_hi.astype(F32)).astype(BF16)
    y2 = _dot(perm_ref[1], jnp.concatenate([y_hi, y_lo], axis=-1))
    y = y2[:, :d_s5] + y2[:, d_s5:] + d_ref[...] * u
    y = jax.nn.gelu(y)
    y = y * jax.nn.sigmoid(_dot(y.astype(BF16), wglu_ref[...]) + bglu_ref[...])
    y_ref[...] = _rms(y, onorm_ref[...]).reshape(batch, steps, d_s5)


def _s5_mixer(u, wb, wc, acoef, d_skip, wglu, bglu, onorm):
    batch, seq, d_s5 = u.shape
    n_state2 = wb.shape[1]
    rows = S5_STEPS * batch
    block = pl.BlockSpec((batch, S5_STEPS, d_s5), lambda t: (0, t, 0))
    return pl.pallas_call(
        _s5_kernel,
        grid=(seq // S5_STEPS,),
        in_specs=[block, _const_spec(wb.shape), _const_spec(wc.shape), _const_spec(acoef.shape),
                  _const_spec(d_skip.shape), _const_spec(wglu.shape), _const_spec(bglu.shape),
                  _const_spec(onorm.shape)],
        out_specs=block,
        out_shape=jax.ShapeDtypeStruct((batch, seq, d_s5), F32),
        scratch_shapes=[pltpu.VMEM((rows, n_state2), F32), pltpu.VMEM((batch, n_state2), F32),
                        pltpu.VMEM((2, rows, rows), BF16)],
        compiler_params=pltpu.CompilerParams(dimension_semantics=("arbitrary",),
                                             vmem_limit_bytes=VMEM_LIMIT_BYTES),
        name="s5_mixer",
    )(u, wb, wc, acoef, d_skip, wglu, bglu, onorm)


def _log_sigmoid(x):
    return jnp.minimum(x, 0.0) - jnp.log1p(jnp.exp(-jnp.abs(x)))


def _gla_kernel(zg_ref, onorm_ref, y_ref, st_ref):
    rows = zg_ref.shape[0]
    d_gla = y_ref.shape[1]
    dv = d_gla // GLA_HEADS
    dk = dv // 2
    kdim = GLA_HEADS * dk
    o_k, o_v, o_g, o_a = kdim, 2 * kdim, 2 * kdim + d_gla, 2 * kdim + 2 * d_gla
    c = GLA_CHUNK
    assert 2 * dk == LANES and dv == LANES

    @pl.when(pl.program_id(1) == 0)
    def _():
        st_ref[...] = jnp.zeros_like(st_ref)

    r_i = lax.broadcasted_iota(jnp.int32, (c, 3 * c), 0)
    c_i = lax.broadcasted_iota(jnp.int32, (c, 3 * c), 1)
    tri3 = jnp.where((c_i % c) <= r_i, 1.0, 0.0).astype(BF16)
    causal = (lax.broadcasted_iota(jnp.int32, (c, LANES), 0)
              >= lax.broadcasted_iota(jnp.int32, (c, LANES), 1) % c)
    low_head = lax.broadcasted_iota(jnp.int32, (c, LANES), 1) < dk
    own_block = ((lax.broadcasted_iota(jnp.int32, (2 * dv, LANES), 0) < dv)
                 == (lax.broadcasted_iota(jnp.int32, (2 * dv, LANES), 1) < dk))
    n_chunks = rows // c
    pairs = GLA_HEADS // 2
    rows_of = lambda t, n: t[n * c:(n + 1) * c]
    zeros_v = jnp.zeros((c, dv), BF16)

    la = zg_ref[:, o_a:o_a + kdim]
    la_hi = la.astype(BF16)
    rem = la - la_hi.astype(F32)
    la_mid = rem.astype(BF16)
    la_lo = (rem - la_mid.astype(F32)).astype(BF16)
    cums = [_dot(tri3, jnp.concatenate([rows_of(la_hi, n), rows_of(la_mid, n), rows_of(la_lo, n)], axis=0))
            for n in range(n_chunks)]
    cum = jnp.concatenate(cums, axis=0)
    totals = [t[c - 1:c, :] for t in cums]
    total = jnp.concatenate([jnp.broadcast_to(t, (c, kdim)) for t in totals], axis=0)
    q_dec = (zg_ref[:, 0:kdim] * (dk ** -0.5) * jnp.exp(cum)).astype(BF16)
    k = zg_ref[:, o_k:o_k + kdim]
    k_inv = k * jnp.exp(-cum)
    k_end = (k * jnp.exp(total - cum)).astype(BF16)
    v = zg_ref[:, o_v:o_v + d_gla].astype(BF16)

    def lanes_of(t, n, pair):
        return t[n * c:(n + 1) * c, pair * LANES:(pair + 1) * LANES]

    scores = []
    for n in range(n_chunks):
        for pair in range(pairs):
            kp = lanes_of(k_inv, n, pair)
            keys = jnp.concatenate([jnp.where(low_head, kp, 0.0), jnp.where(low_head, 0.0, kp)], axis=0)
            s = _dot_nt(lanes_of(q_dec, n, pair), keys.astype(BF16))
            scores.append(jnp.where(causal, s, 0.0).astype(BF16))

    intra, upds = [], []
    for n in range(n_chunks):
        for pair in range(pairs):
            v_even = v[n * c:(n + 1) * c, (2 * pair) * dv:(2 * pair + 1) * dv]
            v_odd = v[n * c:(n + 1) * c, (2 * pair + 1) * dv:(2 * pair + 2) * dv]
            v_diag = jnp.concatenate([jnp.concatenate([v_even, zeros_v], axis=1),
                                      jnp.concatenate([zeros_v, v_odd], axis=1)], axis=0)
            intra.append(_dot(scores[n * pairs + pair], v_diag))
            upd = _dot_tn(jnp.concatenate([v_even, v_odd], axis=1), lanes_of(k_end, n, pair))
            upds.append(jnp.where(own_block, upd, 0.0))

    outs = [[None] * pairs for _ in range(n_chunks)]
    for pair in range(pairs):
        st = st_ref[pair]
        for n in range(n_chunks):
            outs[n][pair] = intra[n * pairs + pair] + _dot_nt(lanes_of(q_dec, n, pair), st.astype(BF16))
            decay = jnp.exp(totals[n][:, pair * LANES:(pair + 1) * LANES])
            st = st * decay + upds[n * pairs + pair]
        st_ref[pair] = st

    for h in range(GLA_HEADS):
        o = jnp.concatenate([outs[n][h // 2][:, (h % 2) * dv:(h % 2 + 1) * dv] for n in range(n_chunks)], axis=0)
        o = o * lax.rsqrt(jnp.mean(o * o, axis=-1, keepdims=True) + EPS)
        g = zg_ref[:, o_g + h * dv:o_g + (h + 1) * dv]
        y_ref[:, h * dv:(h + 1) * dv] = jax.nn.silu(g) * (o * onorm_ref[:, h * dv:(h + 1) * dv])


def _gla_mixer(zg, onorm, *, batch, d_gla):
    rows_total, width = zg.shape
    seq = rows_total // batch
    rows = min(GLA_ROWS, seq)
    n_t = seq // rows
    return pl.pallas_call(
        _gla_kernel,
        grid=(batch, n_t),
        in_specs=[pl.BlockSpec((rows, width), lambda b, t: (b * n_t + t, 0)), _const_spec(onorm.shape)],
        out_specs=pl.BlockSpec((rows, d_gla), lambda b, t: (b * n_t + t, 0)),
        out_shape=jax.ShapeDtypeStruct((rows_total, d_gla), F32),
        scratch_shapes=[pltpu.VMEM((GLA_HEADS // 2, 2 * d_gla // GLA_HEADS, LANES), F32)],
        compiler_params=pltpu.CompilerParams(dimension_semantics=("arbitrary", "arbitrary"),
                                             vmem_limit_bytes=VMEM_LIMIT_BYTES),
        name="gla_mixer",
    )(zg, onorm)


def _dense_kernel(*refs, mix, proj, d_s5, d_ff):
    refs = list(refs)
    x_ref = refs.pop(0)
    x = x_ref[...]
    if mix:
        ys5_ref, ygla_ref, wout_ref, nffn_ref, wfi_ref, wfo_ref = refs[:6]
        refs = refs[6:]
        x = x + _dot(ys5_ref[...].astype(BF16), wout_ref[0:d_s5, :]) \
              + _dot(ygla_ref[...].astype(BF16), wout_ref[d_s5:, :])
        h = _rms(x, nffn_ref[...]).astype(BF16)
        gu = _dot(h, wfi_ref[...])
        act = (jax.nn.silu(gu[:, :d_ff]) * gu[:, d_ff:]).astype(BF16)
        x = x + _dot(act, wfo_ref[...])
    norm_ref = refs.pop(0)
    h = _rms(x, norm_ref[...])
    if not proj:
        (out_ref,) = refs
        out_ref[...] = h
        return
    win_ref, bgate_ref = refs[:2]
    u_ref, zg_ref = refs[-2:]
    if mix:
        refs[2][...] = x
    z = _dot(h.astype(BF16), win_ref[...])
    u_ref[...] = z[:, :d_s5]
    o_a = zg_ref.shape[1] - bgate_ref.shape[1]
    zg_ref[:, :o_a] = z[:, d_s5:d_s5 + o_a]
    zg_ref[:, o_a:] = _log_sigmoid(z[:, d_s5 + o_a:] + bgate_ref[...]) * (1.0 / GLA_GATE_NORM)


def _dense(x, mix_args, norm, proj_args, *, d_s5):
    rows_total, d_model = x.shape
    tm = DENSE_ROWS
    token_rows = lambda w: pl.BlockSpec((tm, w), lambda r: (r, 0))
    mix, proj = mix_args is not None, proj_args is not None
    args, in_specs = [x], [token_rows(d_model)]
    d_ff = 0
    if mix:
        y_s5, y_gla, w_out, norm_ffn, w_fi, w_fo = mix_args
        d_ff = w_fo.shape[0]
        args += [y_s5, y_gla, w_out, norm_ffn, w_fi, w_fo]
        in_specs += [token_rows(d_s5), token_rows(y_gla.shape[1]), _const_spec(w_out.shape),
                     _const_spec(norm_ffn.shape), _const_spec(w_fi.shape), _const_spec(w_fo.shape)]
    args.append(norm)
    in_specs.append(_const_spec(norm.shape))
    if proj:
        w_in, b_gate = proj_args
        zg_w = w_in.shape[1] - d_s5
        args += [w_in, b_gate]
        in_specs += [_const_spec(w_in.shape), _const_spec(b_gate.shape)]
        out_specs = [token_rows(d_model), token_rows(d_s5), token_rows(zg_w)]
        out_shape = [jax.ShapeDtypeStruct((rows_total, d_model), F32),
                     jax.ShapeDtypeStruct((rows_total, d_s5), F32),
                     jax.ShapeDtypeStruct((rows_total, zg_w), F32)]
        if not mix:
            out_specs, out_shape = out_specs[1:], out_shape[1:]
    else:
        out_specs = token_rows(d_model)
        out_shape = jax.ShapeDtypeStruct((rows_total, d_model), F32)
    return pl.pallas_call(
        functools.partial(_dense_kernel, mix=mix, proj=proj, d_s5=d_s5, d_ff=d_ff),
        grid=(rows_total // tm,),
        in_specs=in_specs,
        out_specs=out_specs,
        out_shape=out_shape,
        compiler_params=pltpu.CompilerParams(dimension_semantics=("arbitrary",),
                                             vmem_limit_bytes=VMEM_LIMIT_BYTES),
        name="dense_" + ("mix" if mix else "") + ("proj" if proj else "final"),
    )(*args)


def kernel(x, norm_mix, w_in, s5_lam_re, s5_lam_im, s5_b_re, s5_b_im, s5_c_re, s5_c_im, s5_d, s5_log_step,
           s5_w_glu, s5_b_glu, s5_out_norm, gla_w_gate, gla_b_gate, gla_out_norm, w_out, norm_ffn, w_ffn_in,
           w_ffn_out, norm_final):
    batch, seq, d_model = x.shape
    depth = w_in.shape[0]
    d_s5 = s5_d.shape[1]
    d_gla = gla_out_norm.shape[1]
    kdim = gla_w_gate.shape[2]
    o_gate = d_s5 + 2 * kdim + 2 * d_gla
    assert batch % 8 == 0 and (batch * seq) % DENSE_ROWS == 0 and seq % GLA_CHUNK == 0 and seq % S5_STEPS == 0

    wb, wc, acoef, w_eff = _s5_prep(s5_lam_re, s5_lam_im, s5_log_step, s5_b_re, s5_b_im, s5_c_re, s5_c_im,
                                    w_in[:, :, o_gate:], gla_w_gate)
    w_in_eff = jnp.concatenate([w_in[:, :, :o_gate].astype(BF16), w_eff], axis=-1)
    w_out_b, w_fi_b, w_fo_b, w_glu_b = (w.astype(BF16) for w in (w_out, w_ffn_in, w_ffn_out, s5_w_glu))
    vec = lambda v, i: v[i][None, :]

    xf = x.reshape(batch * seq, d_model)
    u, zg = _dense(xf, None, vec(norm_mix, 0), (w_in_eff[0], vec(gla_b_gate, 0)), d_s5=d_s5)
    for i in range(depth):
        y_s5 = _s5_mixer(u.reshape(batch, seq, d_s5), wb[i], wc[i], acoef[i], vec(s5_d, i), w_glu_b[i],
                         vec(s5_b_glu, i), vec(s5_out_norm, i))
        y_gla = _gla_mixer(zg, vec(gla_out_norm, i), batch=batch, d_gla=d_gla)
        mix_args = (y_s5.reshape(batch * seq, d_s5), y_gla, w_out_b[i], vec(norm_ffn, i), w_fi_b[i], w_fo_b[i])
        if i + 1 < depth:
            xf, u, zg = _dense(xf, mix_args, vec(norm_mix, i + 1), (w_in_eff[i + 1], vec(gla_b_gate, i + 1)),
                               d_s5=d_s5)
        else:
            xf = _dense(xf, mix_args, norm_final[None, :], None, d_s5=d_s5)
    return xf.reshape(batch, seq, d_model)
```

```python
import functools

import jax
import jax.numpy as jnp
from jax import lax
from jax.experimental import pallas as pl
from jax.experimental.pallas import tpu as pltpu

S5_GROUP = 16
S5_STATE = 64
GLA_HEADS = 4
GLA_CHUNK = 64
GLA_GATE_NORM = 16.0
EPS = 1e-6

LANES = 128
MXU_DIM = 256
VMEM_LIMIT_BYTES = 56 * 1024 * 1024

DENSE_ROWS = 512
DENSE_SPLIT = 2
FFN_CHUNKS = 2
S5_STEPS = 8
GLA_ROWS = 512

BF16 = jnp.bfloat16
F32 = jnp.float32


def _dot(a, b):
    return jnp.dot(a, b, preferred_element_type=F32)


def _dot_nt(a, b):
    return lax.dot_general(a, b, (((1,), (1,)), ((), ())), preferred_element_type=F32)


def _dot_tn(a, b):
    return lax.dot_general(a, b, (((0,), (0,)), ((), ())), preferred_element_type=F32)


def _rms(x, gain):
    return x * lax.rsqrt(jnp.mean(x * x, axis=-1, keepdims=True) + EPS) * gain


def _const_spec(shape):
    return pl.BlockSpec(shape, lambda *_: (0,) * len(shape), pipeline_mode=pl.Buffered(1))


def _s5_prep_kernel(lam_re_ref, lam_im_ref, log_step_ref, b_re_ref, b_im_ref, c_re_ref, c_im_ref,
                    win_gate_ref, w_gate_ref, wb_ref, wc_ref, acoef_ref, weff_ref):
    n_state = lam_re_ref.shape[-1]
    d_s5 = b_re_ref.shape[1]
    lr = lam_re_ref[0]
    li = lam_im_ref[0]
    step = jnp.exp(log_step_ref[0])
    mag = jnp.exp(lr * step)
    ar = mag * jnp.cos(li * step)
    ai = mag * jnp.sin(li * step)
    den = lr * lr + li * li
    fr = ((ar - 1.0) * lr + ai * li) / den
    fi = (ai * lr - (ar - 1.0) * li) / den
    acoef_ref[0, 0:1, :] = ar
    acoef_ref[0, 1:2, :] = ai

    row_g = lax.broadcasted_iota(jnp.int32, (d_s5, n_state), 0) // S5_GROUP
    col_g = lax.broadcasted_iota(jnp.int32, (d_s5, n_state), 1) // S5_STATE
    diag = row_g == col_g
    br = b_re_ref[0]
    bi = b_im_ref[0]
    wb_ref[0, :, 0:n_state] = jnp.where(diag, fr * br - fi * bi, 0.0).astype(BF16)
    wb_ref[0, :, n_state:2 * n_state] = jnp.where(diag, fr * bi + fi * br, 0.0).astype(BF16)

    row_g = lax.broadcasted_iota(jnp.int32, (n_state, d_s5), 0) // S5_STATE
    col_g = lax.broadcasted_iota(jnp.int32, (n_state, d_s5), 1) // S5_GROUP
    diag = row_g == col_g
    wc_ref[0, 0:n_state, :] = jnp.where(diag, c_re_ref[0], 0.0).astype(BF16)
    wc_ref[0, n_state:2 * n_state, :] = jnp.where(diag, -c_im_ref[0], 0.0).astype(BF16)

    a = win_gate_ref[0]
    b = w_gate_ref[0]
    a_hi = a.astype(BF16)
    a_lo = (a - a_hi.astype(F32)).astype(BF16)
    b_hi = b.astype(BF16)
    b_lo = (b - b_hi.astype(F32)).astype(BF16)
    weff_ref[0] = (_dot(a_hi, b_hi) + _dot(a_hi, b_lo) + _dot(a_lo, b_hi)).astype(BF16)


def _s5_prep(lam_re, lam_im, log_step, b_re, b_im, c_re, c_im, win_gate, w_gate):
    depth, groups, states = lam_re.shape
    n_state = groups * states
    d_s5 = groups * S5_GROUP
    d_model = win_gate.shape[1]
    rank = win_gate.shape[2]
    kdim = w_gate.shape[2]
    row = lambda t: t.reshape(depth, 1, n_state)
    step = row(jnp.broadcast_to(log_step[:, :, None], (depth, groups, states)))
    tile_b = lambda t: jnp.tile(t.transpose(0, 3, 1, 2).reshape(depth, S5_GROUP, n_state), (1, groups, 1))
    tile_c = lambda t: jnp.tile(t.transpose(0, 1, 3, 2).reshape(depth, n_state, S5_GROUP), (1, 1, groups))
    per_layer = lambda *s: pl.BlockSpec((1,) + s, lambda i: (i,) + (0,) * len(s))
    return pl.pallas_call(
        _s5_prep_kernel,
        grid=(depth,),
        in_specs=[per_layer(1, n_state)] * 3 + [per_layer(d_s5, n_state)] * 2 + [per_layer(n_state, d_s5)] * 2
        + [per_layer(d_model, rank), per_layer(rank, kdim)],
        out_specs=[per_layer(d_s5, 2 * n_state), per_layer(2 * n_state, d_s5), per_layer(2, n_state),
                   per_layer(d_model, kdim)],
        out_shape=[jax.ShapeDtypeStruct((depth, d_s5, 2 * n_state), BF16),
                   jax.ShapeDtypeStruct((depth, 2 * n_state, d_s5), BF16),
                   jax.ShapeDtypeStruct((depth, 2, n_state), F32),
                   jax.ShapeDtypeStruct((depth, d_model, kdim), BF16)],
        compiler_params=pltpu.CompilerParams(dimension_semantics=("arbitrary",),
                                             vmem_limit_bytes=VMEM_LIMIT_BYTES),
        name="s5_prep",
    )(row(lam_re), row(lam_im), step, tile_b(b_re), tile_b(b_im), tile_c(c_re), tile_c(c_im),
      win_gate, w_gate)


def _s5_kernel(u_ref, wb_ref, wc_ref, acoef_ref, d_ref, wglu_ref, bglu_ref, onorm_ref, y_ref,
               xs_ref, carry_ref, perm_ref):
    batch, steps, d_s5 = u_ref.shape
    rows = batch * steps
    n_state = acoef_ref.shape[1]
    half_c = d_s5 // 2
    half_s = n_state // 2

    @pl.when(pl.program_id(0) == 0)
    def _():
        carry_ref[...] = jnp.zeros_like(carry_ref)
        dst = lax.broadcasted_iota(jnp.int32, (rows, rows), 0)
        src = lax.broadcasted_iota(jnp.int32, (rows, rows), 1)
        perm_ref[0] = jnp.where(src == (dst % batch) * steps + dst // batch, 1.0, 0.0).astype(BF16)
        perm_ref[1] = jnp.where(src == (dst % steps) * batch + dst // steps, 1.0, 0.0).astype(BF16)

    u = u_ref[...].reshape(rows, d_s5)
    ub = _dot(perm_ref[0], u.astype(BF16)).astype(BF16)
    for part in range(2):
        for half in range(2):
            cols = pl.ds(part * n_state + half * half_s, half_s)
            xs_ref[:, cols] = _dot(ub[:, half * half_c:(half + 1) * half_c],
                                   wb_ref[pl.ds(half * half_c, half_c), cols])

    for j in range(n_state // LANES):
        re = pl.ds(j * LANES, LANES)
        im = pl.ds(n_state + j * LANES, LANES)
        ar = jnp.broadcast_to(acoef_ref[0:1, re], (batch, LANES))
        ai = jnp.broadcast_to(acoef_ref[1:2, re], (batch, LANES))
        sr = carry_ref[:, re]
        si = carry_ref[:, im]
        for t in range(steps):
            r = pl.ds(t * batch, batch)
            sr, si = (ar * sr - ai * si + xs_ref[r, re],
                      ar * si + ai * sr + xs_ref[r, im])
            xs_ref[r, re] = sr
            xs_ref[r, im] = si
        carry_ref[:, re] = sr
        carry_ref[:, im] = si

    ys = []
    for half in range(2):
        cols = pl.ds(half * half_c, half_c)
        acc = None
        for part in range(2):
            srows = pl.ds(part * n_state + half * half_s, half_s)
            term = _dot(xs_ref[:, srows].astype(BF16), wc_ref[srows, cols])
            acc = term if acc is None else acc + term
        ys.append(acc)
    y = jnp.concatenate(ys, axis=-1)
    y_hi = y.astype(BF16)
    y_lo = (y - y_hi.astype(F32)).astype(BF16)
    y2 = _dot(perm_ref[1], jnp.concatenate([y_hi, y_lo], axis=-1))
    y = y2[:, :d_s5] + y2[:, d_s5:] + d_ref[...] * u
    y = jax.nn.gelu(y)
    y = y * jax.nn.sigmoid(_dot(y.astype(BF16), wglu_ref[...]) + bglu_ref[...])
    y_ref[...] = _rms(y, onorm_ref[...]).reshape(batch, steps, d_s5)


def _s5_mixer(u, wb, wc, acoef, d_skip, wglu, bglu, onorm):
    batch, seq, d_s5 = u.shape
    n_state2 = wb.shape[1]
    rows = S5_STEPS * batch
    block = pl.BlockSpec((batch, S5_STEPS, d_s5), lambda t: (0, t, 0))
    return pl.pallas_call(
        _s5_kernel,
        grid=(seq // S5_STEPS,),
        in_specs=[block, _const_spec(wb.shape), _const_spec(wc.shape), _const_spec(acoef.shape),
                  _const_spec(d_skip.shape), _const_spec(wglu.shape), _const_spec(bglu.shape),
                  _const_spec(onorm.shape)],
        out_specs=block,
        out_shape=jax.ShapeDtypeStruct((batch, seq, d_s5), F32),
        scratch_shapes=[pltpu.VMEM((rows, n_state2), F32), pltpu.VMEM((batch, n_state2), F32),
                        pltpu.VMEM((2, rows, rows), BF16)],
        compiler_params=pltpu.CompilerParams(dimension_semantics=("arbitrary",),
                                             vmem_limit_bytes=VMEM_LIMIT_BYTES),
        name="s5_mixer",
    )(u, wb, wc, acoef, d_skip, wglu, bglu, onorm)


def _log_sigmoid(x):
    return jnp.minimum(x, 0.0) - jnp.log1p(jnp.exp(-jnp.abs(x)))


def _gla_kernel(zg_ref, la_ref, onorm_ref, y_ref, st_ref):
    rows = zg_ref.shape[0]
    d_gla = y_ref.shape[1]
    dv = d_gla // GLA_HEADS
    dk = dv // 2
    kdim = GLA_HEADS * dk
    o_k, o_v, o_g = kdim, 2 * kdim, 2 * kdim + d_gla
    c = GLA_CHUNK
    assert 2 * dk == LANES and dv == LANES

    @pl.when(pl.program_id(1) == 0)
    def _():
        st_ref[...] = jnp.zeros_like(st_ref)

    r_i = lax.broadcasted_iota(jnp.int32, (c, 3 * c), 0)
    c_i = lax.broadcasted_iota(jnp.int32, (c, 3 * c), 1)
    tri3 = jnp.where((c_i % c) <= r_i, 1.0, 0.0).astype(BF16)
    causal = (lax.broadcasted_iota(jnp.int32, (c, LANES), 0)
              >= lax.broadcasted_iota(jnp.int32, (c, LANES), 1) % c)
    low_head = lax.broadcasted_iota(jnp.int32, (c, LANES), 1) < dk
    own_block = ((lax.broadcasted_iota(jnp.int32, (2 * dv, LANES), 0) < dv)
                 == (lax.broadcasted_iota(jnp.int32, (2 * dv, LANES), 1) < dk))
    n_chunks = rows // c
    pairs = GLA_HEADS // 2
    rows_of = lambda t, n: t[n * c:(n + 1) * c]
    zeros_v = jnp.zeros((c, dv), BF16)

    la = la_ref[...]
    la_hi = la.astype(BF16)
    rem = la - la_hi.astype(F32)
    la_mid = rem.astype(BF16)
    la_lo = (rem - la_mid.astype(F32)).astype(BF16)
    cums = [_dot(tri3, jnp.concatenate([rows_of(la_hi, n), rows_of(la_mid, n), rows_of(la_lo, n)], axis=0))
            for n in range(n_chunks)]
    cum = jnp.concatenate(cums, axis=0)
    totals = [t[c - 1:c, :] for t in cums]
    total = jnp.concatenate([jnp.broadcast_to(t, (c, kdim)) for t in totals], axis=0)
    q_dec = (zg_ref[:, 0:kdim].astype(F32) * (dk ** -0.5) * jnp.exp(cum)).astype(BF16)
    k = zg_ref[:, o_k:o_k + kdim].astype(F32)
    k_inv = k * jnp.exp(-cum)
    k_end = (k * jnp.exp(total - cum)).astype(BF16)
    v = zg_ref[:, o_v:o_v + d_gla]

    def lanes_of(t, n, pair):
        return t[n * c:(n + 1) * c, pair * LANES:(pair + 1) * LANES]

    scores = []
    for n in range(n_chunks):
        for pair in range(pairs):
            kp = lanes_of(k_inv, n, pair)
            keys = jnp.concatenate([jnp.where(low_head, kp, 0.0), jnp.where(low_head, 0.0, kp)], axis=0)
            s = _dot_nt(lanes_of(q_dec, n, pair), keys.astype(BF16))
            scores.append(jnp.where(causal, s, 0.0).astype(BF16))

    intra, upds = [], []
    for n in range(n_chunks):
        for pair in range(pairs):
            v_even = v[n * c:(n + 1) * c, (2 * pair) * dv:(2 * pair + 1) * dv]
            v_odd = v[n * c:(n + 1) * c, (2 * pair + 1) * dv:(2 * pair + 2) * dv]
            v_diag = jnp.concatenate([jnp.concatenate([v_even, zeros_v], axis=1),
                                      jnp.concatenate([zeros_v, v_odd], axis=1)], axis=0)
            intra.append(_dot(scores[n * pairs + pair], v_diag))
            upd = _dot_tn(jnp.concatenate([v_even, v_odd], axis=1), lanes_of(k_end, n, pair))
            upds.append(jnp.where(own_block, upd, 0.0))

    outs = [[None] * pairs for _ in range(n_chunks)]
    for pair in range(pairs):
        st = st_ref[pair]
        for n in range(n_chunks):
            outs[n][pair] = intra[n * pairs + pair] + _dot_nt(lanes_of(q_dec, n, pair), st.astype(BF16))
            decay = jnp.exp(totals[n][:, pair * LANES:(pair + 1) * LANES])
            st = st * decay + upds[n * pairs + pair]
        st_ref[pair] = st

    for h in range(GLA_HEADS):
        o = jnp.concatenate([outs[n][h // 2][:, (h % 2) * dv:(h % 2 + 1) * dv] for n in range(n_chunks)], axis=0)
        o = o * lax.rsqrt(jnp.mean(o * o, axis=-1, keepdims=True) + EPS)
        g = zg_ref[:, o_g + h * dv:o_g + (h + 1) * dv].astype(F32)
        y = jax.nn.silu(g) * (o * onorm_ref[:, h * dv:(h + 1) * dv])
        y_ref[:, h * dv:(h + 1) * dv] = y.astype(y_ref.dtype)


def _gla_mixer(zg, la, onorm, *, batch, d_gla):
    rows_total = zg.shape[0]
    seq = rows_total // batch
    rows = min(GLA_ROWS, seq)
    n_t = seq // rows
    token_rows = lambda w: pl.BlockSpec((rows, w), lambda b, t: (b * n_t + t, 0))
    return pl.pallas_call(
        _gla_kernel,
        grid=(batch, n_t),
        in_specs=[token_rows(zg.shape[1]), token_rows(la.shape[1]), _const_spec(onorm.shape)],
        out_specs=token_rows(d_gla),
        out_shape=jax.ShapeDtypeStruct((rows_total, d_gla), BF16),
        scratch_shapes=[pltpu.VMEM((GLA_HEADS // 2, 2 * d_gla // GLA_HEADS, LANES), F32)],
        compiler_params=pltpu.CompilerParams(dimension_semantics=("arbitrary", "arbitrary"),
                                             vmem_limit_bytes=VMEM_LIMIT_BYTES),
        name="gla_mixer",
    )(zg, la, onorm)


def _dense_kernel(*refs, mix, proj, d_s5, d_ff):
    refs = list(refs)
    x_ref = refs.pop(0)
    sub = x_ref.shape[0] // DENSE_SPLIT
    groups = [pl.ds(s * sub, sub) for s in range(DENSE_SPLIT)]
    xs = [x_ref[r, :] for r in groups]
    if mix:
        ys5_ref, ygla_ref, wout_ref, nffn_ref, wfi_ref, wfo_ref = refs[:6]
        refs = refs[6:]
        xs = [x + _dot(ys5_ref[r, :].astype(BF16), wout_ref[0:d_s5, :]) + _dot(ygla_ref[r, :], wout_ref[d_s5:, :])
              for x, r in zip(xs, groups)]
        hs = [_rms(x, nffn_ref[...]).astype(BF16) for x in xs]
        tiles = d_ff // MXU_DIM
        bounds = [MXU_DIM * ((tiles * c + FFN_CHUNKS - 1) // FFN_CHUNKS) for c in range(FFN_CHUNKS + 1)]
        acts = [[] for _ in range(DENSE_SPLIT)]
        for lo, hi in zip(bounds[:-1], bounds[1:]):
            for s in range(DENSE_SPLIT):
                gate = _dot(hs[s], wfi_ref[:, lo:hi])
                up = _dot(hs[s], wfi_ref[:, d_ff + lo:d_ff + hi])
                acts[s].append((jax.nn.silu(gate) * up).astype(BF16))
        xs = [x + _dot(jnp.concatenate(a, axis=1), wfo_ref[...]) for x, a in zip(xs, acts)]
    norm_ref = refs.pop(0)
    hs = [_rms(x, norm_ref[...]) for x in xs]
    if not proj:
        (out_ref,) = refs
        for h, r in zip(hs, groups):
            out_ref[r, :] = h
        return
    win_ref, bgate_ref = refs[:2]
    u_ref, zg_ref, la_ref = refs[-3:]
    o_a = d_s5 + zg_ref.shape[1]
    for x, h, r in zip(xs, hs, groups):
        if mix:
            refs[2][r, :] = x
        z = _dot(h.astype(BF16), win_ref[...])
        u_ref[r, :] = z[:, :d_s5]
        zg_ref[r, :] = z[:, d_s5:o_a].astype(BF16)
        la_ref[r, :] = _log_sigmoid(z[:, o_a:] + bgate_ref[...]) * (1.0 / GLA_GATE_NORM)


def _dense(x, mix_args, norm, proj_args, *, d_s5):
    rows_total, d_model = x.shape
    tm = DENSE_ROWS
    token_rows = lambda w: pl.BlockSpec((tm, w), lambda r: (r, 0))
    mix, proj = mix_args is not None, proj_args is not None
    args, in_specs = [x], [token_rows(d_model)]
    d_ff = 0
    if mix:
        y_s5, y_gla, w_out, norm_ffn, w_fi, w_fo = mix_args
        d_ff = w_fo.shape[0]
        assert d_ff % MXU_DIM == 0
        args += [y_s5, y_gla, w_out, norm_ffn, w_fi, w_fo]
        in_specs += [token_rows(d_s5), token_rows(y_gla.shape[1]), _const_spec(w_out.shape),
                     _const_spec(norm_ffn.shape), _const_spec(w_fi.shape), _const_spec(w_fo.shape)]
    args.append(norm)
    in_specs.append(_const_spec(norm.shape))
    if proj:
        w_in, b_gate = proj_args
        la_w = b_gate.shape[1]
        zg_w = w_in.shape[1] - d_s5 - la_w
        args += [w_in, b_gate]
        in_specs += [_const_spec(w_in.shape), _const_spec(b_gate.shape)]
        out_specs = [token_rows(d_model), token_rows(d_s5), token_rows(zg_w), token_rows(la_w)]
        out_shape = [jax.ShapeDtypeStruct((rows_total, d_model), F32),
                     jax.ShapeDtypeStruct((rows_total, d_s5), F32),
                     jax.ShapeDtypeStruct((rows_total, zg_w), BF16),
                     jax.ShapeDtypeStruct((rows_total, la_w), F32)]
        if not mix:
            out_specs, out_shape = out_specs[1:], out_shape[1:]
    else:
        out_specs = token_rows(d_model)
        out_shape = jax.ShapeDtypeStruct((rows_total, d_model), F32)
    return pl.pallas_call(
        functools.partial(_dense_kernel, mix=mix, proj=proj, d_s5=d_s5, d_ff=d_ff),
        grid=(rows_total // tm,),
        in_specs=in_specs,
        out_specs=out_specs,
        out_shape=out_shape,
        compiler_params=pltpu.CompilerParams(dimension_semantics=("arbitrary",),
                                             vmem_limit_bytes=VMEM_LIMIT_BYTES),
        name="dense_" + ("mix" if mix else "") + ("proj" if proj else "final"),
    )(*args)


def kernel(x, norm_mix, w_in, s5_lam_re, s5_lam_im, s5_b_re, s5_b_im, s5_c_re, s5_c_im, s5_d, s5_log_step,
           s5_w_glu, s5_b_glu, s5_out_norm, gla_w_gate, gla_b_gate, gla_out_norm, w_out, norm_ffn, w_ffn_in,
           w_ffn_out, norm_final):
    batch, seq, d_model = x.shape
    depth = w_in.shape[0]
    d_s5 = s5_d.shape[1]
    d_gla = gla_out_norm.shape[1]
    kdim = gla_w_gate.shape[2]
    o_gate = d_s5 + 2 * kdim + 2 * d_gla
    assert batch % 8 == 0 and (batch * seq) % DENSE_ROWS == 0 and seq % GLA_CHUNK == 0 and seq % S5_STEPS == 0

    wb, wc, acoef, w_eff = _s5_prep(s5_lam_re, s5_lam_im, s5_log_step, s5_b_re, s5_b_im, s5_c_re, s5_c_im,
                                    w_in[:, :, o_gate:], gla_w_gate)
    w_in_eff = jnp.concatenate([w_in[:, :, :o_gate].astype(BF16), w_eff], axis=-1)
    w_out_b, w_fi_b, w_fo_b, w_glu_b = (w.astype(BF16) for w in (w_out, w_ffn_in, w_ffn_out, s5_w_glu))
    vec = lambda v, i: v[i][None, :]

    xf = x.reshape(batch * seq, d_model)
    u, zg, la = _dense(xf, None, vec(norm_mix, 0), (w_in_eff[0], vec(gla_b_gate, 0)), d_s5=d_s5)
    for i in range(depth):
        y_s5 = _s5_mixer(u.reshape(batch, seq, d_s5), wb[i], wc[i], acoef[i], vec(s5_d, i), w_glu_b[i],
                         vec(s5_b_glu, i), vec(s5_out_norm, i))
        y_gla = _gla_mixer(zg, la, vec(gla_out_norm, i), batch=batch, d_gla=d_gla)
        mix_args = (y_s5.reshape(batch * seq, d_s5), y_gla, w_out_b[i], vec(norm_ffn, i), w_fi_b[i], w_fo_b[i])
        if i + 1 < depth:
            xf, u, zg, la = _dense(xf, mix_args, vec(norm_mix, i + 1), (w_in_eff[i + 1], vec(gla_b_gate, i + 1)),
                                   d_s5=d_s5)
        else:
            xf = _dense(xf, mix_args, norm_final[None, :], None, d_s5=d_s5)
    return xf.reshape(batch, seq, d_model)
```

```python
import functools

import jax
import jax.numpy as jnp
from jax import lax
from jax.experimental import pallas as pl
from jax.experimental.pallas import tpu as pltpu

S5_GROUP = 16
S5_STATE = 64
GLA_HEADS = 4
GLA_CHUNK = 64
GLA_GATE_NORM = 16.0
EPS = 1e-6

LANES = 128
MXU_DIM = 256
VMEM_LIMIT_BYTES = 56 * 1024 * 1024

DENSE_ROWS = 512
DENSE_SPLIT = 2
FFN_CHUNKS = 2
S5_CHUNK = 16
S5_CHUNKS = 8
S5_FINISH_ROWS = 512
GLA_ROWS = 512

BF16 = jnp.bfloat16
F32 = jnp.float32


def _dot(a, b):
    return jnp.dot(a, b, preferred_element_type=F32)


def _dot_nt(a, b):
    return lax.dot_general(a, b, (((1,), (1,)), ((), ())), preferred_element_type=F32)


def _dot_tn(a, b):
    return lax.dot_general(a, b, (((0,), (0,)), ((), ())), preferred_element_type=F32)


def _rms(x, gain):
    return x * lax.rsqrt(jnp.mean(x * x, axis=-1, keepdims=True) + EPS) * gain


def _const_spec(shape):
    return pl.BlockSpec(shape, lambda *_: (0,) * len(shape), pipeline_mode=pl.Buffered(1))


def _split3(a):
    hi = a.astype(BF16)
    rem = a - hi.astype(F32)
    mid = rem.astype(BF16)
    return hi, mid, (rem - mid.astype(F32)).astype(BF16)


def _dot_f32(a, b):
    a0, a1, a2 = _split3(a)
    b0, b1, b2 = _split3(b)
    return (_dot(a0, b0) + _dot(a0, b1) + _dot(a1, b0)) + (_dot(a0, b2) + _dot(a1, b1) + _dot(a2, b0))


def _gate_fold_kernel(win_gate_ref, w_gate_ref, weff_ref):
    weff_ref[0] = _dot_f32(win_gate_ref[0], w_gate_ref[0]).astype(BF16)


def _gate_fold(win_gate, w_gate):
    depth, d_model, rank = win_gate.shape
    kdim = w_gate.shape[2]
    per_layer = lambda *s: pl.BlockSpec((1,) + s, lambda i: (i,) + (0,) * len(s))
    return pl.pallas_call(
        _gate_fold_kernel,
        grid=(depth,),
        in_specs=[per_layer(d_model, rank), per_layer(rank, kdim)],
        out_specs=per_layer(d_model, kdim),
        out_shape=jax.ShapeDtypeStruct((depth, d_model, kdim), BF16),
        compiler_params=pltpu.CompilerParams(dimension_semantics=("arbitrary",)),
        name="gate_fold",
    )(win_gate, w_gate)


def _cmul(ar, ai, br, bi):
    return ar * br - ai * bi, ar * bi + ai * br


def _discretise(lr, li, log_step):
    step = jnp.exp(log_step)
    mag = jnp.exp(lr * step)
    ar = mag * jnp.cos(li * step)
    ai = mag * jnp.sin(li * step)
    den = lr * lr + li * li
    fr = ((ar - 1.0) * lr + ai * li) / den
    fi = (ai * lr - (ar - 1.0) * li) / den
    return ar, ai, fr, fi


def _s5_prep_kernel(lam_re_row_ref, lam_im_row_ref, lam_re_col_ref, lam_im_col_ref, log_step_ref,
                    c_re_ref, c_im_ref, b_re_ref, b_im_ref, m_ref, e_ref, f_re_ref, f_im_ref, apow_ref):
    t_c = S5_CHUNK
    h_c = S5_GROUP
    p_c = lam_re_row_ref.shape[-1]
    log_step = log_step_ref[0, 0]
    ar, ai, _, _ = _discretise(lam_re_row_ref[0, 0], lam_im_row_ref[0, 0], log_step)
    arc, aic, frc, fic = _discretise(lam_re_col_ref[0, 0], lam_im_col_ref[0, 0], log_step)

    pr = jnp.ones((h_c, p_c), F32)
    pi = jnp.zeros((h_c, p_c), F32)
    blocks_r, blocks_i = [], []
    for _ in range(t_c):
        blocks_r.append(pr)
        blocks_i.append(pi)
        pr, pi = _cmul(pr, pi, ar, ai)
    l_re, l_im = _cmul(c_re_ref[0, 0], c_im_ref[0, 0], jnp.concatenate(blocks_r, axis=0),
                       jnp.concatenate(blocks_i, axis=0))
    f_re, f_im = _cmul(l_re, l_im, ar, ai)
    f_re_ref[0, 0] = f_re.astype(BF16)
    f_im_ref[0, 0] = (-f_im).astype(BF16)

    bw_re, bw_im = _cmul(frc, fic, b_re_ref[0, 0], b_im_ref[0, 0])
    k_wide = _dot_f32(l_re, bw_re) - _dot_f32(l_im, bw_im)
    lane_blk = lax.broadcasted_iota(jnp.int32, (t_c * h_c, t_c * h_c), 1) // h_c
    m = jnp.zeros((t_c * h_c, t_c * h_c), F32)
    for j in range(t_c):
        moved = k_wide if j == 0 else jnp.concatenate(
            [jnp.zeros((j * h_c, t_c * h_c), F32), k_wide[:(t_c - j) * h_c]], axis=0)
        m = jnp.where(lane_blk == j, moved, m)
    m_ref[0, 0] = m.astype(BF16)

    lane_blk = lax.broadcasted_iota(jnp.int32, (p_c, t_c * h_c), 1) // h_c
    qr = jnp.ones((p_c, 1), F32)
    qi = jnp.zeros((p_c, 1), F32)
    pe_r = jnp.zeros((p_c, t_c * h_c), F32)
    pe_i = jnp.zeros((p_c, t_c * h_c), F32)
    for j in range(t_c - 1, -1, -1):
        pe_r = jnp.where(lane_blk == j, qr, pe_r)
        pe_i = jnp.where(lane_blk == j, qi, pe_i)
        qr, qi = _cmul(qr, qi, arc, aic)
    e_re, e_im = _cmul(pe_r, pe_i, bw_re, bw_im)
    e_ref[0, 0] = jnp.concatenate([e_re, e_im], axis=0).astype(BF16)

    for k in range(3):
        apow_ref[0, 0, k, 0] = jnp.broadcast_to(qr, (p_c, LANES))
        apow_ref[0, 0, k, 1] = jnp.broadcast_to(qi, (p_c, LANES))
        qr, qi = _cmul(qr, qi, qr, qi)


def _s5_prep(lam_re, lam_im, log_step, b_re, b_im, c_re, c_im):
    depth, groups, states = lam_re.shape
    t_c, h_c = S5_CHUNK, S5_GROUP
    row = lambda t: t.reshape(depth, groups, 1, states)
    col = lambda t: t.reshape(depth, groups, states, 1)
    c_rows = lambda t: jnp.tile(t, (1, 1, t_c, 1))
    b_lanes = lambda t: jnp.tile(t, (1, 1, 1, t_c))
    blk = lambda *s: pl.BlockSpec((1, 1) + s, lambda i, g: (i, g) + (0,) * len(s))
    w = t_c * h_c
    return pl.pallas_call(
        _s5_prep_kernel,
        grid=(depth, groups),
        in_specs=[blk(1, states)] * 2 + [blk(states, 1)] * 2 + [blk(1, 1)]
        + [blk(w, states)] * 2 + [blk(states, w)] * 2,
        out_specs=[blk(w, w), blk(2 * states, w), blk(w, states), blk(w, states), blk(3, 2, states, LANES)],
        out_shape=[jax.ShapeDtypeStruct((depth, groups, w, w), BF16),
                   jax.ShapeDtypeStruct((depth, groups, 2 * states, w), BF16),
                   jax.ShapeDtypeStruct((depth, groups, w, states), BF16),
                   jax.ShapeDtypeStruct((depth, groups, w, states), BF16),
                   jax.ShapeDtypeStruct((depth, groups, 3, 2, states, LANES), F32)],
        compiler_params=pltpu.CompilerParams(dimension_semantics=("arbitrary", "arbitrary")),
        name="s5_prep",
    )(row(lam_re), row(lam_im), col(lam_re), col(lam_im), log_step.reshape(depth, groups, 1, 1),
      c_rows(c_re), c_rows(c_im), b_lanes(b_re), b_lanes(b_im))


def _s5_kernel(u_ref, m_ref, e_ref, f_ref, apow_ref, d_ref, wglu_ref, bglu_ref, onorm_ref, y_ref,
               ub_ref, xt_ref, yt_ref, y2_ref, z_ref, s_ref, carry_ref, perm_ref):
    n_b, t_blk, d_s5 = u_ref.shape
    groups = m_ref.shape[0]
    t_c, h_c = S5_CHUNK, S5_GROUP
    n_c = t_blk // t_c
    p_c = f_ref.shape[2] // 2
    samples = n_b * n_c
    assert samples == LANES and n_c == 8

    @pl.when(pl.program_id(1) == 0)
    def _():
        carry_ref[...] = jnp.zeros_like(carry_ref)

    @pl.when((pl.program_id(0) == 0) & (pl.program_id(1) == 0))
    def _():
        dst = lax.broadcasted_iota(jnp.int32, (t_blk, t_blk), 0)
        src = lax.broadcasted_iota(jnp.int32, (t_blk, t_blk), 1)
        perm_ref[0] = jnp.where(src == (dst % n_c) * t_c + dst // n_c, 1.0, 0.0).astype(BF16)
        perm_ref[1] = jnp.where(src == (dst % t_c) * n_c + dst // t_c, 1.0, 0.0).astype(BF16)

    for b in range(n_b):
        ub_ref[b] = _dot(perm_ref[0], u_ref[b].astype(BF16))
    for t in range(t_c):
        xt_ref[t] = ub_ref[:, t * n_c:(t + 1) * n_c, :].reshape(samples, d_s5).T
    for g in range(groups):
        a = xt_ref[:, g * h_c:(g + 1) * h_c, :].reshape(t_c * h_c, samples).astype(BF16)
        yt_ref[g] = _dot(m_ref[g], a)
        z_ref[g] = _dot(e_ref[g], a)

    c_idx = lax.broadcasted_iota(jnp.int32, (p_c, LANES), 1) % n_c
    for g in range(groups):
        z_re, z_im = z_ref[g, 0:p_c], z_ref[g, p_c:2 * p_c]
        v_re = jnp.where(c_idx == 0, carry_ref[g, 0:p_c], pltpu.roll(z_re, 1, 1))
        v_im = jnp.where(c_idx == 0, carry_ref[g, p_c:2 * p_c], pltpu.roll(z_im, 1, 1))
        for k in range(3):
            hop = 1 << k
            sh_re = jnp.where(c_idx >= hop, pltpu.roll(v_re, hop, 1), 0.0)
            sh_im = jnp.where(c_idx >= hop, pltpu.roll(v_im, hop, 1), 0.0)
            d_re, d_im = _cmul(apow_ref[g, k, 0], apow_ref[g, k, 1], sh_re, sh_im)
            v_re, v_im = v_re + d_re, v_im + d_im
        s_ref[g, 0:p_c] = v_re.astype(BF16)
        s_ref[g, p_c:2 * p_c] = v_im.astype(BF16)
        o_re, o_im = _cmul(apow_ref[g, 0, 0], apow_ref[g, 0, 1], v_re, v_im)
        carry_ref[g, 0:p_c] = pltpu.roll(o_re + z_re, LANES - (n_c - 1), 1)
        carry_ref[g, p_c:2 * p_c] = pltpu.roll(o_im + z_im, LANES - (n_c - 1), 1)

    for g in range(groups):
        yt_ref[g] += _dot(f_ref[g], s_ref[g])
    for t in range(t_c):
        y2_ref[t] = yt_ref[:, t * h_c:(t + 1) * h_c, :].reshape(d_s5, samples).T
    for b in range(n_b):
        y = y2_ref[:, b * n_c:(b + 1) * n_c, :].reshape(t_blk, d_s5)
        y_hi = y.astype(BF16)
        y_lo = (y - y_hi.astype(F32)).astype(BF16)
        y2 = _dot(perm_ref[1], jnp.concatenate([y_hi, y_lo], axis=-1))
        ub_ref[b] = y2[:, :d_s5] + y2[:, d_s5:] + d_ref[...] * u_ref[b]
    fin = S5_FINISH_ROWS // t_blk
    for q in range(n_b // fin):
        y = ub_ref[q * fin:(q + 1) * fin].reshape(S5_FINISH_ROWS, d_s5)
        y = jax.nn.gelu(y)
        y = y * jax.nn.sigmoid(_dot(y.astype(BF16), wglu_ref[...]) + bglu_ref[...])
        y_ref[q * fin:(q + 1) * fin] = _rms(y, onorm_ref[...]).reshape(fin, t_blk, d_s5)


def _s5_mixer(u, m, e, f, apow, d_skip, wglu, bglu, onorm):
    batch, seq, d_s5 = u.shape
    groups, w = m.shape[0], m.shape[1]
    states = f.shape[2] // 2
    t_blk = S5_CHUNK * S5_CHUNKS
    n_b = LANES // S5_CHUNKS
    assert batch % n_b == 0 and seq % t_blk == 0
    block = pl.BlockSpec((n_b, t_blk, d_s5), lambda bb, t: (bb, t, 0))
    return pl.pallas_call(
        _s5_kernel,
        grid=(batch // n_b, seq // t_blk),
        in_specs=[block, _const_spec(m.shape), _const_spec(e.shape), _const_spec(f.shape), _const_spec(apow.shape),
                  _const_spec(d_skip.shape), _const_spec(wglu.shape), _const_spec(bglu.shape),
                  _const_spec(onorm.shape)],
        out_specs=block,
        out_shape=jax.ShapeDtypeStruct((batch, seq, d_s5), F32),
        scratch_shapes=[pltpu.VMEM((n_b, t_blk, d_s5), F32),
                        pltpu.VMEM((S5_CHUNK, d_s5, LANES), F32),
                        pltpu.VMEM((groups, w, LANES), F32),
                        pltpu.VMEM((S5_CHUNK, LANES, d_s5), F32),
                        pltpu.VMEM((groups, 2 * states, LANES), F32),
                        pltpu.VMEM((groups, 2 * states, LANES), BF16),
                        pltpu.VMEM((groups, 2 * states, LANES), F32),
                        pltpu.VMEM((2, t_blk, t_blk), BF16)],
        compiler_params=pltpu.CompilerParams(dimension_semantics=("arbitrary", "arbitrary"),
                                             vmem_limit_bytes=VMEM_LIMIT_BYTES),
        name="s5_mixer",
    )(u, m, e, f, apow, d_skip, wglu, bglu, onorm)


def _log_sigmoid(x):
    return jnp.minimum(x, 0.0) - jnp.log1p(jnp.exp(-jnp.abs(x)))


def _gla_kernel(zg_ref, la_ref, onorm_ref, y_ref, st_ref):
    rows = zg_ref.shape[0]
    d_gla = y_ref.shape[1]
    dv = d_gla // GLA_HEADS
    dk = dv // 2
    kdim = GLA_HEADS * dk
    o_k, o_v, o_g = kdim, 2 * kdim, 2 * kdim + d_gla
    c = GLA_CHUNK
    assert 2 * dk == LANES and dv == LANES

    @pl.when(pl.program_id(1) == 0)
    def _():
        st_ref[...] = jnp.zeros_like(st_ref)

    r_i = lax.broadcasted_iota(jnp.int32, (c, 3 * c), 0)
    c_i = lax.broadcasted_iota(jnp.int32, (c, 3 * c), 1)
    tri3 = jnp.where((c_i % c) <= r_i, 1.0, 0.0).astype(BF16)
    causal = (lax.broadcasted_iota(jnp.int32, (c, LANES), 0)
              >= lax.broadcasted_iota(jnp.int32, (c, LANES), 1) % c)
    low_head = lax.broadcasted_iota(jnp.int32, (c, LANES), 1) < dk
    own_block = ((lax.broadcasted_iota(jnp.int32, (2 * dv, LANES), 0) < dv)
                 == (lax.broadcasted_iota(jnp.int32, (2 * dv, LANES), 1) < dk))
    n_chunks = rows // c
    pairs = GLA_HEADS // 2
    rows_of = lambda t, n: t[n * c:(n + 1) * c]
    zeros_v = jnp.zeros((c, dv), BF16)

    la = la_ref[...]
    la_hi = la.astype(BF16)
    rem = la - la_hi.astype(F32)
    la_mid = rem.astype(BF16)
    la_lo = (rem - la_mid.astype(F32)).astype(BF16)
    cums = [_dot(tri3, jnp.concatenate([rows_of(la_hi, n), rows_of(la_mid, n), rows_of(la_lo, n)], axis=0))
            for n in range(n_chunks)]
    cum = jnp.concatenate(cums, axis=0)
    totals = [t[c - 1:c, :] for t in cums]
    total = jnp.concatenate([jnp.broadcast_to(t, (c, kdim)) for t in totals], axis=0)
    q_dec = (zg_ref[:, 0:kdim].astype(F32) * (dk ** -0.5) * jnp.exp(cum)).astype(BF16)
    k = zg_ref[:, o_k:o_k + kdim].astype(F32)
    k_inv = k * jnp.exp(-cum)
    k_end = (k * jnp.exp(total - cum)).astype(BF16)
    v = zg_ref[:, o_v:o_v + d_gla]

    def lanes_of(t, n, pair):
        return t[n * c:(n + 1) * c, pair * LANES:(pair + 1) * LANES]

    scores = []
    for n in range(n_chunks):
        for pair in range(pairs):
            kp = lanes_of(k_inv, n, pair)
            keys = jnp.concatenate([jnp.where(low_head, kp, 0.0), jnp.where(low_head, 0.0, kp)], axis=0)
            s = _dot_nt(lanes_of(q_dec, n, pair), keys.astype(BF16))
            scores.append(jnp.where(causal, s, 0.0).astype(BF16))

    intra, upds = [], []
    for n in range(n_chunks):
        for pair in range(pairs):
            v_even = v[n * c:(n + 1) * c, (2 * pair) * dv:(2 * pair + 1) * dv]
            v_odd = v[n * c:(n + 1) * c, (2 * pair + 1) * dv:(2 * pair + 2) * dv]
            v_diag = jnp.concatenate([jnp.concatenate([v_even, zeros_v], axis=1),
                                      jnp.concatenate([zeros_v, v_odd], axis=1)], axis=0)
            intra.append(_dot(scores[n * pairs + pair], v_diag))
            upd = _dot_tn(jnp.concatenate([v_even, v_odd], axis=1), lanes_of(k_end, n, pair))
            upds.append(jnp.where(own_block, upd, 0.0))

    outs = [[None] * pairs for _ in range(n_chunks)]
    for pair in range(pairs):
        st = st_ref[pair]
        for n in range(n_chunks):
            outs[n][pair] = intra[n * pairs + pair] + _dot_nt(lanes_of(q_dec, n, pair), st.astype(BF16))
            decay = jnp.exp(totals[n][:, pair * LANES:(pair + 1) * LANES])
            st = st * decay + upds[n * pairs + pair]
        st_ref[pair] = st

    for h in range(GLA_HEADS):
        o = jnp.concatenate([outs[n][h // 2][:, (h % 2) * dv:(h % 2 + 1) * dv] for n in range(n_chunks)], axis=0)
        o = o * lax.rsqrt(jnp.mean(o * o, axis=-1, keepdims=True) + EPS)
        g = zg_ref[:, o_g + h * dv:o_g + (h + 1) * dv].astype(F32)
        y = jax.nn.silu(g) * (o * onorm_ref[:, h * dv:(h + 1) * dv])
        y_ref[:, h * dv:(h + 1) * dv] = y.astype(y_ref.dtype)


def _gla_mixer(zg, la, onorm, *, batch, d_gla):
    rows_total = zg.shape[0]
    seq = rows_total // batch
    rows = min(GLA_ROWS, seq)
    n_t = seq // rows
    token_rows = lambda w: pl.BlockSpec((rows, w), lambda b, t: (b * n_t + t, 0))
    return pl.pallas_call(
        _gla_kernel,
        grid=(batch, n_t),
        in_specs=[token_rows(zg.shape[1]), token_rows(la.shape[1]), _const_spec(onorm.shape)],
        out_specs=token_rows(d_gla),
        out_shape=jax.ShapeDtypeStruct((rows_total, d_gla), BF16),
        scratch_shapes=[pltpu.VMEM((GLA_HEADS // 2, 2 * d_gla // GLA_HEADS, LANES), F32)],
        compiler_params=pltpu.CompilerParams(dimension_semantics=("arbitrary", "arbitrary"),
                                             vmem_limit_bytes=VMEM_LIMIT_BYTES),
        name="gla_mixer",
    )(zg, la, onorm)


def _dense_kernel(*refs, mix, proj, d_s5, d_ff):
    refs = list(refs)
    x_ref = refs.pop(0)
    sub = x_ref.shape[0] // DENSE_SPLIT
    groups = [pl.ds(s * sub, sub) for s in range(DENSE_SPLIT)]
    xs = [x_ref[r, :] for r in groups]
    if mix:
        ys5_ref, ygla_ref, wout_ref, nffn_ref, wfi_ref, wfo_ref = refs[:6]
        refs = refs[6:]
        xs = [x + _dot(ys5_ref[r, :].astype(BF16), wout_ref[0:d_s5, :]) + _dot(ygla_ref[r, :], wout_ref[d_s5:, :])
              for x, r in zip(xs, groups)]
        hs = [_rms(x, nffn_ref[...]).astype(BF16) for x in xs]
        tiles = d_ff // MXU_DIM
        bounds = [MXU_DIM * ((tiles * c + FFN_CHUNKS - 1) // FFN_CHUNKS) for c in range(FFN_CHUNKS + 1)]
        acts = [[] for _ in range(DENSE_SPLIT)]
        for lo, hi in zip(bounds[:-1], bounds[1:]):
            for s in range(DENSE_SPLIT):
                gate = _dot(hs[s], wfi_ref[:, lo:hi])
                up = _dot(hs[s], wfi_ref[:, d_ff + lo:d_ff + hi])
                acts[s].append((jax.nn.silu(gate) * up).astype(BF16))
        xs = [x + _dot(jnp.concatenate(a, axis=1), wfo_ref[...]) for x, a in zip(xs, acts)]
    norm_ref = refs.pop(0)
    hs = [_rms(x, norm_ref[...]) for x in xs]
    if not proj:
        (out_ref,) = refs
        for h, r in zip(hs, groups):
            out_ref[r, :] = h
        return
    win_ref, bgate_ref = refs[:2]
    u_ref, zg_ref, la_ref = refs[-3:]
    o_a = d_s5 + zg_ref.shape[1]
    for x, h, r in zip(xs, hs, groups):
        if mix:
            refs[2][r, :] = x
        z = _dot(h.astype(BF16), win_ref[...])
        u_ref[r, :] = z[:, :d_s5]
        zg_ref[r, :] = z[:, d_s5:o_a].astype(BF16)
        la_ref[r, :] = _log_sigmoid(z[:, o_a:] + bgate_ref[...]) * (1.0 / GLA_GATE_NORM)


def _dense(x, mix_args, norm, proj_args, *, d_s5):
    rows_total, d_model = x.shape
    tm = DENSE_ROWS
    token_rows = lambda w: pl.BlockSpec((tm, w), lambda r: (r, 0))
    mix, proj = mix_args is not None, proj_args is not None
    args, in_specs = [x], [token_rows(d_model)]
    d_ff = 0
    if mix:
        y_s5, y_gla, w_out, norm_ffn, w_fi, w_fo = mix_args
        d_ff = w_fo.shape[0]
        assert d_ff % MXU_DIM == 0
        args += [y_s5, y_gla, w_out, norm_ffn, w_fi, w_fo]
        in_specs += [token_rows(d_s5), token_rows(y_gla.shape[1]), _const_spec(w_out.shape),
                     _const_spec(norm_ffn.shape), _const_spec(w_fi.shape), _const_spec(w_fo.shape)]
    args.append(norm)
    in_specs.append(_const_spec(norm.shape))
    if proj:
        w_in, b_gate = proj_args
        la_w = b_gate.shape[1]
        zg_w = w_in.shape[1] - d_s5 - la_w
        args += [w_in, b_gate]
        in_specs += [_const_spec(w_in.shape), _const_spec(b_gate.shape)]
        out_specs = [token_rows(d_model), token_rows(d_s5), token_rows(zg_w), token_rows(la_w)]
        out_shape = [jax.ShapeDtypeStruct((rows_total, d_model), F32),
                     jax.ShapeDtypeStruct((rows_total, d_s5), F32),
                     jax.ShapeDtypeStruct((rows_total, zg_w), BF16),
                     jax.ShapeDtypeStruct((rows_total, la_w), F32)]
        if not mix:
            out_specs, out_shape = out_specs[1:], out_shape[1:]
    else:
        out_specs = token_rows(d_model)
        out_shape = jax.ShapeDtypeStruct((rows_total, d_model), F32)
    return pl.pallas_call(
        functools.partial(_dense_kernel, mix=mix, proj=proj, d_s5=d_s5, d_ff=d_ff),
        grid=(rows_total // tm,),
        in_specs=in_specs,
        out_specs=out_specs,
        out_shape=out_shape,
        compiler_params=pltpu.CompilerParams(dimension_semantics=("arbitrary",),
                                             vmem_limit_bytes=VMEM_LIMIT_BYTES),
        name="dense_" + ("mix" if mix else "") + ("proj" if proj else "final"),
    )(*args)


def kernel(x, norm_mix, w_in, s5_lam_re, s5_lam_im, s5_b_re, s5_b_im, s5_c_re, s5_c_im, s5_d, s5_log_step,
           s5_w_glu, s5_b_glu, s5_out_norm, gla_w_gate, gla_b_gate, gla_out_norm, w_out, norm_ffn, w_ffn_in,
           w_ffn_out, norm_final):
    batch, seq, d_model = x.shape
    depth = w_in.shape[0]
    d_s5 = s5_d.shape[1]
    d_gla = gla_out_norm.shape[1]
    kdim = gla_w_gate.shape[2]
    o_gate = d_s5 + 2 * kdim + 2 * d_gla
    assert batch % 8 == 0 and (batch * seq) % DENSE_ROWS == 0 and seq % GLA_CHUNK == 0 and seq % (S5_CHUNK * S5_CHUNKS) == 0

    s5_m, s5_e, s5_f_re, s5_f_im, s5_apow = _s5_prep(s5_lam_re, s5_lam_im, s5_log_step, s5_b_re, s5_b_im,
                                                     s5_c_re, s5_c_im)
    s5_f = jnp.concatenate([s5_f_re, s5_f_im], axis=-1)
    w_eff = _gate_fold(w_in[:, :, o_gate:], gla_w_gate)
    w_in_eff = jnp.concatenate([w_in[:, :, :o_gate].astype(BF16), w_eff], axis=-1)
    w_out_b, w_fi_b, w_fo_b, w_glu_b = (w.astype(BF16) for w in (w_out, w_ffn_in, w_ffn_out, s5_w_glu))
    vec = lambda v, i: v[i][None, :]

    xf = x.reshape(batch * seq, d_model)
    u, zg, la = _dense(xf, None, vec(norm_mix, 0), (w_in_eff[0], vec(gla_b_gate, 0)), d_s5=d_s5)
    for i in range(depth):
        y_s5 = _s5_mixer(u.reshape(batch, seq, d_s5), s5_m[i], s5_e[i], s5_f[i], s5_apow[i], vec(s5_d, i),
                         w_glu_b[i], vec(s5_b_glu, i), vec(s5_out_norm, i))
        y_gla = _gla_mixer(zg, la, vec(gla_out_norm, i), batch=batch, d_gla=d_gla)
        mix_args = (y_s5.reshape(batch * seq, d_s5), y_gla, w_out_b[i], vec(norm_ffn, i), w_fi_b[i], w_fo_b[i])
        if i + 1 < depth:
            xf, u, zg, la = _dense(xf, mix_args, vec(norm_mix, i + 1), (w_in_eff[i + 1], vec(gla_b_gate, i + 1)),
                                   d_s5=d_s5)
        else:
            xf = _dense(xf, mix_args, norm_final[None, :], None, d_s5=d_s5)
    return xf.reshape(batch, seq, d_model)
```

```python
import functools
from typing import NamedTuple

import jax
import jax.numpy as jnp
from jax import lax
from jax.experimental import pallas as pl
from jax.experimental.pallas import tpu as pltpu

S5_GROUP = 16
S5_STATE = 64
GLA_HEADS = 4
GLA_CHUNK = 64
GLA_GATE_NORM = 16.0
EPS = 1e-6

LANES = 128
MXU_DIM = 256
VMEM_LIMIT_BYTES = 56 * 1024 * 1024

DENSE_ROWS = 512
DENSE_SPLIT = 2
FFN_CHUNKS = 2
S5_CHUNK = 16
S5_CHUNKS = 8
S5_FINISH_ROWS = 512
GLA_ROWS = 512

BF16 = jnp.bfloat16
F32 = jnp.float32


def _dot(a, b):
    return jnp.dot(a, b, preferred_element_type=F32)


def _dot_nt(a, b):
    return lax.dot_general(a, b, (((1,), (1,)), ((), ())), preferred_element_type=F32)


def _dot_tn(a, b):
    return lax.dot_general(a, b, (((0,), (0,)), ((), ())), preferred_element_type=F32)


def _rms(x, gain):
    return x * lax.rsqrt(jnp.mean(x * x, axis=-1, keepdims=True) + EPS) * gain


class _Layer(NamedTuple):
    stacked: jax.Array
    index: int

    @property
    def shape(self):
        return self.stacked.shape[1:]


def _resident(x):
    if isinstance(x, _Layer):
        lead, shape, operand = (x.index,), x.shape, x.stacked
        block = (None,) + shape
    else:
        lead, shape, operand = (), x.shape, x
        block = shape
    return operand, pl.BlockSpec(block, lambda *_: lead + (0,) * len(shape), pipeline_mode=pl.Buffered(1))


def _dot_split(a, b):
    a_hi, b_hi = a.astype(BF16), b.astype(BF16)
    a_lo = (a - a_hi.astype(F32)).astype(BF16)
    b_lo = (b - b_hi.astype(F32)).astype(BF16)
    return _dot(a_hi, b_hi) + _dot(a_hi, b_lo) + _dot(a_lo, b_hi)


def _gate_fold_kernel(win_gate_ref, w_gate_ref, weff_ref):
    weff_ref[0] = _dot_split(win_gate_ref[0], w_gate_ref[0]).astype(BF16)


def _gate_fold(win_gate, w_gate):
    depth, d_model, rank = win_gate.shape
    kdim = w_gate.shape[2]
    per_layer = lambda *s: pl.BlockSpec((1,) + s, lambda i: (i,) + (0,) * len(s))
    return pl.pallas_call(
        _gate_fold_kernel,
        grid=(depth,),
        in_specs=[per_layer(d_model, rank), per_layer(rank, kdim)],
        out_specs=per_layer(d_model, kdim),
        out_shape=jax.ShapeDtypeStruct((depth, d_model, kdim), BF16),
        compiler_params=pltpu.CompilerParams(dimension_semantics=("arbitrary",)),
        name="gate_fold",
    )(win_gate, w_gate)


def _cmul(ar, ai, br, bi):
    return ar * br - ai * bi, ar * bi + ai * br


def _discretise(lr, li, log_step):
    step = jnp.exp(log_step)
    mag = jnp.exp(lr * step)
    ar = mag * jnp.cos(li * step)
    ai = mag * jnp.sin(li * step)
    den = lr * lr + li * li
    fr = ((ar - 1.0) * lr + ai * li) / den
    fi = (ai * lr - (ar - 1.0) * li) / den
    return ar, ai, fr, fi


def _s5_prep_kernel(lam_re_row_ref, lam_im_row_ref, lam_re_col_ref, lam_im_col_ref, log_step_ref,
                    c_re_ref, c_im_ref, b_re_ref, b_im_ref, m_ref, e_ref, f_re_ref, f_im_ref, apow_ref):
    t_c = S5_CHUNK
    h_c = S5_GROUP
    p_c = lam_re_row_ref.shape[-1]
    log_step = log_step_ref[0, 0]
    ar, ai, _, _ = _discretise(lam_re_row_ref[0, 0], lam_im_row_ref[0, 0], log_step)
    arc, aic, frc, fic = _discretise(lam_re_col_ref[0, 0], lam_im_col_ref[0, 0], log_step)

    pr = jnp.ones((h_c, p_c), F32)
    pi = jnp.zeros((h_c, p_c), F32)
    blocks_r, blocks_i = [], []
    for _ in range(t_c):
        blocks_r.append(pr)
        blocks_i.append(pi)
        pr, pi = _cmul(pr, pi, ar, ai)
    l_re, l_im = _cmul(c_re_ref[0, 0], c_im_ref[0, 0], jnp.concatenate(blocks_r, axis=0),
                       jnp.concatenate(blocks_i, axis=0))
    f_re, f_im = _cmul(l_re, l_im, ar, ai)
    f_re_ref[0, 0] = f_re.astype(BF16)
    f_im_ref[0, 0] = (-f_im).astype(BF16)

    bw_re, bw_im = _cmul(frc, fic, b_re_ref[0, 0], b_im_ref[0, 0])
    k_wide = _dot_split(l_re, bw_re) - _dot_split(l_im, bw_im)
    per_tile = LANES // h_c
    lane_blk = lax.broadcasted_iota(jnp.int32, (t_c * h_c, LANES), 1) // h_c
    for lt in range(t_c // per_tile):
        k_tile = k_wide[:, lt * LANES:(lt + 1) * LANES]
        m = jnp.zeros((t_c * h_c, LANES), F32)
        for jj in range(per_tile):
            j = lt * per_tile + jj
            moved = k_tile if j == 0 else jnp.concatenate(
                [jnp.zeros((j * h_c, LANES), F32), k_tile[:(t_c - j) * h_c]], axis=0)
            m = jnp.where(lane_blk == jj, moved, m)
        m_ref[0, 0, :, lt * LANES:(lt + 1) * LANES] = m.astype(BF16)

    lane_blk = lax.broadcasted_iota(jnp.int32, (p_c, LANES), 1) // h_c
    qr = jnp.ones((p_c, 1), F32)
    qi = jnp.zeros((p_c, 1), F32)
    for lt in range(t_c // per_tile - 1, -1, -1):
        pe_r = jnp.zeros((p_c, LANES), F32)
        pe_i = jnp.zeros((p_c, LANES), F32)
        for jj in range(per_tile - 1, -1, -1):
            pe_r = jnp.where(lane_blk == jj, qr, pe_r)
            pe_i = jnp.where(lane_blk == jj, qi, pe_i)
            qr, qi = _cmul(qr, qi, arc, aic)
        cols = slice(lt * LANES, (lt + 1) * LANES)
        e_re, e_im = _cmul(pe_r, pe_i, bw_re[:, cols], bw_im[:, cols])
        e_ref[0, 0, 0:p_c, cols] = e_re.astype(BF16)
        e_ref[0, 0, p_c:2 * p_c, cols] = e_im.astype(BF16)

    for k in range(3):
        apow_ref[0, 0, k, 0] = jnp.broadcast_to(qr, (p_c, LANES))
        apow_ref[0, 0, k, 1] = jnp.broadcast_to(qi, (p_c, LANES))
        qr, qi = _cmul(qr, qi, qr, qi)


def _s5_prep(lam_re, lam_im, log_step, b_re, b_im, c_re, c_im):
    depth, groups, states = lam_re.shape
    t_c, h_c = S5_CHUNK, S5_GROUP
    row = lambda t: t.reshape(depth, groups, 1, states)
    col = lambda t: t.reshape(depth, groups, states, 1)
    c_rows = lambda t: jnp.tile(t, (1, 1, t_c, 1))
    b_lanes = lambda t: jnp.tile(t, (1, 1, 1, t_c))
    blk = lambda *s: pl.BlockSpec((1, 1) + s, lambda i, g: (i, g) + (0,) * len(s))
    w = t_c * h_c
    return pl.pallas_call(
        _s5_prep_kernel,
        grid=(depth, groups),
        in_specs=[blk(1, states)] * 2 + [blk(states, 1)] * 2 + [blk(1, 1)]
        + [blk(w, states)] * 2 + [blk(states, w)] * 2,
        out_specs=[blk(w, w), blk(2 * states, w), blk(w, states), blk(w, states), blk(3, 2, states, LANES)],
        out_shape=[jax.ShapeDtypeStruct((depth, groups, w, w), BF16),
                   jax.ShapeDtypeStruct((depth, groups, 2 * states, w), BF16),
                   jax.ShapeDtypeStruct((depth, groups, w, states), BF16),
                   jax.ShapeDtypeStruct((depth, groups, w, states), BF16),
                   jax.ShapeDtypeStruct((depth, groups, 3, 2, states, LANES), F32)],
        compiler_params=pltpu.CompilerParams(dimension_semantics=("arbitrary", "arbitrary")),
        name="s5_prep",
    )(row(lam_re), row(lam_im), col(lam_re), col(lam_im), log_step.reshape(depth, groups, 1, 1),
      c_rows(c_re), c_rows(c_im), b_lanes(b_re), b_lanes(b_im))


def _s5_kernel(u_ref, m_ref, e_ref, f_ref, apow_ref, d_ref, wglu_ref, bglu_ref, onorm_ref, y_ref,
               ub_ref, xt_ref, yt_ref, y2_ref, z_ref, s_ref, carry_ref, perm_ref):
    n_b, t_blk, d_s5 = u_ref.shape
    groups = m_ref.shape[0]
    t_c, h_c = S5_CHUNK, S5_GROUP
    n_c = t_blk // t_c
    p_c = f_ref.shape[2] // 2
    samples = n_b * n_c
    assert samples == LANES and n_c == 8

    @pl.when(pl.program_id(1) == 0)
    def _():
        carry_ref[...] = jnp.zeros_like(carry_ref)

    @pl.when((pl.program_id(0) == 0) & (pl.program_id(1) == 0))
    def _():
        dst = lax.broadcasted_iota(jnp.int32, (t_blk, t_blk), 0)
        src = lax.broadcasted_iota(jnp.int32, (t_blk, t_blk), 1)
        perm_ref[0] = jnp.where(src == (dst % n_c) * t_c + dst // n_c, 1.0, 0.0).astype(BF16)
        perm_ref[1] = jnp.where(src == (dst % t_c) * n_c + dst // t_c, 1.0, 0.0).astype(BF16)

    for b in range(n_b):
        ub_ref[b] = _dot(perm_ref[0], u_ref[b].astype(BF16))
    for t in range(t_c):
        xt_ref[t] = ub_ref[:, t * n_c:(t + 1) * n_c, :].reshape(samples, d_s5).T
    for g in range(groups):
        a = xt_ref[:, g * h_c:(g + 1) * h_c, :].reshape(t_c * h_c, samples).astype(BF16)
        yt_ref[g] = _dot(m_ref[g], a)
        z_ref[g] = _dot(e_ref[g], a)

    c_idx = lax.broadcasted_iota(jnp.int32, (p_c, LANES), 1) % n_c
    for g in range(groups):
        z_re, z_im = z_ref[g, 0:p_c], z_ref[g, p_c:2 * p_c]
        v_re = jnp.where(c_idx == 0, carry_ref[g, 0:p_c], pltpu.roll(z_re, 1, 1))
        v_im = jnp.where(c_idx == 0, carry_ref[g, p_c:2 * p_c], pltpu.roll(z_im, 1, 1))
        for k in range(3):
            hop = 1 << k
            sh_re = jnp.where(c_idx >= hop, pltpu.roll(v_re, hop, 1), 0.0)
            sh_im = jnp.where(c_idx >= hop, pltpu.roll(v_im, hop, 1), 0.0)
            d_re, d_im = _cmul(apow_ref[g, k, 0], apow_ref[g, k, 1], sh_re, sh_im)
            v_re, v_im = v_re + d_re, v_im + d_im
        s_ref[g, 0:p_c] = v_re.astype(BF16)
        s_ref[g, p_c:2 * p_c] = v_im.astype(BF16)
        o_re, o_im = _cmul(apow_ref[g, 0, 0], apow_ref[g, 0, 1], v_re, v_im)
        carry_ref[g, 0:p_c] = pltpu.roll(o_re + z_re, LANES - (n_c - 1), 1)
        carry_ref[g, p_c:2 * p_c] = pltpu.roll(o_im + z_im, LANES - (n_c - 1), 1)

    for g in range(groups):
        yt_ref[g] += _dot(f_ref[g], s_ref[g])
    for t in range(t_c):
        y2_ref[t] = yt_ref[:, t * h_c:(t + 1) * h_c, :].reshape(d_s5, samples).T
    for b in range(n_b):
        y = y2_ref[:, b * n_c:(b + 1) * n_c, :].reshape(t_blk, d_s5)
        y_hi = y.astype(BF16)
        y_lo = (y - y_hi.astype(F32)).astype(BF16)
        y2 = _dot(perm_ref[1], jnp.concatenate([y_hi, y_lo], axis=-1))
        ub_ref[b] = y2[:, :d_s5] + y2[:, d_s5:] + d_ref[...] * u_ref[b]
    fin = S5_FINISH_ROWS // t_blk
    for q in range(n_b // fin):
        y = ub_ref[q * fin:(q + 1) * fin].reshape(S5_FINISH_ROWS, d_s5)
        y = jax.nn.gelu(y)
        y = y * jax.nn.sigmoid(_dot(y.astype(BF16), wglu_ref[...]) + bglu_ref[...])
        y_ref[q * fin:(q + 1) * fin] = _rms(y, onorm_ref[...]).reshape(fin, t_blk, d_s5)


def _s5_mixer(u, m, e, f, apow, d_skip, wglu, bglu, onorm):
    batch, seq, d_s5 = u.shape
    groups, w = m.shape[0], m.shape[1]
    states = f.shape[2] // 2
    t_blk = S5_CHUNK * S5_CHUNKS
    n_b = LANES // S5_CHUNKS
    assert batch % n_b == 0 and seq % t_blk == 0
    block = pl.BlockSpec((n_b, t_blk, d_s5), lambda bb, t: (bb, t, 0))
    params, param_specs = zip(*map(_resident, (m, e, f, apow, d_skip, wglu, bglu, onorm)))
    return pl.pallas_call(
        _s5_kernel,
        grid=(batch // n_b, seq // t_blk),
        in_specs=[block, *param_specs],
        out_specs=block,
        out_shape=jax.ShapeDtypeStruct((batch, seq, d_s5), F32),
        scratch_shapes=[pltpu.VMEM((n_b, t_blk, d_s5), F32),
                        pltpu.VMEM((S5_CHUNK, d_s5, LANES), F32),
                        pltpu.VMEM((groups, w, LANES), F32),
                        pltpu.VMEM((S5_CHUNK, LANES, d_s5), F32),
                        pltpu.VMEM((groups, 2 * states, LANES), F32),
                        pltpu.VMEM((groups, 2 * states, LANES), BF16),
                        pltpu.VMEM((groups, 2 * states, LANES), F32),
                        pltpu.VMEM((2, t_blk, t_blk), BF16)],
        compiler_params=pltpu.CompilerParams(dimension_semantics=("arbitrary", "arbitrary"),
                                             vmem_limit_bytes=VMEM_LIMIT_BYTES),
        name="s5_mixer",
    )(u, *params)


def _log_sigmoid(x):
    return jnp.minimum(x, 0.0) - jnp.log1p(jnp.exp(-jnp.abs(x)))


def _gla_kernel(zg_ref, la_ref, onorm_ref, y_ref, st_ref):
    rows = zg_ref.shape[0]
    d_gla = y_ref.shape[1]
    dv = d_gla // GLA_HEADS
    dk = dv // 2
    kdim = GLA_HEADS * dk
    o_k, o_v, o_g = kdim, 2 * kdim, 2 * kdim + d_gla
    c = GLA_CHUNK
    assert 2 * dk == LANES and dv == LANES

    @pl.when(pl.program_id(1) == 0)
    def _():
        st_ref[...] = jnp.zeros_like(st_ref)

    r_i = lax.broadcasted_iota(jnp.int32, (c, 3 * c), 0)
    c_i = lax.broadcasted_iota(jnp.int32, (c, 3 * c), 1)
    tri3 = jnp.where((c_i % c) <= r_i, 1.0, 0.0).astype(BF16)
    causal = (lax.broadcasted_iota(jnp.int32, (c, LANES), 0)
              >= lax.broadcasted_iota(jnp.int32, (c, LANES), 1) % c)
    low_head = lax.broadcasted_iota(jnp.int32, (c, LANES), 1) < dk
    own_block = ((lax.broadcasted_iota(jnp.int32, (2 * dv, LANES), 0) < dv)
                 == (lax.broadcasted_iota(jnp.int32, (2 * dv, LANES), 1) < dk))
    n_chunks = rows // c
    pairs = GLA_HEADS // 2
    rows_of = lambda t, n: t[n * c:(n + 1) * c]
    zeros_v = jnp.zeros((c, dv), BF16)

    la = la_ref[...]
    la_hi = la.astype(BF16)
    rem = la - la_hi.astype(F32)
    la_mid = rem.astype(BF16)
    la_lo = (rem - la_mid.astype(F32)).astype(BF16)
    cums = [_dot(tri3, jnp.concatenate([rows_of(la_hi, n), rows_of(la_mid, n), rows_of(la_lo, n)], axis=0))
            for n in range(n_chunks)]
    cum = jnp.concatenate(cums, axis=0)
    totals = [t[c - 1:c, :] for t in cums]
    total = jnp.concatenate([jnp.broadcast_to(t, (c, kdim)) for t in totals], axis=0)
    q_dec = (zg_ref[:, 0:kdim].astype(F32) * (dk ** -0.5) * jnp.exp(cum)).astype(BF16)
    k = zg_ref[:, o_k:o_k + kdim].astype(F32)
    k_inv = k * jnp.exp(-cum)
    k_end = (k * jnp.exp(total - cum)).astype(BF16)
    v = zg_ref[:, o_v:o_v + d_gla]

    def lanes_of(t, n, pair):
        return t[n * c:(n + 1) * c, pair * LANES:(pair + 1) * LANES]

    scores = []
    for n in range(n_chunks):
        for pair in range(pairs):
            kp = lanes_of(k_inv, n, pair)
            keys = jnp.concatenate([jnp.where(low_head, kp, 0.0), jnp.where(low_head, 0.0, kp)], axis=0)
            s = _dot_nt(lanes_of(q_dec, n, pair), keys.astype(BF16))
            scores.append(jnp.where(causal, s, 0.0).astype(BF16))

    intra, upds = [], []
    for n in range(n_chunks):
        for pair in range(pairs):
            v_even = v[n * c:(n + 1) * c, (2 * pair) * dv:(2 * pair + 1) * dv]
            v_odd = v[n * c:(n + 1) * c, (2 * pair + 1) * dv:(2 * pair + 2) * dv]
            v_diag = jnp.concatenate([jnp.concatenate([v_even, zeros_v], axis=1),
                                      jnp.concatenate([zeros_v, v_odd], axis=1)], axis=0)
            intra.append(_dot(scores[n * pairs + pair], v_diag))
            upd = _dot_tn(jnp.concatenate([v_even, v_odd], axis=1), lanes_of(k_end, n, pair))
            upds.append(jnp.where(own_block, upd, 0.0))

    outs = [[None] * pairs for _ in range(n_chunks)]
    for pair in range(pairs):
        st = st_ref[pair]
        for n in range(n_chunks):
            outs[n][pair] = intra[n * pairs + pair] + _dot_nt(lanes_of(q_dec, n, pair), st.astype(BF16))
            decay = jnp.exp(totals[n][:, pair * LANES:(pair + 1) * LANES])
            st = st * decay + upds[n * pairs + pair]
        st_ref[pair] = st

    for h in range(GLA_HEADS):
        o = jnp.concatenate([outs[n][h // 2][:, (h % 2) * dv:(h % 2 + 1) * dv] for n in range(n_chunks)], axis=0)
        o = o * lax.rsqrt(jnp.mean(o * o, axis=-1, keepdims=True) + EPS)
        g = zg_ref[:, o_g + h * dv:o_g + (h + 1) * dv].astype(F32)
        y = jax.nn.silu(g) * (o * onorm_ref[:, h * dv:(h + 1) * dv])
        y_ref[:, h * dv:(h + 1) * dv] = y.astype(y_ref.dtype)


def _gla_mixer(zg, la, onorm, *, batch, d_gla):
    rows_total = zg.shape[0]
    seq = rows_total // batch
    rows = min(GLA_ROWS, seq)
    n_t = seq // rows
    token_rows = lambda w: pl.BlockSpec((rows, w), lambda b, t: (b * n_t + t, 0))
    onorm, onorm_spec = _resident(onorm)
    return pl.pallas_call(
        _gla_kernel,
        grid=(batch, n_t),
        in_specs=[token_rows(zg.shape[1]), token_rows(la.shape[1]), onorm_spec],
        out_specs=token_rows(d_gla),
        out_shape=jax.ShapeDtypeStruct((rows_total, d_gla), BF16),
        scratch_shapes=[pltpu.VMEM((GLA_HEADS // 2, 2 * d_gla // GLA_HEADS, LANES), F32)],
        compiler_params=pltpu.CompilerParams(dimension_semantics=("arbitrary", "arbitrary"),
                                             vmem_limit_bytes=VMEM_LIMIT_BYTES),
        name="gla_mixer",
    )(zg, la, onorm)


def _dense_kernel(*refs, mix, proj, d_s5, d_ff):
    refs = list(refs)
    x_ref = refs.pop(0)
    sub = x_ref.shape[0] // DENSE_SPLIT
    groups = [pl.ds(s * sub, sub) for s in range(DENSE_SPLIT)]
    xs = [x_ref[r, :] for r in groups]
    if mix:
        ys5_ref, ygla_ref, wout_ref, nffn_ref, wfi_ref, wfo_ref = refs[:6]
        refs = refs[6:]
        xs = [x + _dot(ys5_ref[r, :].astype(BF16), wout_ref[0:d_s5, :]) + _dot(ygla_ref[r, :], wout_ref[d_s5:, :])
              for x, r in zip(xs, groups)]
        hs = [_rms(x, nffn_ref[...]).astype(BF16) for x in xs]
        tiles = d_ff // MXU_DIM
        bounds = [MXU_DIM * ((tiles * c + FFN_CHUNKS - 1) // FFN_CHUNKS) for c in range(FFN_CHUNKS + 1)]
        acts = [[] for _ in range(DENSE_SPLIT)]
        for lo, hi in zip(bounds[:-1], bounds[1:]):
            for s in range(DENSE_SPLIT):
                gate = _dot(hs[s], wfi_ref[:, lo:hi])
                up = _dot(hs[s], wfi_ref[:, d_ff + lo:d_ff + hi])
                acts[s].append((jax.nn.silu(gate) * up).astype(BF16))
        xs = [x + _dot(jnp.concatenate(a, axis=1), wfo_ref[...]) for x, a in zip(xs, acts)]
    norm_ref = refs.pop(0)
    hs = [_rms(x, norm_ref[...]) for x in xs]
    if not proj:
        (out_ref,) = refs
        for h, r in zip(hs, groups):
            out_ref[r, :] = h
        return
    win_ref, bgate_ref = refs[:2]
    u_ref, zg_ref, la_ref = refs[-3:]
    o_a = d_s5 + zg_ref.shape[1]
    for x, h, r in zip(xs, hs, groups):
        if mix:
            refs[2][r, :] = x
        z = _dot(h.astype(BF16), win_ref[...])
        u_ref[r, :] = z[:, :d_s5]
        zg_ref[r, :] = z[:, d_s5:o_a].astype(BF16)
        la_ref[r, :] = _log_sigmoid(z[:, o_a:] + bgate_ref[...]) * (1.0 / GLA_GATE_NORM)


def _dense(x, mix_args, norm, proj_args, *, d_s5):
    rows_total, d_model = x.shape
    tm = DENSE_ROWS
    token_rows = lambda w: pl.BlockSpec((tm, w), lambda r: (r, 0))
    mix, proj = mix_args is not None, proj_args is not None
    args, in_specs = [x], [token_rows(d_model)]
    d_ff = 0
    if mix:
        y_s5, y_gla, w_out, norm_ffn, w_fi, w_fo = mix_args
        d_ff = w_fo.shape[0]
        assert d_ff % MXU_DIM == 0
        args += [y_s5, y_gla]
        in_specs += [token_rows(d_s5), token_rows(y_gla.shape[1])]
        resident = [w_out, norm_ffn, w_fi, w_fo, norm]
    else:
        resident = [norm]
    if proj:
        w_in, b_gate = proj_args
        la_w = b_gate.shape[1]
        zg_w = w_in.shape[1] - d_s5 - la_w
        resident += [w_in, b_gate]
        out_specs = [token_rows(d_model), token_rows(d_s5), token_rows(zg_w), token_rows(la_w)]
        out_shape = [jax.ShapeDtypeStruct((rows_total, d_model), F32),
                     jax.ShapeDtypeStruct((rows_total, d_s5), F32),
                     jax.ShapeDtypeStruct((rows_total, zg_w), BF16),
                     jax.ShapeDtypeStruct((rows_total, la_w), F32)]
        if not mix:
            out_specs, out_shape = out_specs[1:], out_shape[1:]
    else:
        out_specs = token_rows(d_model)
        out_shape = jax.ShapeDtypeStruct((rows_total, d_model), F32)
    for operand, spec in map(_resident, resident):
        args.append(operand)
        in_specs.append(spec)
    return pl.pallas_call(
        functools.partial(_dense_kernel, mix=mix, proj=proj, d_s5=d_s5, d_ff=d_ff),
        grid=(rows_total // tm,),
        in_specs=in_specs,
        out_specs=out_specs,
        out_shape=out_shape,
        compiler_params=pltpu.CompilerParams(dimension_semantics=("arbitrary",),
                                             vmem_limit_bytes=VMEM_LIMIT_BYTES),
        name="dense_" + ("mix" if mix else "") + ("proj" if proj else "final"),
    )(*args)


def kernel(x, norm_mix, w_in, s5_lam_re, s5_lam_im, s5_b_re, s5_b_im, s5_c_re, s5_c_im, s5_d, s5_log_step,
           s5_w_glu, s5_b_glu, s5_out_norm, gla_w_gate, gla_b_gate, gla_out_norm, w_out, norm_ffn, w_ffn_in,
           w_ffn_out, norm_final):
    batch, seq, d_model = x.shape
    depth = w_in.shape[0]
    d_s5 = s5_d.shape[1]
    d_gla = gla_out_norm.shape[1]
    kdim = gla_w_gate.shape[2]
    o_gate = d_s5 + 2 * kdim + 2 * d_gla
    assert batch % 8 == 0 and (batch * seq) % DENSE_ROWS == 0 and seq % GLA_CHUNK == 0 and seq % (S5_CHUNK * S5_CHUNKS) == 0

    s5_m, s5_e, s5_f_re, s5_f_im, s5_apow = _s5_prep(s5_lam_re, s5_lam_im, s5_log_step, s5_b_re, s5_b_im,
                                                     s5_c_re, s5_c_im)
    s5_f = jnp.concatenate([s5_f_re, s5_f_im], axis=-1)
    w_eff = _gate_fold(w_in[:, :, o_gate:], gla_w_gate)
    w_in_eff = jnp.concatenate([w_in[:, :, :o_gate].astype(BF16), w_eff], axis=-1)
    w_out_b, w_fi_b, w_fo_b, w_glu_b = (w.astype(BF16) for w in (w_out, w_ffn_in, w_ffn_out, s5_w_glu))
    layer = _Layer
    vec = lambda v, i: _Layer(v[:, None, :], i)

    xf = x.reshape(batch * seq, d_model)
    u, zg, la = _dense(xf, None, vec(norm_mix, 0), (layer(w_in_eff, 0), vec(gla_b_gate, 0)), d_s5=d_s5)
    for i in range(depth):
        y_s5 = _s5_mixer(u.reshape(batch, seq, d_s5), layer(s5_m, i), layer(s5_e, i), layer(s5_f, i),
                         layer(s5_apow, i), vec(s5_d, i), layer(w_glu_b, i), vec(s5_b_glu, i), vec(s5_out_norm, i))
        y_gla = _gla_mixer(zg, la, vec(gla_out_norm, i), batch=batch, d_gla=d_gla)
        mix_args = (y_s5.reshape(batch * seq, d_s5), y_gla, layer(w_out_b, i), vec(norm_ffn, i),
                    layer(w_fi_b, i), layer(w_fo_b, i))
        if i + 1 < depth:
            xf, u, zg, la = _dense(xf, mix_args, vec(norm_mix, i + 1),
                                   (layer(w_in_eff, i + 1), vec(gla_b_gate, i + 1)), d_s5=d_s5)
        else:
            xf = _dense(xf, mix_args, norm_final[None, :], None, d_s5=d_s5)
    return xf.reshape(batch, seq, d_model)
```

```python
import functools
from typing import NamedTuple

import jax
import jax.numpy as jnp
from jax import lax
from jax.experimental import pallas as pl
from jax.experimental.pallas import tpu as pltpu

S5_GROUP = 16
S5_STATE = 64
GLA_HEADS = 4
GLA_CHUNK = 64
GLA_GATE_NORM = 16.0
EPS = 1e-6

LANES = 128
MXU_DIM = 256
VMEM_LIMIT_BYTES = 56 * 1024 * 1024

DENSE_ROWS = 512
DENSE_SPLIT = 2
FFN_CHUNKS = 2
S5_CHUNK = 16
S5_CHUNKS = 8
S5_FINISH_ROWS = 512
S5_PREP_GROUPS = 8
GLA_ROWS = 2048

BF16 = jnp.bfloat16
F32 = jnp.float32


def _dot(a, b):
    return jnp.dot(a, b, preferred_element_type=F32)


def _dot_nt(a, b):
    return lax.dot_general(a, b, (((1,), (1,)), ((), ())), preferred_element_type=F32)


def _dot_tn(a, b):
    return lax.dot_general(a, b, (((0,), (0,)), ((), ())), preferred_element_type=F32)


def _rms(x, gain):
    return x * lax.rsqrt(jnp.mean(x * x, axis=-1, keepdims=True) + EPS) * gain


class _Layer(NamedTuple):
    stacked: jax.Array
    index: int

    @property
    def shape(self):
        return self.stacked.shape[1:]


def _resident(x):
    if isinstance(x, _Layer):
        lead, shape, operand = (x.index,), x.shape, x.stacked
        block = (None,) + shape
    else:
        lead, shape, operand = (), x.shape, x
        block = shape
    return operand, pl.BlockSpec(block, lambda *_: lead + (0,) * len(shape), pipeline_mode=pl.Buffered(1))


def _dot_split(a, b):
    a_hi, b_hi = a.astype(BF16), b.astype(BF16)
    a_lo = (a - a_hi.astype(F32)).astype(BF16)
    b_lo = (b - b_hi.astype(F32)).astype(BF16)
    return _dot(a_hi, b_hi) + _dot(a_hi, b_lo) + _dot(a_lo, b_hi)


def _gate_fold_kernel(win_gate_ref, w_gate_ref, weff_ref):
    weff_ref[0] = _dot_split(win_gate_ref[0], w_gate_ref[0]).astype(BF16)


def _gate_fold(win_gate, w_gate):
    depth, d_model, rank = win_gate.shape
    kdim = w_gate.shape[2]
    per_layer = lambda *s: pl.BlockSpec((1,) + s, lambda i: (i,) + (0,) * len(s))
    return pl.pallas_call(
        _gate_fold_kernel,
        grid=(depth,),
        in_specs=[per_layer(d_model, rank), per_layer(rank, kdim)],
        out_specs=per_layer(d_model, kdim),
        out_shape=jax.ShapeDtypeStruct((depth, d_model, kdim), BF16),
        compiler_params=pltpu.CompilerParams(dimension_semantics=("arbitrary",)),
        name="gate_fold",
    )(win_gate, w_gate)


def _cmul(ar, ai, br, bi):
    return ar * br - ai * bi, ar * bi + ai * br


def _discretise(lr, li, log_step):
    step = jnp.exp(log_step)
    mag = jnp.exp(lr * step)
    ar = mag * jnp.cos(li * step)
    ai = mag * jnp.sin(li * step)
    den = lr * lr + li * li
    fr = ((ar - 1.0) * lr + ai * li) / den
    fi = (ai * lr - (ar - 1.0) * li) / den
    return ar, ai, fr, fi


def _s5_prep_kernel(lam_re_row_ref, lam_im_row_ref, lam_re_col_ref, lam_im_col_ref, log_step_ref,
                    c_re_ref, c_im_ref, b_re_ref, b_im_ref, m_ref, e_ref, f_re_ref, f_im_ref, apow_ref):
    t_c = S5_CHUNK
    h_c = S5_GROUP
    p_c = lam_re_row_ref.shape[-1]

    def one_group(gi, _):
        log_step = log_step_ref[0, gi]
        ar, ai, _, _ = _discretise(lam_re_row_ref[0, gi], lam_im_row_ref[0, gi], log_step)
        arc, aic, frc, fic = _discretise(lam_re_col_ref[0, gi], lam_im_col_ref[0, gi], log_step)

        pr = jnp.ones((h_c, p_c), F32)
        pi = jnp.zeros((h_c, p_c), F32)
        blocks_r, blocks_i = [], []
        for _ in range(t_c):
            blocks_r.append(pr)
            blocks_i.append(pi)
            pr, pi = _cmul(pr, pi, ar, ai)
        l_re, l_im = _cmul(c_re_ref[0, gi], c_im_ref[0, gi], jnp.concatenate(blocks_r, axis=0),
                           jnp.concatenate(blocks_i, axis=0))
        f_re, f_im = _cmul(l_re, l_im, ar, ai)
        f_re_ref[0, gi] = f_re.astype(BF16)
        f_im_ref[0, gi] = (-f_im).astype(BF16)

        bw_re, bw_im = _cmul(frc, fic, b_re_ref[0, gi], b_im_ref[0, gi])
        k_wide = _dot_split(l_re, bw_re) - _dot_split(l_im, bw_im)
        per_tile = LANES // h_c
        lane_blk = lax.broadcasted_iota(jnp.int32, (t_c * h_c, LANES), 1) // h_c
        for lt in range(t_c // per_tile):
            k_tile = k_wide[:, lt * LANES:(lt + 1) * LANES]
            m = jnp.zeros((t_c * h_c, LANES), F32)
            for jj in range(per_tile):
                j = lt * per_tile + jj
                moved = k_tile if j == 0 else jnp.concatenate(
                    [jnp.zeros((j * h_c, LANES), F32), k_tile[:(t_c - j) * h_c]], axis=0)
                m = jnp.where(lane_blk == jj, moved, m)
            m_ref[0, gi, :, lt * LANES:(lt + 1) * LANES] = m.astype(BF16)

        lane_blk = lax.broadcasted_iota(jnp.int32, (p_c, LANES), 1) // h_c
        qr = jnp.ones((p_c, 1), F32)
        qi = jnp.zeros((p_c, 1), F32)
        for lt in range(t_c // per_tile - 1, -1, -1):
            pe_r = jnp.zeros((p_c, LANES), F32)
            pe_i = jnp.zeros((p_c, LANES), F32)
            for jj in range(per_tile - 1, -1, -1):
                pe_r = jnp.where(lane_blk == jj, qr, pe_r)
                pe_i = jnp.where(lane_blk == jj, qi, pe_i)
                qr, qi = _cmul(qr, qi, arc, aic)
            cols = slice(lt * LANES, (lt + 1) * LANES)
            e_re, e_im = _cmul(pe_r, pe_i, bw_re[:, cols], bw_im[:, cols])
            e_ref[0, gi, 0:p_c, cols] = e_re.astype(BF16)
            e_ref[0, gi, p_c:2 * p_c, cols] = e_im.astype(BF16)

        for k in range(3):
            apow_ref[0, gi, k, 0] = jnp.broadcast_to(qr, (p_c, LANES))
            apow_ref[0, gi, k, 1] = jnp.broadcast_to(qi, (p_c, LANES))
            qr, qi = _cmul(qr, qi, qr, qi)
        return 0

    lax.fori_loop(0, m_ref.shape[1], one_group, 0)


def _s5_prep(lam_re, lam_im, log_step, b_re, b_im, c_re, c_im):
    depth, groups, states = lam_re.shape
    t_c, h_c = S5_CHUNK, S5_GROUP
    row = lambda t: t.reshape(depth, groups, 1, states)
    col = lambda t: t.reshape(depth, groups, states, 1)
    c_rows = lambda t: jnp.tile(t, (1, 1, t_c, 1))
    b_lanes = lambda t: jnp.tile(t, (1, 1, 1, t_c))
    blk = lambda *s: pl.BlockSpec((1, S5_PREP_GROUPS) + s, lambda i, g: (i, g) + (0,) * len(s))
    w = t_c * h_c
    return pl.pallas_call(
        _s5_prep_kernel,
        grid=(depth, groups // S5_PREP_GROUPS),
        in_specs=[blk(1, states)] * 2 + [blk(states, 1)] * 2 + [blk(1, 1)]
        + [blk(w, states)] * 2 + [blk(states, w)] * 2,
        out_specs=[blk(w, w), blk(2 * states, w), blk(w, states), blk(w, states), blk(3, 2, states, LANES)],
        out_shape=[jax.ShapeDtypeStruct((depth, groups, w, w), BF16),
                   jax.ShapeDtypeStruct((depth, groups, 2 * states, w), BF16),
                   jax.ShapeDtypeStruct((depth, groups, w, states), BF16),
                   jax.ShapeDtypeStruct((depth, groups, w, states), BF16),
                   jax.ShapeDtypeStruct((depth, groups, 3, 2, states, LANES), F32)],
        compiler_params=pltpu.CompilerParams(dimension_semantics=("arbitrary", "arbitrary")),
        name="s5_prep",
    )(row(lam_re), row(lam_im), col(lam_re), col(lam_im), log_step.reshape(depth, groups, 1, 1),
      c_rows(c_re), c_rows(c_im), b_lanes(b_re), b_lanes(b_im))


def _s5_kernel(u_ref, m_ref, e_ref, f_ref, apow_ref, d_ref, wglu_ref, bglu_ref, onorm_ref, y_ref,
               ub_ref, xt_ref, yt_ref, y2_ref, z_ref, s_ref, carry_ref, perm_ref):
    n_b, t_blk, d_s5 = u_ref.shape
    groups = m_ref.shape[0]
    t_c, h_c = S5_CHUNK, S5_GROUP
    n_c = t_blk // t_c
    p_c = f_ref.shape[2] // 2
    samples = n_b * n_c
    assert samples == LANES and n_c == 8

    @pl.when(pl.program_id(1) == 0)
    def _():
        carry_ref[...] = jnp.zeros_like(carry_ref)

    @pl.when((pl.program_id(0) == 0) & (pl.program_id(1) == 0))
    def _():
        dst = lax.broadcasted_iota(jnp.int32, (t_blk, t_blk), 0)
        src = lax.broadcasted_iota(jnp.int32, (t_blk, t_blk), 1)
        perm_ref[0] = jnp.where(src == (dst % n_c) * t_c + dst // n_c, 1.0, 0.0).astype(BF16)
        perm_ref[1] = jnp.where(src == (dst % t_c) * n_c + dst // t_c, 1.0, 0.0).astype(BF16)

    for b in range(n_b):
        ub_ref[b] = _dot(perm_ref[0], u_ref[b].astype(BF16))
    for t in range(t_c):
        xt_ref[t] = ub_ref[:, t * n_c:(t + 1) * n_c, :].reshape(samples, d_s5).T
    for g in range(groups):
        a = xt_ref[:, g * h_c:(g + 1) * h_c, :].reshape(t_c * h_c, samples).astype(BF16)
        yt_ref[g] = _dot(m_ref[g], a)
        z_ref[g] = _dot(e_ref[g], a)

    c_idx = lax.broadcasted_iota(jnp.int32, (p_c, LANES), 1) % n_c
    for g in range(groups):
        z_re, z_im = z_ref[g, 0:p_c], z_ref[g, p_c:2 * p_c]
        v_re = jnp.where(c_idx == 0, carry_ref[g, 0:p_c], pltpu.roll(z_re, 1, 1))
        v_im = jnp.where(c_idx == 0, carry_ref[g, p_c:2 * p_c], pltpu.roll(z_im, 1, 1))
        for k in range(3):
            hop = 1 << k
            sh_re = jnp.where(c_idx >= hop, pltpu.roll(v_re, hop, 1), 0.0)
            sh_im = jnp.where(c_idx >= hop, pltpu.roll(v_im, hop, 1), 0.0)
            d_re, d_im = _cmul(apow_ref[g, k, 0], apow_ref[g, k, 1], sh_re, sh_im)
            v_re, v_im = v_re + d_re, v_im + d_im
        s_ref[g, 0:p_c] = v_re.astype(BF16)
        s_ref[g, p_c:2 * p_c] = v_im.astype(BF16)
        o_re, o_im = _cmul(apow_ref[g, 0, 0], apow_ref[g, 0, 1], v_re, v_im)
        carry_ref[g, 0:p_c] = pltpu.roll(o_re + z_re, LANES - (n_c - 1), 1)
        carry_ref[g, p_c:2 * p_c] = pltpu.roll(o_im + z_im, LANES - (n_c - 1), 1)

    for g in range(groups):
        yt_ref[g] += _dot(f_ref[g], s_ref[g])
    for t in range(t_c):
        y2_ref[t] = yt_ref[:, t * h_c:(t + 1) * h_c, :].reshape(d_s5, samples).T
    for b in range(n_b):
        y = y2_ref[:, b * n_c:(b + 1) * n_c, :].reshape(t_blk, d_s5)
        y_hi = y.astype(BF16)
        y_lo = (y - y_hi.astype(F32)).astype(BF16)
        y2 = _dot(perm_ref[1], jnp.concatenate([y_hi, y_lo], axis=-1))
        ub_ref[b] = y2[:, :d_s5] + y2[:, d_s5:] + d_ref[...] * u_ref[b]
    fin = S5_FINISH_ROWS // t_blk
    for q in range(n_b // fin):
        y = ub_ref[q * fin:(q + 1) * fin].reshape(S5_FINISH_ROWS, d_s5)
        y = jax.nn.gelu(y)
        y = y * jax.nn.sigmoid(_dot(y.astype(BF16), wglu_ref[...]) + bglu_ref[...])
        y_ref[q * fin:(q + 1) * fin] = _rms(y, onorm_ref[...]).reshape(fin, t_blk, d_s5)


def _s5_mixer(u, m, e, f, apow, d_skip, wglu, bglu, onorm):
    batch, seq, d_s5 = u.shape
    groups, w = m.shape[0], m.shape[1]
    states = f.shape[2] // 2
    t_blk = S5_CHUNK * S5_CHUNKS
    n_b = LANES // S5_CHUNKS
    assert batch % n_b == 0 and seq % t_blk == 0
    block = pl.BlockSpec((n_b, t_blk, d_s5), lambda bb, t: (bb, t, 0))
    params, param_specs = zip(*map(_resident, (m, e, f, apow, d_skip, wglu, bglu, onorm)))
    return pl.pallas_call(
        _s5_kernel,
        grid=(batch // n_b, seq // t_blk),
        in_specs=[block, *param_specs],
        out_specs=block,
        out_shape=jax.ShapeDtypeStruct((batch, seq, d_s5), F32),
        scratch_shapes=[pltpu.VMEM((n_b, t_blk, d_s5), F32),
                        pltpu.VMEM((S5_CHUNK, d_s5, LANES), F32),
                        pltpu.VMEM((groups, w, LANES), F32),
                        pltpu.VMEM((S5_CHUNK, LANES, d_s5), F32),
                        pltpu.VMEM((groups, 2 * states, LANES), F32),
                        pltpu.VMEM((groups, 2 * states, LANES), BF16),
                        pltpu.VMEM((groups, 2 * states, LANES), F32),
                        pltpu.VMEM((2, t_blk, t_blk), BF16)],
        compiler_params=pltpu.CompilerParams(dimension_semantics=("arbitrary", "arbitrary"),
                                             vmem_limit_bytes=VMEM_LIMIT_BYTES),
        name="s5_mixer",
    )(u, *params)


def _log_sigmoid(x):
    return jnp.minimum(x, 0.0) - jnp.log1p(jnp.exp(-jnp.abs(x)))


def _gla_kernel(zg_ref, la_ref, onorm_ref, y_ref, st_ref):
    rows = zg_ref.shape[0]
    d_gla = y_ref.shape[1]
    dv = d_gla // GLA_HEADS
    dk = dv // 2
    kdim = GLA_HEADS * dk
    o_k, o_v, o_g = kdim, 2 * kdim, 2 * kdim + d_gla
    c = GLA_CHUNK
    assert 2 * dk == LANES and dv == LANES

    @pl.when(pl.program_id(1) == 0)
    def _():
        st_ref[...] = jnp.zeros_like(st_ref)

    r_i = lax.broadcasted_iota(jnp.int32, (c, 3 * c), 0)
    c_i = lax.broadcasted_iota(jnp.int32, (c, 3 * c), 1)
    tri3 = jnp.where((c_i % c) <= r_i, 1.0, 0.0).astype(BF16)
    causal = (lax.broadcasted_iota(jnp.int32, (c, LANES), 0)
              >= lax.broadcasted_iota(jnp.int32, (c, LANES), 1) % c)
    low_head = lax.broadcasted_iota(jnp.int32, (c, LANES), 1) < dk
    own_block = ((lax.broadcasted_iota(jnp.int32, (2 * dv, LANES), 0) < dv)
                 == (lax.broadcasted_iota(jnp.int32, (2 * dv, LANES), 1) < dk))
    n_chunks = rows // c
    pairs = GLA_HEADS // 2
    rows_of = lambda t, n: t[n * c:(n + 1) * c]
    zeros_v = jnp.zeros((c, dv), BF16)

    la = la_ref[...]
    la_hi = la.astype(BF16)
    rem = la - la_hi.astype(F32)
    la_mid = rem.astype(BF16)
    la_lo = (rem - la_mid.astype(F32)).astype(BF16)
    cums = [_dot(tri3, jnp.concatenate([rows_of(la_hi, n), rows_of(la_mid, n), rows_of(la_lo, n)], axis=0))
            for n in range(n_chunks)]
    cum = jnp.concatenate(cums, axis=0)
    totals = [t[c - 1:c, :] for t in cums]
    total = jnp.concatenate([jnp.broadcast_to(t, (c, kdim)) for t in totals], axis=0)
    q_dec = (zg_ref[:, 0:kdim].astype(F32) * (dk ** -0.5) * jnp.exp(cum)).astype(BF16)
    k = zg_ref[:, o_k:o_k + kdim].astype(F32)
    k_inv = k * jnp.exp(-cum)
    k_end = (k * jnp.exp(total - cum)).astype(BF16)
    v = zg_ref[:, o_v:o_v + d_gla]

    def lanes_of(t, n, pair):
        return t[n * c:(n + 1) * c, pair * LANES:(pair + 1) * LANES]

    scores = []
    for n in range(n_chunks):
        for pair in range(pairs):
            kp = lanes_of(k_inv, n, pair)
            keys = jnp.concatenate([jnp.where(low_head, kp, 0.0), jnp.where(low_head, 0.0, kp)], axis=0)
            s = _dot_nt(lanes_of(q_dec, n, pair), keys.astype(BF16))
            scores.append(jnp.where(causal, s, 0.0).astype(BF16))

    intra, upds = [], []
    for n in range(n_chunks):
        for pair in range(pairs):
            v_even = v[n * c:(n + 1) * c, (2 * pair) * dv:(2 * pair + 1) * dv]
            v_odd = v[n * c:(n + 1) * c, (2 * pair + 1) * dv:(2 * pair + 2) * dv]
            v_diag = jnp.concatenate([jnp.concatenate([v_even, zeros_v], axis=1),
                                      jnp.concatenate([zeros_v, v_odd], axis=1)], axis=0)
            intra.append(_dot(scores[n * pairs + pair], v_diag))
            upd = _dot_tn(jnp.concatenate([v_even, v_odd], axis=1), lanes_of(k_end, n, pair))
            upds.append(jnp.where(own_block, upd, 0.0))

    outs = [[None] * pairs for _ in range(n_chunks)]
    for pair in range(pairs):
        st = st_ref[pair]
        for n in range(n_chunks):
            outs[n][pair] = intra[n * pairs + pair] + _dot_nt(lanes_of(q_dec, n, pair), st.astype(BF16))
            decay = jnp.exp(totals[n][:, pair * LANES:(pair + 1) * LANES])
            st = st * decay + upds[n * pairs + pair]
        st_ref[pair] = st

    for h in range(GLA_HEADS):
        o = jnp.concatenate([outs[n][h // 2][:, (h % 2) * dv:(h % 2 + 1) * dv] for n in range(n_chunks)], axis=0)
        o = o * lax.rsqrt(jnp.mean(o * o, axis=-1, keepdims=True) + EPS)
        g = zg_ref[:, o_g + h * dv:o_g + (h + 1) * dv].astype(F32)
        y = jax.nn.silu(g) * (o * onorm_ref[:, h * dv:(h + 1) * dv])
        y_ref[:, h * dv:(h + 1) * dv] = y.astype(y_ref.dtype)


def _gla_mixer(zg, la, onorm, *, batch, d_gla):
    rows_total = zg.shape[0]
    seq = rows_total // batch
    rows = min(GLA_ROWS, seq)
    n_t = seq // rows
    token_rows = lambda w: pl.BlockSpec((rows, w), lambda b, t: (b * n_t + t, 0))
    onorm, onorm_spec = _resident(onorm)
    return pl.pallas_call(
        _gla_kernel,
        grid=(batch, n_t),
        in_specs=[token_rows(zg.shape[1]), token_rows(la.shape[1]), onorm_spec],
        out_specs=token_rows(d_gla),
        out_shape=jax.ShapeDtypeStruct((rows_total, d_gla), BF16),
        scratch_shapes=[pltpu.VMEM((GLA_HEADS // 2, 2 * d_gla // GLA_HEADS, LANES), F32)],
        compiler_params=pltpu.CompilerParams(dimension_semantics=("arbitrary", "arbitrary"),
                                             vmem_limit_bytes=VMEM_LIMIT_BYTES),
        name="gla_mixer",
    )(zg, la, onorm)


def _dense_kernel(*refs, mix, proj, d_s5, d_ff):
    refs = list(refs)
    x_ref = refs.pop(0)
    sub = x_ref.shape[0] // DENSE_SPLIT
    groups = [pl.ds(s * sub, sub) for s in range(DENSE_SPLIT)]
    xs = [x_ref[r, :] for r in groups]
    if mix:
        ys5_ref, ygla_ref, wout_ref, nffn_ref, wfi_ref, wfo_ref = refs[:6]
        refs = refs[6:]
        xs = [x + _dot(ys5_ref[r, :].astype(BF16), wout_ref[0:d_s5, :]) + _dot(ygla_ref[r, :], wout_ref[d_s5:, :])
              for x, r in zip(xs, groups)]
        hs = [_rms(x, nffn_ref[...]).astype(BF16) for x in xs]
        tiles = d_ff // MXU_DIM
        bounds = [MXU_DIM * ((tiles * c + FFN_CHUNKS - 1) // FFN_CHUNKS) for c in range(FFN_CHUNKS + 1)]
        acts = [[] for _ in range(DENSE_SPLIT)]
        for lo, hi in zip(bounds[:-1], bounds[1:]):
            for s in range(DENSE_SPLIT):
                gate = _dot(hs[s], wfi_ref[:, lo:hi])
                up = _dot(hs[s], wfi_ref[:, d_ff + lo:d_ff + hi])
                acts[s].append((jax.nn.silu(gate) * up).astype(BF16))
        xs = [x + _dot(jnp.concatenate(a, axis=1), wfo_ref[...]) for x, a in zip(xs, acts)]
    norm_ref = refs.pop(0)
    hs = [_rms(x, norm_ref[...]) for x in xs]
    if not proj:
        (out_ref,) = refs
        for h, r in zip(hs, groups):
            out_ref[r, :] = h
        return
    win_ref, bgate_ref = refs[:2]
    u_ref, zg_ref, la_ref = refs[-3:]
    o_a = d_s5 + zg_ref.shape[1]
    for x, h, r in zip(xs, hs, groups):
        if mix:
            refs[2][r, :] = x
        z = _dot(h.astype(BF16), win_ref[...])
        u_ref[r, :] = z[:, :d_s5]
        zg_ref[r, :] = z[:, d_s5:o_a].astype(BF16)
        la_ref[r, :] = _log_sigmoid(z[:, o_a:] + bgate_ref[...]) * (1.0 / GLA_GATE_NORM)


def _dense(x, mix_args, norm, proj_args, *, d_s5):
    rows_total, d_model = x.shape
    tm = DENSE_ROWS
    token_rows = lambda w: pl.BlockSpec((tm, w), lambda r: (r, 0))
    mix, proj = mix_args is not None, proj_args is not None
    args, in_specs = [x], [token_rows(d_model)]
    d_ff = 0
    if mix:
        y_s5, y_gla, w_out, norm_ffn, w_fi, w_fo = mix_args
        d_ff = w_fo.shape[0]
        assert d_ff % MXU_DIM == 0
        args += [y_s5, y_gla]
        in_specs += [token_rows(d_s5), token_rows(y_gla.shape[1])]
        resident = [w_out, norm_ffn, w_fi, w_fo, norm]
    else:
        resident = [norm]
    if proj:
        w_in, b_gate = proj_args
        la_w = b_gate.shape[1]
        zg_w = w_in.shape[1] - d_s5 - la_w
        resident += [w_in, b_gate]
        out_specs = [token_rows(d_model), token_rows(d_s5), token_rows(zg_w), token_rows(la_w)]
        out_shape = [jax.ShapeDtypeStruct((rows_total, d_model), F32),
                     jax.ShapeDtypeStruct((rows_total, d_s5), F32),
                     jax.ShapeDtypeStruct((rows_total, zg_w), BF16),
                     jax.ShapeDtypeStruct((rows_total, la_w), F32)]
        if not mix:
            out_specs, out_shape = out_specs[1:], out_shape[1:]
    else:
        out_specs = token_rows(d_model)
        out_shape = jax.ShapeDtypeStruct((rows_total, d_model), F32)
    for operand, spec in map(_resident, resident):
        args.append(operand)
        in_specs.append(spec)
    return pl.pallas_call(
        functools.partial(_dense_kernel, mix=mix, proj=proj, d_s5=d_s5, d_ff=d_ff),
        grid=(rows_total // tm,),
        in_specs=in_specs,
        out_specs=out_specs,
        out_shape=out_shape,
        compiler_params=pltpu.CompilerParams(dimension_semantics=("arbitrary",),
                                             vmem_limit_bytes=VMEM_LIMIT_BYTES),
        name="dense_" + ("mix" if mix else "") + ("proj" if proj else "final"),
    )(*args)


def kernel(x, norm_mix, w_in, s5_lam_re, s5_lam_im, s5_b_re, s5_b_im, s5_c_re, s5_c_im, s5_d, s5_log_step,
           s5_w_glu, s5_b_glu, s5_out_norm, gla_w_gate, gla_b_gate, gla_out_norm, w_out, norm_ffn, w_ffn_in,
           w_ffn_out, norm_final):
    batch, seq, d_model = x.shape
    depth = w_in.shape[0]
    d_s5 = s5_d.shape[1]
    d_gla = gla_out_norm.shape[1]
    kdim = gla_w_gate.shape[2]
    o_gate = d_s5 + 2 * kdim + 2 * d_gla
    assert batch % 8 == 0 and (batch * seq) % DENSE_ROWS == 0 and seq % GLA_CHUNK == 0 and seq % (S5_CHUNK * S5_CHUNKS) == 0

    s5_m, s5_e, s5_f_re, s5_f_im, s5_apow = _s5_prep(s5_lam_re, s5_lam_im, s5_log_step, s5_b_re, s5_b_im,
                                                     s5_c_re, s5_c_im)
    s5_f = jnp.concatenate([s5_f_re, s5_f_im], axis=-1)
    w_eff = _gate_fold(w_in[:, :, o_gate:], gla_w_gate)
    w_in_eff = jnp.concatenate([w_in[:, :, :o_gate].astype(BF16), w_eff], axis=-1)
    w_out_b, w_fi_b, w_fo_b, w_glu_b = (w.astype(BF16) for w in (w_out, w_ffn_in, w_ffn_out, s5_w_glu))
    layer = _Layer
    vec = lambda v, i: _Layer(v[:, None, :], i)

    xf = x.reshape(batch * seq, d_model)
    u, zg, la = _dense(xf, None, vec(norm_mix, 0), (layer(w_in_eff, 0), vec(gla_b_gate, 0)), d_s5=d_s5)
    for i in range(depth):
        y_s5 = _s5_mixer(u.reshape(batch, seq, d_s5), layer(s5_m, i), layer(s5_e, i), layer(s5_f, i),
                         layer(s5_apow, i), vec(s5_d, i), layer(w_glu_b, i), vec(s5_b_glu, i), vec(s5_out_norm, i))
        y_gla = _gla_mixer(zg, la, vec(gla_out_norm, i), batch=batch, d_gla=d_gla)
        mix_args = (y_s5.reshape(batch * seq, d_s5), y_gla, layer(w_out_b, i), vec(norm_ffn, i),
                    layer(w_fi_b, i), layer(w_fo_b, i))
        if i + 1 < depth:
            xf, u, zg, la = _dense(xf, mix_args, vec(norm_mix, i + 1),
                                   (layer(w_in_eff, i + 1), vec(gla_b_gate, i + 1)), d_s5=d_s5)
        else:
            xf = _dense(xf, mix_args, norm_final[None, :], None, d_s5=d_s5)
    return xf.reshape(batch, seq, d_model)
```

```python
import functools
from typing import NamedTuple

import jax
import jax.numpy as jnp
from jax import lax
from jax.experimental import pallas as pl
from jax.experimental.pallas import tpu as pltpu

S5_GROUP = 16
S5_STATE = 64
GLA_HEADS = 4
GLA_CHUNK = 64
GLA_GATE_NORM = 16.0
EPS = 1e-6

LANES = 128
MXU_DIM = 256
VMEM_LIMIT_BYTES = 56 * 1024 * 1024

DENSE_ROWS = 512
DENSE_SPLIT = 2
FFN_CHUNKS = 2
S5_CHUNK = 16
S5_CHUNKS = 8
S5_FINISH_ROWS = 512
S5_PREP_GROUPS = 8
GLA_ROWS = 2048

BF16 = jnp.bfloat16
F32 = jnp.float32


def _dot(a, b):
    return jnp.dot(a, b, preferred_element_type=F32)


def _dot_nt(a, b):
    return lax.dot_general(a, b, (((1,), (1,)), ((), ())), preferred_element_type=F32)


def _dot_tn(a, b):
    return lax.dot_general(a, b, (((0,), (0,)), ((), ())), preferred_element_type=F32)


def _rms(x, gain):
    return x * lax.rsqrt(jnp.mean(x * x, axis=-1, keepdims=True) + EPS) * gain


class _Layer(NamedTuple):
    stacked: jax.Array
    index: int

    @property
    def shape(self):
        return self.stacked.shape[1:]


def _resident(x):
    if isinstance(x, _Layer):
        lead, shape, operand = (x.index,), x.shape, x.stacked
        block = (None,) + shape
    else:
        lead, shape, operand = (), x.shape, x
        block = shape
    return operand, pl.BlockSpec(block, lambda *_: lead + (0,) * len(shape), pipeline_mode=pl.Buffered(1))


def _dot_split(a, b):
    a_hi, b_hi = a.astype(BF16), b.astype(BF16)
    a_lo = (a - a_hi.astype(F32)).astype(BF16)
    b_lo = (b - b_hi.astype(F32)).astype(BF16)
    return _dot(a_hi, b_hi) + _dot(a_hi, b_lo) + _dot(a_lo, b_hi)


def _gate_fold_kernel(win_gate_ref, w_gate_ref, weff_ref):
    weff_ref[0] = _dot_split(win_gate_ref[0], w_gate_ref[0]).astype(BF16)


def _gate_fold(win_gate, w_gate):
    depth, d_model, rank = win_gate.shape
    kdim = w_gate.shape[2]
    per_layer = lambda *s: pl.BlockSpec((1,) + s, lambda i: (i,) + (0,) * len(s))
    return pl.pallas_call(
        _gate_fold_kernel,
        grid=(depth,),
        in_specs=[per_layer(d_model, rank), per_layer(rank, kdim)],
        out_specs=per_layer(d_model, kdim),
        out_shape=jax.ShapeDtypeStruct((depth, d_model, kdim), BF16),
        compiler_params=pltpu.CompilerParams(dimension_semantics=("arbitrary",)),
        name="gate_fold",
    )(win_gate, w_gate)


def _cmul(ar, ai, br, bi):
    return ar * br - ai * bi, ar * bi + ai * br


def _discretise(lr, li, log_step):
    step = jnp.exp(log_step)
    mag = jnp.exp(lr * step)
    ar = mag * jnp.cos(li * step)
    ai = mag * jnp.sin(li * step)
    den = lr * lr + li * li
    fr = ((ar - 1.0) * lr + ai * li) / den
    fi = (ai * lr - (ar - 1.0) * li) / den
    return ar, ai, fr, fi


def _s5_prep_kernel(lam_re_row_ref, lam_im_row_ref, lam_re_col_ref, lam_im_col_ref, log_step_ref,
                    c_re_ref, c_im_ref, b_re_ref, b_im_ref, m_ref, e_ref, f_re_ref, f_im_ref, apow_ref):
    t_c = S5_CHUNK
    h_c = S5_GROUP
    p_c = lam_re_row_ref.shape[-1]

    def one_group(gi, _):
        log_step = log_step_ref[0, gi]
        ar, ai, _, _ = _discretise(lam_re_row_ref[0, gi], lam_im_row_ref[0, gi], log_step)
        arc, aic, frc, fic = _discretise(lam_re_col_ref[0, gi], lam_im_col_ref[0, gi], log_step)

        pr = jnp.ones((h_c, p_c), F32)
        pi = jnp.zeros((h_c, p_c), F32)
        blocks_r, blocks_i = [], []
        for _ in range(t_c):
            blocks_r.append(pr)
            blocks_i.append(pi)
            pr, pi = _cmul(pr, pi, ar, ai)
        c_re = jnp.concatenate([c_re_ref[0, gi]] * t_c, axis=0)
        c_im = jnp.concatenate([c_im_ref[0, gi]] * t_c, axis=0)
        l_re, l_im = _cmul(c_re, c_im, jnp.concatenate(blocks_r, axis=0),
                           jnp.concatenate(blocks_i, axis=0))
        f_re, f_im = _cmul(l_re, l_im, ar, ai)
        f_re_ref[0, gi] = f_re.astype(BF16)
        f_im_ref[0, gi] = (-f_im).astype(BF16)

        bw_re, bw_im = _cmul(frc, fic, b_re_ref[0, gi], b_im_ref[0, gi])
        k_wide = _dot_split(l_re, bw_re) - _dot_split(l_im, bw_im)
        per_tile = LANES // h_c
        lane_blk = lax.broadcasted_iota(jnp.int32, (t_c * h_c, LANES), 1) // h_c
        for lt in range(t_c // per_tile):
            k_tile = k_wide[:, lt * LANES:(lt + 1) * LANES]
            m = jnp.zeros((t_c * h_c, LANES), F32)
            for jj in range(per_tile):
                j = lt * per_tile + jj
                moved = k_tile if j == 0 else jnp.concatenate(
                    [jnp.zeros((j * h_c, LANES), F32), k_tile[:(t_c - j) * h_c]], axis=0)
                m = jnp.where(lane_blk == jj, moved, m)
            m_ref[0, gi, :, lt * LANES:(lt + 1) * LANES] = m.astype(BF16)

        lane_blk = lax.broadcasted_iota(jnp.int32, (p_c, LANES), 1) // h_c
        qr = jnp.ones((p_c, 1), F32)
        qi = jnp.zeros((p_c, 1), F32)
        for lt in range(t_c // per_tile - 1, -1, -1):
            pe_r = jnp.zeros((p_c, LANES), F32)
            pe_i = jnp.zeros((p_c, LANES), F32)
            for jj in range(per_tile - 1, -1, -1):
                pe_r = jnp.where(lane_blk == jj, qr, pe_r)
                pe_i = jnp.where(lane_blk == jj, qi, pe_i)
                qr, qi = _cmul(qr, qi, arc, aic)
            cols = slice(lt * LANES, (lt + 1) * LANES)
            e_re, e_im = _cmul(pe_r, pe_i, bw_re[:, cols], bw_im[:, cols])
            e_ref[0, gi, 0:p_c, cols] = e_re.astype(BF16)
            e_ref[0, gi, p_c:2 * p_c, cols] = e_im.astype(BF16)

        for k in range(3):
            apow_ref[0, gi, k, 0] = jnp.broadcast_to(qr, (p_c, LANES))
            apow_ref[0, gi, k, 1] = jnp.broadcast_to(qi, (p_c, LANES))
            qr, qi = _cmul(qr, qi, qr, qi)
        return 0

    lax.fori_loop(0, m_ref.shape[1], one_group, 0)


def _s5_prep(lam_re, lam_im, log_step, b_re, b_im, c_re, c_im):
    depth, groups, states = lam_re.shape
    t_c, h_c = S5_CHUNK, S5_GROUP
    row = lambda t: t.reshape(depth, groups, 1, states)
    col = lambda t: t.reshape(depth, groups, states, 1)
    b_lanes = lambda t: jnp.tile(t, (1, 1, 1, t_c))
    blk = lambda *s: pl.BlockSpec((1, S5_PREP_GROUPS) + s, lambda i, g: (i, g) + (0,) * len(s))
    w = t_c * h_c
    return pl.pallas_call(
        _s5_prep_kernel,
        grid=(depth, groups // S5_PREP_GROUPS),
        in_specs=[blk(1, states)] * 2 + [blk(states, 1)] * 2 + [blk(1, 1)]
        + [blk(h_c, states)] * 2 + [blk(states, w)] * 2,
        out_specs=[blk(w, w), blk(2 * states, w), blk(w, states), blk(w, states), blk(3, 2, states, LANES)],
        out_shape=[jax.ShapeDtypeStruct((depth, groups, w, w), BF16),
                   jax.ShapeDtypeStruct((depth, groups, 2 * states, w), BF16),
                   jax.ShapeDtypeStruct((depth, groups, w, states), BF16),
                   jax.ShapeDtypeStruct((depth, groups, w, states), BF16),
                   jax.ShapeDtypeStruct((depth, groups, 3, 2, states, LANES), F32)],
        compiler_params=pltpu.CompilerParams(dimension_semantics=("arbitrary", "arbitrary")),
        name="s5_prep",
    )(row(lam_re), row(lam_im), col(lam_re), col(lam_im), log_step.reshape(depth, groups, 1, 1),
      c_re, c_im, b_lanes(b_re), b_lanes(b_im))


def _s5_kernel(u_ref, m_ref, e_ref, f_ref, apow_ref, d_ref, wglu_ref, bglu_ref, onorm_ref, y_ref,
               ub_ref, xt_ref, yt_ref, y2_ref, z_ref, s_ref, carry_ref, perm_ref):
    n_b, t_blk, d_s5 = u_ref.shape
    groups = m_ref.shape[0]
    t_c, h_c = S5_CHUNK, S5_GROUP
    n_c = t_blk // t_c
    p_c = f_ref.shape[2] // 2
    samples = n_b * n_c
    assert samples == LANES and n_c == 8

    @pl.when(pl.program_id(1) == 0)
    def _():
        carry_ref[...] = jnp.zeros_like(carry_ref)

    @pl.when((pl.program_id(0) == 0) & (pl.program_id(1) == 0))
    def _():
        dst = lax.broadcasted_iota(jnp.int32, (t_blk, t_blk), 0)
        src = lax.broadcasted_iota(jnp.int32, (t_blk, t_blk), 1)
        perm_ref[0] = jnp.where(src == (dst % n_c) * t_c + dst // n_c, 1.0, 0.0).astype(BF16)
        perm_ref[1] = jnp.where(src == (dst % t_c) * n_c + dst // t_c, 1.0, 0.0).astype(BF16)

    for b in range(n_b):
        ub_ref[b] = _dot(perm_ref[0], u_ref[b].astype(BF16))
    for t in range(t_c):
        xt_ref[t] = ub_ref[:, t * n_c:(t + 1) * n_c, :].reshape(samples, d_s5).T
    for g in range(groups):
        a = xt_ref[:, g * h_c:(g + 1) * h_c, :].reshape(t_c * h_c, samples).astype(BF16)
        yt_ref[g] = _dot(m_ref[g], a)
        z_ref[g] = _dot(e_ref[g], a)

    c_idx = lax.broadcasted_iota(jnp.int32, (p_c, LANES), 1) % n_c
    for g in range(groups):
        z_re, z_im = z_ref[g, 0:p_c], z_ref[g, p_c:2 * p_c]
        v_re = jnp.where(c_idx == 0, carry_ref[g, 0:p_c], pltpu.roll(z_re, 1, 1))
        v_im = jnp.where(c_idx == 0, carry_ref[g, p_c:2 * p_c], pltpu.roll(z_im, 1, 1))
        for k in range(3):
            hop = 1 << k
            sh_re = jnp.where(c_idx >= hop, pltpu.roll(v_re, hop, 1), 0.0)
            sh_im = jnp.where(c_idx >= hop, pltpu.roll(v_im, hop, 1), 0.0)
            d_re, d_im = _cmul(apow_ref[g, k, 0], apow_ref[g, k, 1], sh_re, sh_im)
            v_re, v_im = v_re + d_re, v_im + d_im
        s_ref[g, 0:p_c] = v_re.astype(BF16)
        s_ref[g, p_c:2 * p_c] = v_im.astype(BF16)
        o_re, o_im = _cmul(apow_ref[g, 0, 0], apow_ref[g, 0, 1], v_re, v_im)
        carry_ref[g, 0:p_c] = pltpu.roll(o_re + z_re, LANES - (n_c - 1), 1)
        carry_ref[g, p_c:2 * p_c] = pltpu.roll(o_im + z_im, LANES - (n_c - 1), 1)

    for g in range(groups):
        yt_ref[g] += _dot(f_ref[g], s_ref[g])
    for t in range(t_c):
        y2_ref[t] = yt_ref[:, t * h_c:(t + 1) * h_c, :].reshape(d_s5, samples).T
    for b in range(n_b):
        y = y2_ref[:, b * n_c:(b + 1) * n_c, :].reshape(t_blk, d_s5)
        y_hi = y.astype(BF16)
        y_lo = (y - y_hi.astype(F32)).astype(BF16)
        y2 = _dot(perm_ref[1], jnp.concatenate([y_hi, y_lo], axis=-1))
        ub_ref[b] = y2[:, :d_s5] + y2[:, d_s5:] + d_ref[...] * u_ref[b]
    fin = S5_FINISH_ROWS // t_blk
    for q in range(n_b // fin):
        y = ub_ref[q * fin:(q + 1) * fin].reshape(S5_FINISH_ROWS, d_s5)
        y = jax.nn.gelu(y)
        y = y * jax.nn.sigmoid(_dot(y.astype(BF16), wglu_ref[...]) + bglu_ref[...])
        y_ref[q * fin:(q + 1) * fin] = _rms(y, onorm_ref[...]).reshape(fin, t_blk, d_s5)


def _s5_mixer(u, m, e, f, apow, d_skip, wglu, bglu, onorm):
    batch, seq, d_s5 = u.shape
    groups, w = m.shape[0], m.shape[1]
    states = f.shape[2] // 2
    t_blk = S5_CHUNK * S5_CHUNKS
    n_b = LANES // S5_CHUNKS
    assert batch % n_b == 0 and seq % t_blk == 0
    block = pl.BlockSpec((n_b, t_blk, d_s5), lambda bb, t: (bb, t, 0))
    params, param_specs = zip(*map(_resident, (m, e, f, apow, d_skip, wglu, bglu, onorm)))
    return pl.pallas_call(
        _s5_kernel,
        grid=(batch // n_b, seq // t_blk),
        in_specs=[block, *param_specs],
        out_specs=block,
        out_shape=jax.ShapeDtypeStruct((batch, seq, d_s5), F32),
        scratch_shapes=[pltpu.VMEM((n_b, t_blk, d_s5), F32),
                        pltpu.VMEM((S5_CHUNK, d_s5, LANES), F32),
                        pltpu.VMEM((groups, w, LANES), F32),
                        pltpu.VMEM((S5_CHUNK, LANES, d_s5), F32),
                        pltpu.VMEM((groups, 2 * states, LANES), F32),
                        pltpu.VMEM((groups, 2 * states, LANES), BF16),
                        pltpu.VMEM((groups, 2 * states, LANES), F32),
                        pltpu.VMEM((2, t_blk, t_blk), BF16)],
        compiler_params=pltpu.CompilerParams(dimension_semantics=("arbitrary", "arbitrary"),
                                             vmem_limit_bytes=VMEM_LIMIT_BYTES),
        name="s5_mixer",
    )(u, *params)


def _log_sigmoid(x):
    return jnp.minimum(x, 0.0) - jnp.log1p(jnp.exp(-jnp.abs(x)))


def _gla_kernel(zg_ref, la_ref, onorm_ref, y_ref, st_ref):
    rows = zg_ref.shape[0]
    d_gla = y_ref.shape[1]
    dv = d_gla // GLA_HEADS
    dk = dv // 2
    kdim = GLA_HEADS * dk
    o_k, o_v, o_g = kdim, 2 * kdim, 2 * kdim + d_gla
    c = GLA_CHUNK
    assert 2 * dk == LANES and dv == LANES

    @pl.when(pl.program_id(1) == 0)
    def _():
        st_ref[...] = jnp.zeros_like(st_ref)

    r_i = lax.broadcasted_iota(jnp.int32, (c, 3 * c), 0)
    c_i = lax.broadcasted_iota(jnp.int32, (c, 3 * c), 1)
    tri3 = jnp.where((c_i % c) <= r_i, 1.0, 0.0).astype(BF16)
    causal = (lax.broadcasted_iota(jnp.int32, (c, LANES), 0)
              >= lax.broadcasted_iota(jnp.int32, (c, LANES), 1) % c)
    low_head = lax.broadcasted_iota(jnp.int32, (c, LANES), 1) < dk
    own_block = ((lax.broadcasted_iota(jnp.int32, (2 * dv, LANES), 0) < dv)
                 == (lax.broadcasted_iota(jnp.int32, (2 * dv, LANES), 1) < dk))
    n_chunks = rows // c
    pairs = GLA_HEADS // 2
    rows_of = lambda t, n: t[n * c:(n + 1) * c]
    zeros_v = jnp.zeros((c, dv), BF16)

    la = la_ref[...]
    la_hi = la.astype(BF16)
    rem = la - la_hi.astype(F32)
    la_mid = rem.astype(BF16)
    la_lo = (rem - la_mid.astype(F32)).astype(BF16)
    cums = [_dot(tri3, jnp.concatenate([rows_of(la_hi, n), rows_of(la_mid, n), rows_of(la_lo, n)], axis=0))
            for n in range(n_chunks)]
    cum = jnp.concatenate(cums, axis=0)
    totals = [t[c - 1:c, :] for t in cums]
    total = jnp.concatenate([jnp.broadcast_to(t, (c, kdim)) for t in totals], axis=0)
    q_dec = (zg_ref[:, 0:kdim].astype(F32) * (dk ** -0.5) * jnp.exp(cum)).astype(BF16)
    k = zg_ref[:, o_k:o_k + kdim].astype(F32)
    k_inv = k * jnp.exp(-cum)
    k_end = (k * jnp.exp(total - cum)).astype(BF16)
    v = zg_ref[:, o_v:o_v + d_gla]

    def lanes_of(t, n, pair):
        return t[n * c:(n + 1) * c, pair * LANES:(pair + 1) * LANES]

    scores = []
    for n in range(n_chunks):
        for pair in range(pairs):
            kp = lanes_of(k_inv, n, pair)
            keys = jnp.concatenate([jnp.where(low_head, kp, 0.0), jnp.where(low_head, 0.0, kp)], axis=0)
            s = _dot_nt(lanes_of(q_dec, n, pair), keys.astype(BF16))
            scores.append(jnp.where(causal, s, 0.0).astype(BF16))

    intra, upds = [], []
    for n in range(n_chunks):
        for pair in range(pairs):
            v_even = v[n * c:(n + 1) * c, (2 * pair) * dv:(2 * pair + 1) * dv]
            v_odd = v[n * c:(n + 1) * c, (2 * pair + 1) * dv:(2 * pair + 2) * dv]
            v_diag = jnp.concatenate([jnp.concatenate([v_even, zeros_v], axis=1),
                                      jnp.concatenate([zeros_v, v_odd], axis=1)], axis=0)
            intra.append(_dot(scores[n * pairs + pair], v_diag))
            upd = _dot_tn(jnp.concatenate([v_even, v_odd], axis=1), lanes_of(k_end, n, pair))
            upds.append(jnp.where(own_block, upd, 0.0))

    outs = [[None] * pairs for _ in range(n_chunks)]
    for pair in range(pairs):
        st = st_ref[pair]
        for n in range(n_chunks):
            outs[n][pair] = intra[n * pairs + pair] + _dot_nt(lanes_of(q_dec, n, pair), st.astype(BF16))
            decay = jnp.exp(totals[n][:, pair * LANES:(pair + 1) * LANES])
            st = st * decay + upds[n * pairs + pair]
        st_ref[pair] = st

    for h in range(GLA_HEADS):
        o = jnp.concatenate([outs[n][h // 2][:, (h % 2) * dv:(h % 2 + 1) * dv] for n in range(n_chunks)], axis=0)
        o = o * lax.rsqrt(jnp.mean(o * o, axis=-1, keepdims=True) + EPS)
        g = zg_ref[:, o_g + h * dv:o_g + (h + 1) * dv].astype(F32)
        y = jax.nn.silu(g) * (o * onorm_ref[:, h * dv:(h + 1) * dv])
        y_ref[:, h * dv:(h + 1) * dv] = y.astype(y_ref.dtype)


def _gla_mixer(zg, la, onorm, *, batch, d_gla):
    rows_total = zg.shape[0]
    seq = rows_total // batch
    rows = min(GLA_ROWS, seq)
    n_t = seq // rows
    token_rows = lambda w: pl.BlockSpec((rows, w), lambda b, t: (b * n_t + t, 0))
    onorm, onorm_spec = _resident(onorm)
    return pl.pallas_call(
        _gla_kernel,
        grid=(batch, n_t),
        in_specs=[token_rows(zg.shape[1]), token_rows(la.shape[1]), onorm_spec],
        out_specs=token_rows(d_gla),
        out_shape=jax.ShapeDtypeStruct((rows_total, d_gla), BF16),
        scratch_shapes=[pltpu.VMEM((GLA_HEADS // 2, 2 * d_gla // GLA_HEADS, LANES), F32)],
        compiler_params=pltpu.CompilerParams(dimension_semantics=("arbitrary", "arbitrary"),
                                             vmem_limit_bytes=VMEM_LIMIT_BYTES),
        name="gla_mixer",
    )(zg, la, onorm)


def _dense_kernel(*refs, mix, proj, d_s5, d_ff):
    refs = list(refs)
    x_ref = refs.pop(0)
    sub = x_ref.shape[0] // DENSE_SPLIT
    groups = [pl.ds(s * sub, sub) for s in range(DENSE_SPLIT)]
    xs = [x_ref[r, :] for r in groups]
    if mix:
        ys5_ref, ygla_ref, wout_ref, nffn_ref, wfi_ref, wfo_ref = refs[:6]
        refs = refs[6:]
        xs = [x + _dot(ys5_ref[r, :].astype(BF16), wout_ref[0:d_s5, :]) + _dot(ygla_ref[r, :], wout_ref[d_s5:, :])
              for x, r in zip(xs, groups)]
        hs = [_rms(x, nffn_ref[...]).astype(BF16) for x in xs]
        tiles = d_ff // MXU_DIM
        bounds = [MXU_DIM * ((tiles * c + FFN_CHUNKS - 1) // FFN_CHUNKS) for c in range(FFN_CHUNKS + 1)]
        acts = [[] for _ in range(DENSE_SPLIT)]
        for lo, hi in zip(bounds[:-1], bounds[1:]):
            for s in range(DENSE_SPLIT):
                gate = _dot(hs[s], wfi_ref[:, lo:hi])
                up = _dot(hs[s], wfi_ref[:, d_ff + lo:d_ff + hi])
                acts[s].append((jax.nn.silu(gate) * up).astype(BF16))
        xs = [x + _dot(jnp.concatenate(a, axis=1), wfo_ref[...]) for x, a in zip(xs, acts)]
    norm_ref = refs.pop(0)
    hs = [_rms(x, norm_ref[...]) for x in xs]
    if not proj:
        (out_ref,) = refs
        for h, r in zip(hs, groups):
            out_ref[r, :] = h
        return
    win_ref, weff_ref, bgate_ref = refs[:3]
    u_ref, zg_ref, la_ref = refs[-3:]
    o_a = d_s5 + zg_ref.shape[1]
    for x, h, r in zip(xs, hs, groups):
        if mix:
            refs[3][r, :] = x
        hb = h.astype(BF16)
        z = _dot(hb, win_ref[:, :o_a])
        u_ref[r, :] = z[:, :d_s5]
        zg_ref[r, :] = z[:, d_s5:].astype(BF16)
        la_ref[r, :] = _log_sigmoid(_dot(hb, weff_ref[...]) + bgate_ref[...]) * (1.0 / GLA_GATE_NORM)


def _dense(x, mix_args, norm, proj_args, *, d_s5):
    rows_total, d_model = x.shape
    tm = DENSE_ROWS
    token_rows = lambda w: pl.BlockSpec((tm, w), lambda r: (r, 0))
    mix, proj = mix_args is not None, proj_args is not None
    args, in_specs = [x], [token_rows(d_model)]
    d_ff = 0
    if mix:
        y_s5, y_gla, w_out, norm_ffn, w_fi, w_fo = mix_args
        d_ff = w_fo.shape[0]
        assert d_ff % MXU_DIM == 0
        args += [y_s5, y_gla]
        in_specs += [token_rows(d_s5), token_rows(y_gla.shape[1])]
        resident = [w_out, norm_ffn, w_fi, w_fo, norm]
    else:
        resident = [norm]
    if proj:
        w_in, w_eff, b_gate, zg_w = proj_args
        la_w = b_gate.shape[1]
        resident += [w_in, w_eff, b_gate]
        out_specs = [token_rows(d_model), token_rows(d_s5), token_rows(zg_w), token_rows(la_w)]
        out_shape = [jax.ShapeDtypeStruct((rows_total, d_model), F32),
                     jax.ShapeDtypeStruct((rows_total, d_s5), F32),
                     jax.ShapeDtypeStruct((rows_total, zg_w), BF16),
                     jax.ShapeDtypeStruct((rows_total, la_w), F32)]
        if not mix:
            out_specs, out_shape = out_specs[1:], out_shape[1:]
    else:
        out_specs = token_rows(d_model)
        out_shape = jax.ShapeDtypeStruct((rows_total, d_model), F32)
    for operand, spec in map(_resident, resident):
        args.append(operand)
        in_specs.append(spec)
    return pl.pallas_call(
        functools.partial(_dense_kernel, mix=mix, proj=proj, d_s5=d_s5, d_ff=d_ff),
        grid=(rows_total // tm,),
        in_specs=in_specs,
        out_specs=out_specs,
        out_shape=out_shape,
        compiler_params=pltpu.CompilerParams(dimension_semantics=("arbitrary",),
                                             vmem_limit_bytes=VMEM_LIMIT_BYTES),
        name="dense_" + ("mix" if mix else "") + ("proj" if proj else "final"),
    )(*args)


def kernel(x, norm_mix, w_in, s5_lam_re, s5_lam_im, s5_b_re, s5_b_im, s5_c_re, s5_c_im, s5_d, s5_log_step,
           s5_w_glu, s5_b_glu, s5_out_norm, gla_w_gate, gla_b_gate, gla_out_norm, w_out, norm_ffn, w_ffn_in,
           w_ffn_out, norm_final):
    batch, seq, d_model = x.shape
    depth = w_in.shape[0]
    d_s5 = s5_d.shape[1]
    d_gla = gla_out_norm.shape[1]
    kdim = gla_w_gate.shape[2]
    o_gate = d_s5 + 2 * kdim + 2 * d_gla
    assert batch % 8 == 0 and (batch * seq) % DENSE_ROWS == 0 and seq % GLA_CHUNK == 0 and seq % (S5_CHUNK * S5_CHUNKS) == 0

    s5_m, s5_e, s5_f_re, s5_f_im, s5_apow = _s5_prep(s5_lam_re, s5_lam_im, s5_log_step, s5_b_re, s5_b_im,
                                                     s5_c_re, s5_c_im)
    s5_f = jnp.concatenate([s5_f_re, s5_f_im], axis=-1)
    w_eff = _gate_fold(w_in[:, :, o_gate:], gla_w_gate)
    layer = _Layer
    vec = lambda v, i: _Layer(v[:, None, :], i)
    w_in_b, w_out_b, w_fi_b, w_fo_b, w_glu_b = (w.astype(BF16) for w in (w_in, w_out, w_ffn_in, w_ffn_out, s5_w_glu))
    proj = lambda i: (layer(w_in_b, i), layer(w_eff, i), vec(gla_b_gate, i), o_gate - d_s5)

    xf = x.reshape(batch * seq, d_model)
    u, zg, la = _dense(xf, None, vec(norm_mix, 0), proj(0), d_s5=d_s5)
    for i in range(depth):
        y_s5 = _s5_mixer(u.reshape(batch, seq, d_s5), layer(s5_m, i), layer(s5_e, i), layer(s5_f, i),
                         layer(s5_apow, i), vec(s5_d, i), layer(w_glu_b, i), vec(s5_b_glu, i), vec(s5_out_norm, i))
        y_gla = _gla_mixer(zg, la, vec(gla_out_norm, i), batch=batch, d_gla=d_gla)
        mix_args = (y_s5.reshape(batch * seq, d_s5), y_gla, layer(w_out_b, i), vec(norm_ffn, i),
                    layer(w_fi_b, i), layer(w_fo_b, i))
        if i + 1 < depth:
            xf, u, zg, la = _dense(xf, mix_args, vec(norm_mix, i + 1), proj(i + 1), d_s5=d_s5)
        else:
            xf = _dense(xf, mix_args, norm_final[None, :], None, d_s5=d_s5)
    return xf.reshape(batch, seq, d_model)
```

```python
import functools
from typing import NamedTuple

import jax
import jax.numpy as jnp
from jax import lax
from jax.experimental import pallas as pl
from jax.experimental.pallas import tpu as pltpu

S5_GROUP = 16
GLA_HEADS = 4
GLA_CHUNK = 64
GLA_GATE_NORM = 16.0
EPS = 1e-6

LANES = 128
MXU_DIM = 256
VMEM_LIMIT_BYTES = 56 * 1024 * 1024

DENSE_ROWS = 512
DENSE_SPLIT = 2
FFN_CHUNKS = 2
S5_CHUNK = 16
S5_CHUNKS = 8
S5_FINISH_ROWS = 512
S5_PREP_GROUPS = 8
GLA_ROWS = 2048

BF16 = jnp.bfloat16
F32 = jnp.float32


def _dot(a, b):
    return jnp.dot(a, b, preferred_element_type=F32)


def _dot_nt(a, b):
    return lax.dot_general(a, b, (((1,), (1,)), ((), ())), preferred_element_type=F32)


def _dot_tn(a, b):
    return lax.dot_general(a, b, (((0,), (0,)), ((), ())), preferred_element_type=F32)


def _rms(x, gain):
    return x * lax.rsqrt(jnp.mean(x * x, axis=-1, keepdims=True) + EPS) * gain


class _Layer(NamedTuple):
    stacked: jax.Array
    index: int

    @property
    def shape(self):
        return self.stacked.shape[1:]


def _resident(x):
    if isinstance(x, _Layer):
        lead, shape, operand = (x.index,), x.shape, x.stacked
        block = (None,) + shape
    else:
        lead, shape, operand = (), x.shape, x
        block = shape
    return operand, pl.BlockSpec(block, lambda *_: lead + (0,) * len(shape), pipeline_mode=pl.Buffered(1))


def _dot_split(a, b):
    a_hi, b_hi = a.astype(BF16), b.astype(BF16)
    a_lo = (a - a_hi.astype(F32)).astype(BF16)
    b_lo = (b - b_hi.astype(F32)).astype(BF16)
    return _dot(a_hi, b_hi) + _dot(a_hi, b_lo) + _dot(a_lo, b_hi)


def _gate_fold_kernel(win_gate_ref, w_gate_ref, weff_ref):
    weff_ref[0] = _dot_split(win_gate_ref[0], w_gate_ref[0]).astype(BF16)


def _gate_fold(win_gate, w_gate):
    depth, d_model, rank = win_gate.shape
    kdim = w_gate.shape[2]
    per_layer = lambda *s: pl.BlockSpec((1,) + s, lambda i: (i,) + (0,) * len(s))
    return pl.pallas_call(
        _gate_fold_kernel,
        grid=(depth,),
        in_specs=[per_layer(d_model, rank), per_layer(rank, kdim)],
        out_specs=per_layer(d_model, kdim),
        out_shape=jax.ShapeDtypeStruct((depth, d_model, kdim), BF16),
        compiler_params=pltpu.CompilerParams(dimension_semantics=("arbitrary",)),
        name="gate_fold",
    )(win_gate, w_gate)


def _cmul(ar, ai, br, bi):
    return ar * br - ai * bi, ar * bi + ai * br


def _discretise(lr, li, log_step):
    step = jnp.exp(log_step)
    mag = jnp.exp(lr * step)
    ar = mag * jnp.cos(li * step)
    ai = mag * jnp.sin(li * step)
    den = lr * lr + li * li
    fr = ((ar - 1.0) * lr + ai * li) / den
    fi = (ai * lr - (ar - 1.0) * li) / den
    return ar, ai, fr, fi


def _s5_prep_kernel(lam_re_row_ref, lam_im_row_ref, lam_re_col_ref, lam_im_col_ref, log_step_ref,
                    c_re_ref, c_im_ref, b_re_ref, b_im_ref, m_ref, e_ref, f_re_ref, f_im_ref, apow_ref):
    t_c = S5_CHUNK
    h_c = S5_GROUP
    p_c = lam_re_row_ref.shape[-1]

    def one_group(gi, _):
        log_step = log_step_ref[0, gi]
        ar, ai, _, _ = _discretise(lam_re_row_ref[0, gi], lam_im_row_ref[0, gi], log_step)
        arc, aic, frc, fic = _discretise(lam_re_col_ref[0, gi], lam_im_col_ref[0, gi], log_step)

        pr = jnp.ones((h_c, p_c), F32)
        pi = jnp.zeros((h_c, p_c), F32)
        blocks_r, blocks_i = [], []
        for _ in range(t_c):
            blocks_r.append(pr)
            blocks_i.append(pi)
            pr, pi = _cmul(pr, pi, ar, ai)
        l_re, l_im = _cmul(c_re_ref[0, gi], c_im_ref[0, gi], jnp.concatenate(blocks_r, axis=0),
                           jnp.concatenate(blocks_i, axis=0))
        f_re, f_im = _cmul(l_re, l_im, ar, ai)
        f_re_ref[0, gi] = f_re.astype(BF16)
        f_im_ref[0, gi] = (-f_im).astype(BF16)

        bw_re, bw_im = _cmul(frc, fic, b_re_ref[0, gi], b_im_ref[0, gi])
        k_wide = _dot_split(l_re, bw_re) - _dot_split(l_im, bw_im)
        per_tile = LANES // h_c
        lane_blk = lax.broadcasted_iota(jnp.int32, (t_c * h_c, LANES), 1) // h_c
        for lt in range(t_c // per_tile):
            k_tile = k_wide[:, lt * LANES:(lt + 1) * LANES]
            m = jnp.zeros((t_c * h_c, LANES), F32)
            for jj in range(per_tile):
                j = lt * per_tile + jj
                moved = k_tile if j == 0 else jnp.concatenate(
                    [jnp.zeros((j * h_c, LANES), F32), k_tile[:(t_c - j) * h_c]], axis=0)
                m = jnp.where(lane_blk == jj, moved, m)
            m_ref[0, gi, :, lt * LANES:(lt + 1) * LANES] = m.astype(BF16)

        lane_blk = lax.broadcasted_iota(jnp.int32, (p_c, LANES), 1) // h_c
        qr = jnp.ones((p_c, 1), F32)
        qi = jnp.zeros((p_c, 1), F32)
        for lt in range(t_c // per_tile - 1, -1, -1):
            pe_r = jnp.zeros((p_c, LANES), F32)
            pe_i = jnp.zeros((p_c, LANES), F32)
            for jj in range(per_tile - 1, -1, -1):
                pe_r = jnp.where(lane_blk == jj, qr, pe_r)
                pe_i = jnp.where(lane_blk == jj, qi, pe_i)
                qr, qi = _cmul(qr, qi, arc, aic)
            cols = slice(lt * LANES, (lt + 1) * LANES)
            e_re, e_im = _cmul(pe_r, pe_i, bw_re[:, cols], bw_im[:, cols])
            e_ref[0, gi, 0:p_c, cols] = e_re.astype(BF16)
            e_ref[0, gi, p_c:2 * p_c, cols] = e_im.astype(BF16)

        for k in range(3):
            apow_ref[0, gi, k, 0] = jnp.broadcast_to(qr, (p_c, LANES))
            apow_ref[0, gi, k, 1] = jnp.broadcast_to(qi, (p_c, LANES))
            qr, qi = _cmul(qr, qi, qr, qi)
        return 0

    lax.fori_loop(0, m_ref.shape[1], one_group, 0)


def _s5_prep(lam_re, lam_im, log_step, b_re, b_im, c_re, c_im):
    depth, groups, states = lam_re.shape
    t_c, h_c = S5_CHUNK, S5_GROUP
    row = lambda t: t.reshape(depth, groups, 1, states)
    col = lambda t: t.reshape(depth, groups, states, 1)
    c_rows = lambda t: jnp.tile(t, (1, 1, t_c, 1))
    b_lanes = lambda t: jnp.tile(t, (1, 1, 1, t_c))
    blk = lambda *s: pl.BlockSpec((1, S5_PREP_GROUPS) + s, lambda i, g: (i, g) + (0,) * len(s))
    w = t_c * h_c
    return pl.pallas_call(
        _s5_prep_kernel,
        grid=(depth, groups // S5_PREP_GROUPS),
        in_specs=[blk(1, states)] * 2 + [blk(states, 1)] * 2 + [blk(1, 1)]
        + [blk(w, states)] * 2 + [blk(states, w)] * 2,
        out_specs=[blk(w, w), blk(2 * states, w), blk(w, states), blk(w, states), blk(3, 2, states, LANES)],
        out_shape=[jax.ShapeDtypeStruct((depth, groups, w, w), BF16),
                   jax.ShapeDtypeStruct((depth, groups, 2 * states, w), BF16),
                   jax.ShapeDtypeStruct((depth, groups, w, states), BF16),
                   jax.ShapeDtypeStruct((depth, groups, w, states), BF16),
                   jax.ShapeDtypeStruct((depth, groups, 3, 2, states, LANES), F32)],
        compiler_params=pltpu.CompilerParams(dimension_semantics=("arbitrary", "arbitrary")),
        name="s5_prep",
    )(row(lam_re), row(lam_im), col(lam_re), col(lam_im), log_step.reshape(depth, groups, 1, 1),
      c_rows(c_re), c_rows(c_im), b_lanes(b_re), b_lanes(b_im))


def _s5_kernel(u_ref, m_ref, e_ref, f_ref, apow_ref, d_ref, wglu_ref, bglu_ref, onorm_ref, y_ref,
               ub_ref, xt_ref, yt_ref, y2_ref, z_ref, s_ref, carry_ref, perm_ref):
    n_b, t_blk, d_s5 = u_ref.shape
    groups = m_ref.shape[0]
    t_c, h_c = S5_CHUNK, S5_GROUP
    n_c = t_blk // t_c
    p_c = f_ref.shape[2] // 2
    samples = n_b * n_c
    assert samples == LANES and n_c == 8

    @pl.when(pl.program_id(1) == 0)
    def _():
        carry_ref[...] = jnp.zeros_like(carry_ref)

    @pl.when((pl.program_id(0) == 0) & (pl.program_id(1) == 0))
    def _():
        dst = lax.broadcasted_iota(jnp.int32, (t_blk, t_blk), 0)
        src = lax.broadcasted_iota(jnp.int32, (t_blk, t_blk), 1)
        perm_ref[0] = jnp.where(src == (dst % n_c) * t_c + dst // n_c, 1.0, 0.0).astype(BF16)
        perm_ref[1] = jnp.where(src == (dst % t_c) * n_c + dst // t_c, 1.0, 0.0).astype(BF16)

    for b in range(n_b):
        ub_ref[b] = _dot(perm_ref[0], u_ref[b].astype(BF16))
    for t in range(t_c):
        xt_ref[t] = ub_ref[:, t * n_c:(t + 1) * n_c, :].reshape(samples, d_s5).T
    for g in range(groups):
        a = xt_ref[:, g * h_c:(g + 1) * h_c, :].reshape(t_c * h_c, samples).astype(BF16)
        yt_ref[g] = _dot(m_ref[g], a)
        z_ref[g] = _dot(e_ref[g], a)

    c_idx = lax.broadcasted_iota(jnp.int32, (p_c, LANES), 1) % n_c
    for g in range(groups):
        z_re, z_im = z_ref[g, 0:p_c], z_ref[g, p_c:2 * p_c]
        v_re = jnp.where(c_idx == 0, carry_ref[g, 0:p_c], pltpu.roll(z_re, 1, 1))
        v_im = jnp.where(c_idx == 0, carry_ref[g, p_c:2 * p_c], pltpu.roll(z_im, 1, 1))
        for k in range(3):
            hop = 1 << k
            sh_re = jnp.where(c_idx >= hop, pltpu.roll(v_re, hop, 1), 0.0)
            sh_im = jnp.where(c_idx >= hop, pltpu.roll(v_im, hop, 1), 0.0)
            d_re, d_im = _cmul(apow_ref[g, k, 0], apow_ref[g, k, 1], sh_re, sh_im)
            v_re, v_im = v_re + d_re, v_im + d_im
        s_ref[g, 0:p_c] = v_re.astype(BF16)
        s_ref[g, p_c:2 * p_c] = v_im.astype(BF16)
        o_re, o_im = _cmul(apow_ref[g, 0, 0], apow_ref[g, 0, 1], v_re, v_im)
        carry_ref[g, 0:p_c] = pltpu.roll(o_re + z_re, LANES - (n_c - 1), 1)
        carry_ref[g, p_c:2 * p_c] = pltpu.roll(o_im + z_im, LANES - (n_c - 1), 1)

    for g in range(groups):
        yt_ref[g] += _dot(f_ref[g], s_ref[g])
    for t in range(t_c):
        y2_ref[t] = yt_ref[:, t * h_c:(t + 1) * h_c, :].reshape(d_s5, samples).T
    for b in range(n_b):
        y = y2_ref[:, b * n_c:(b + 1) * n_c, :].reshape(t_blk, d_s5)
        y_hi = y.astype(BF16)
        y_lo = (y - y_hi.astype(F32)).astype(BF16)
        y2 = _dot(perm_ref[1], jnp.concatenate([y_hi, y_lo], axis=-1))
        ub_ref[b] = y2[:, :d_s5] + y2[:, d_s5:] + d_ref[...] * u_ref[b]
    fin = S5_FINISH_ROWS // t_blk
    for q in range(n_b // fin):
        y = ub_ref[q * fin:(q + 1) * fin].reshape(S5_FINISH_ROWS, d_s5)
        y = jax.nn.gelu(y)
        y = y * jax.nn.sigmoid(_dot(y.astype(BF16), wglu_ref[...]) + bglu_ref[...])
        y_ref[q * fin:(q + 1) * fin] = _rms(y, onorm_ref[...]).reshape(fin, t_blk, d_s5)


def _s5_mixer(u, m, e, f, apow, d_skip, wglu, bglu, onorm):
    batch, seq, d_s5 = u.shape
    groups, w = m.shape[0], m.shape[1]
    states = f.shape[2] // 2
    t_blk = S5_CHUNK * S5_CHUNKS
    n_b = LANES // S5_CHUNKS
    assert batch % n_b == 0 and seq % t_blk == 0
    block = pl.BlockSpec((n_b, t_blk, d_s5), lambda bb, t: (bb, t, 0))
    params, param_specs = zip(*map(_resident, (m, e, f, apow, d_skip, wglu, bglu, onorm)))
    return pl.pallas_call(
        _s5_kernel,
        grid=(batch // n_b, seq // t_blk),
        in_specs=[block, *param_specs],
        out_specs=block,
        out_shape=jax.ShapeDtypeStruct((batch, seq, d_s5), F32),
        scratch_shapes=[pltpu.VMEM((n_b, t_blk, d_s5), F32),
                        pltpu.VMEM((S5_CHUNK, d_s5, LANES), F32),
                        pltpu.VMEM((groups, w, LANES), F32),
                        pltpu.VMEM((S5_CHUNK, LANES, d_s5), F32),
                        pltpu.VMEM((groups, 2 * states, LANES), F32),
                        pltpu.VMEM((groups, 2 * states, LANES), BF16),
                        pltpu.VMEM((groups, 2 * states, LANES), F32),
                        pltpu.VMEM((2, t_blk, t_blk), BF16)],
        compiler_params=pltpu.CompilerParams(dimension_semantics=("arbitrary", "arbitrary"),
                                             vmem_limit_bytes=VMEM_LIMIT_BYTES),
        name="s5_mixer",
    )(u, *params)


def _log_sigmoid(x):
    return jnp.minimum(x, 0.0) - jnp.log1p(jnp.exp(-jnp.abs(x)))


def _gla_kernel(zg_ref, la_ref, onorm_ref, y_ref, st_ref):
    rows = zg_ref.shape[0]
    d_gla = y_ref.shape[1]
    dv = d_gla // GLA_HEADS
    dk = dv // 2
    kdim = GLA_HEADS * dk
    o_k, o_v, o_g = kdim, 2 * kdim, 2 * kdim + d_gla
    c = GLA_CHUNK
    assert 2 * dk == LANES and dv == LANES

    @pl.when(pl.program_id(1) == 0)
    def _():
        st_ref[...] = jnp.zeros_like(st_ref)

    r_i = lax.broadcasted_iota(jnp.int32, (c, 3 * c), 0)
    c_i = lax.broadcasted_iota(jnp.int32, (c, 3 * c), 1)
    tri3 = jnp.where((c_i % c) <= r_i, 1.0, 0.0).astype(BF16)
    causal = (lax.broadcasted_iota(jnp.int32, (c, LANES), 0)
              >= lax.broadcasted_iota(jnp.int32, (c, LANES), 1) % c)
    low_head = lax.broadcasted_iota(jnp.int32, (c, LANES), 1) < dk
    own_block = ((lax.broadcasted_iota(jnp.int32, (2 * dv, LANES), 0) < dv)
                 == (lax.broadcasted_iota(jnp.int32, (2 * dv, LANES), 1) < dk))
    n_chunks = rows // c
    pairs = GLA_HEADS // 2
    rows_of = lambda t, n: t[n * c:(n + 1) * c]
    zeros_v = jnp.zeros((c, dv), BF16)

    la = la_ref[...]
    la_hi = la.astype(BF16)
    rem = la - la_hi.astype(F32)
    la_mid = rem.astype(BF16)
    la_lo = (rem - la_mid.astype(F32)).astype(BF16)
    cums = [_dot(tri3, jnp.concatenate([rows_of(la_hi, n), rows_of(la_mid, n), rows_of(la_lo, n)], axis=0))
            for n in range(n_chunks)]
    cum = jnp.concatenate(cums, axis=0)
    totals = [t[c - 1:c, :] for t in cums]
    total = jnp.concatenate([jnp.broadcast_to(t, (c, kdim)) for t in totals], axis=0)
    q_dec = (zg_ref[:, 0:kdim].astype(F32) * (dk ** -0.5) * jnp.exp(cum)).astype(BF16)
    k = zg_ref[:, o_k:o_k + kdim].astype(F32)
    k_inv = k * jnp.exp(-cum)
    k_end = (k * jnp.exp(total - cum)).astype(BF16)
    v = zg_ref[:, o_v:o_v + d_gla]

    def lanes_of(t, n, pair):
        return t[n * c:(n + 1) * c, pair * LANES:(pair + 1) * LANES]

    scores = []
    for n in range(n_chunks):
        for pair in range(pairs):
            kp = lanes_of(k_inv, n, pair)
            keys = jnp.concatenate([jnp.where(low_head, kp, 0.0), jnp.where(low_head, 0.0, kp)], axis=0)
            s = _dot_nt(lanes_of(q_dec, n, pair), keys.astype(BF16))
            scores.append(jnp.where(causal, s, 0.0).astype(BF16))

    intra, upds = [], []
    for n in range(n_chunks):
        for pair in range(pairs):
            v_even = v[n * c:(n + 1) * c, (2 * pair) * dv:(2 * pair + 1) * dv]
            v_odd = v[n * c:(n + 1) * c, (2 * pair + 1) * dv:(2 * pair + 2) * dv]
            v_diag = jnp.concatenate([jnp.concatenate([v_even, zeros_v], axis=1),
                                      jnp.concatenate([zeros_v, v_odd], axis=1)], axis=0)
            intra.append(_dot(scores[n * pairs + pair], v_diag))
            upd = _dot_tn(jnp.concatenate([v_even, v_odd], axis=1), lanes_of(k_end, n, pair))
            upds.append(jnp.where(own_block, upd, 0.0))

    outs = [[None] * pairs for _ in range(n_chunks)]
    for pair in range(pairs):
        st = st_ref[pair]
        for n in range(n_chunks):
            outs[n][pair] = intra[n * pairs + pair] + _dot_nt(lanes_of(q_dec, n, pair), st.astype(BF16))
            decay = jnp.exp(totals[n][:, pair * LANES:(pair + 1) * LANES])
            st = st * decay + upds[n * pairs + pair]
        st_ref[pair] = st

    for h in range(GLA_HEADS):
        o = jnp.concatenate([outs[n][h // 2][:, (h % 2) * dv:(h % 2 + 1) * dv] for n in range(n_chunks)], axis=0)
        o = o * lax.rsqrt(jnp.mean(o * o, axis=-1, keepdims=True) + EPS)
        g = zg_ref[:, o_g + h * dv:o_g + (h + 1) * dv].astype(F32)
        y = jax.nn.silu(g) * (o * onorm_ref[:, h * dv:(h + 1) * dv])
        y_ref[:, h * dv:(h + 1) * dv] = y.astype(y_ref.dtype)


def _gla_mixer(zg, la, onorm, *, batch, d_gla):
    rows_total = zg.shape[0]
    seq = rows_total // batch
    rows = min(GLA_ROWS, seq)
    n_t = seq // rows
    token_rows = lambda w: pl.BlockSpec((rows, w), lambda b, t: (b * n_t + t, 0))
    onorm, onorm_spec = _resident(onorm)
    return pl.pallas_call(
        _gla_kernel,
        grid=(batch, n_t),
        in_specs=[token_rows(zg.shape[1]), token_rows(la.shape[1]), onorm_spec],
        out_specs=token_rows(d_gla),
        out_shape=jax.ShapeDtypeStruct((rows_total, d_gla), BF16),
        scratch_shapes=[pltpu.VMEM((GLA_HEADS // 2, 2 * d_gla // GLA_HEADS, LANES), F32)],
        compiler_params=pltpu.CompilerParams(dimension_semantics=("arbitrary", "arbitrary"),
                                             vmem_limit_bytes=VMEM_LIMIT_BYTES),
        name="gla_mixer",
    )(zg, la, onorm)


def _dense_kernel(*refs, mix, proj, d_s5, d_ff):
    refs = list(refs)
    x_ref = refs.pop(0)
    sub = x_ref.shape[0] // DENSE_SPLIT
    groups = [pl.ds(s * sub, sub) for s in range(DENSE_SPLIT)]
    xs = [x_ref[r, :] for r in groups]
    if mix:
        ys5_ref, ygla_ref, wout_ref, nffn_ref, wfi_ref, wfo_ref = refs[:6]
        refs = refs[6:]
        xs = [x + _dot(ys5_ref[r, :].astype(BF16), wout_ref[0:d_s5, :]) + _dot(ygla_ref[r, :], wout_ref[d_s5:, :])
              for x, r in zip(xs, groups)]
        hs = [_rms(x, nffn_ref[...]).astype(BF16) for x in xs]
        tiles = d_ff // MXU_DIM
        bounds = [MXU_DIM * ((tiles * c + FFN_CHUNKS - 1) // FFN_CHUNKS) for c in range(FFN_CHUNKS + 1)]
        acts = [[] for _ in range(DENSE_SPLIT)]
        for lo, hi in zip(bounds[:-1], bounds[1:]):
            for s in range(DENSE_SPLIT):
                gate = _dot(hs[s], wfi_ref[:, lo:hi])
                up = _dot(hs[s], wfi_ref[:, d_ff + lo:d_ff + hi])
                acts[s].append((jax.nn.silu(gate) * up).astype(BF16))
        xs = [x + _dot(jnp.concatenate(a, axis=1), wfo_ref[...]) for x, a in zip(xs, acts)]
    norm_ref = refs.pop(0)
    hs = [_rms(x, norm_ref[...]) for x in xs]
    if not proj:
        (out_ref,) = refs
        for h, r in zip(hs, groups):
            out_ref[r, :] = h
        return
    win_ref, bgate_ref = refs[:2]
    u_ref, zg_ref, la_ref = refs[-3:]
    o_a = d_s5 + zg_ref.shape[1]
    for x, h, r in zip(xs, hs, groups):
        if mix:
            refs[2][r, :] = x
        z = _dot(h.astype(BF16), win_ref[...])
        u_ref[r, :] = z[:, :d_s5]
        zg_ref[r, :] = z[:, d_s5:o_a].astype(BF16)
        la_ref[r, :] = _log_sigmoid(z[:, o_a:] + bgate_ref[...]) * (1.0 / GLA_GATE_NORM)


def _dense(x, mix_args, norm, proj_args, *, d_s5):
    rows_total, d_model = x.shape
    tm = DENSE_ROWS
    token_rows = lambda w: pl.BlockSpec((tm, w), lambda r: (r, 0))
    mix, proj = mix_args is not None, proj_args is not None
    args, in_specs = [x], [token_rows(d_model)]
    d_ff = 0
    if mix:
        y_s5, y_gla, w_out, norm_ffn, w_fi, w_fo = mix_args
        d_ff = w_fo.shape[0]
        assert d_ff % MXU_DIM == 0
        args += [y_s5, y_gla]
        in_specs += [token_rows(d_s5), token_rows(y_gla.shape[1])]
        resident = [w_out, norm_ffn, w_fi, w_fo, norm]
    else:
        resident = [norm]
    if proj:
        w_in, b_gate = proj_args
        la_w = b_gate.shape[1]
        zg_w = w_in.shape[1] - d_s5 - la_w
        resident += [w_in, b_gate]
        out_specs = [token_rows(d_model), token_rows(d_s5), token_rows(zg_w), token_rows(la_w)]
        out_shape = [jax.ShapeDtypeStruct((rows_total, d_model), F32),
                     jax.ShapeDtypeStruct((rows_total, d_s5), F32),
                     jax.ShapeDtypeStruct((rows_total, zg_w), BF16),
                     jax.ShapeDtypeStruct((rows_total, la_w), F32)]
        if not mix:
            out_specs, out_shape = out_specs[1:], out_shape[1:]
    else:
        out_specs = token_rows(d_model)
        out_shape = jax.ShapeDtypeStruct((rows_total, d_model), F32)
    for operand, spec in map(_resident, resident):
        args.append(operand)
        in_specs.append(spec)
    return pl.pallas_call(
        functools.partial(_dense_kernel, mix=mix, proj=proj, d_s5=d_s5, d_ff=d_ff),
        grid=(rows_total // tm,),
        in_specs=in_specs,
        out_specs=out_specs,
        out_shape=out_shape,
        compiler_params=pltpu.CompilerParams(dimension_semantics=("arbitrary",),
                                             vmem_limit_bytes=VMEM_LIMIT_BYTES),
        name="dense_" + ("mix" if mix else "") + ("proj" if proj else "final"),
    )(*args)


def kernel(x, norm_mix, w_in, s5_lam_re, s5_lam_im, s5_b_re, s5_b_im, s5_c_re, s5_c_im, s5_d, s5_log_step,
           s5_w_glu, s5_b_glu, s5_out_norm, gla_w_gate, gla_b_gate, gla_out_norm, w_out, norm_ffn, w_ffn_in,
           w_ffn_out, norm_final):
    batch, seq, d_model = x.shape
    depth = w_in.shape[0]
    d_s5 = s5_d.shape[1]
    d_gla = gla_out_norm.shape[1]
    kdim = gla_w_gate.shape[2]
    o_gate = d_s5 + 2 * kdim + 2 * d_gla
    assert (batch * seq) % DENSE_ROWS == 0 and seq % GLA_CHUNK == 0 and seq % (S5_CHUNK * S5_CHUNKS) == 0

    s5_m, s5_e, s5_f_re, s5_f_im, s5_apow = _s5_prep(s5_lam_re, s5_lam_im, s5_log_step, s5_b_re, s5_b_im,
                                                     s5_c_re, s5_c_im)
    s5_f = jnp.concatenate([s5_f_re, s5_f_im], axis=-1)
    w_eff = _gate_fold(w_in[:, :, o_gate:], gla_w_gate)
    w_in_eff = jnp.concatenate([w_in[:, :, :o_gate].astype(BF16), w_eff], axis=-1)
    w_out_b, w_fi_b, w_fo_b, w_glu_b = (w.astype(BF16) for w in (w_out, w_ffn_in, w_ffn_out, s5_w_glu))
    layer = _Layer
    vec = lambda v, i: _Layer(v[:, None, :], i)

    xf = x.reshape(batch * seq, d_model)
    u, zg, la = _dense(xf, None, vec(norm_mix, 0), (layer(w_in_eff, 0), vec(gla_b_gate, 0)), d_s5=d_s5)
    for i in range(depth):
        y_s5 = _s5_mixer(u.reshape(batch, seq, d_s5), layer(s5_m, i), layer(s5_e, i), layer(s5_f, i),
                         layer(s5_apow, i), vec(s5_d, i), layer(w_glu_b, i), vec(s5_b_glu, i), vec(s5_out_norm, i))
        y_gla = _gla_mixer(zg, la, vec(gla_out_norm, i), batch=batch, d_gla=d_gla)
        mix_args = (y_s5.reshape(batch * seq, d_s5), y_gla, layer(w_out_b, i), vec(norm_ffn, i),
                    layer(w_fi_b, i), layer(w_fo_b, i))
        if i + 1 < depth:
            xf, u, zg, la = _dense(xf, mix_args, vec(norm_mix, i + 1),
                                   (layer(w_in_eff, i + 1), vec(gla_b_gate, i + 1)), d_s5=d_s5)
        else:
            xf = _dense(xf, mix_args, norm_final[None, :], None, d_s5=d_s5)
    return xf.reshape(batch, seq, d_model)
```

```python
import functools
from typing import NamedTuple

import jax
import jax.numpy as jnp
from jax import lax
from jax.experimental import pallas as pl
from jax.experimental.pallas import tpu as pltpu

S5_GROUP = 16
GLA_HEADS = 4
GLA_CHUNK = 64
GLA_GATE_NORM = 16.0
EPS = 1e-6

LANES = 128
MXU_DIM = 256
VMEM_LIMIT_BYTES = 56 * 1024 * 1024

DENSE_ROWS = 512
DENSE_SPLIT = 2
FFN_CHUNKS = 2
S5_CHUNK = 16
S5_CHUNKS = 8
S5_FINISH_ROWS = 512
S5_PREP_GROUPS = 8
GLA_ROWS = 2048

BF16 = jnp.bfloat16
F32 = jnp.float32


def _dot(a, b):
    return jnp.dot(a, b, preferred_element_type=F32)


def _dot_nt(a, b):
    return lax.dot_general(a, b, (((1,), (1,)), ((), ())), preferred_element_type=F32)


def _dot_tn(a, b):
    return lax.dot_general(a, b, (((0,), (0,)), ((), ())), preferred_element_type=F32)


def _rms(x, gain):
    return x * lax.rsqrt(jnp.mean(x * x, axis=-1, keepdims=True) + EPS) * gain


class _Layer(NamedTuple):
    stacked: jax.Array
    index: int

    @property
    def shape(self):
        return self.stacked.shape[1:]


def _resident(x):
    if isinstance(x, _Layer):
        lead, shape, operand = (x.index,), x.shape, x.stacked
        block = (None,) + shape
    else:
        lead, shape, operand = (), x.shape, x
        block = shape
    return operand, pl.BlockSpec(block, lambda *_: lead + (0,) * len(shape), pipeline_mode=pl.Buffered(1))


def _dot_split(a, b):
    a_hi, b_hi = a.astype(BF16), b.astype(BF16)
    a_lo = (a - a_hi.astype(F32)).astype(BF16)
    b_lo = (b - b_hi.astype(F32)).astype(BF16)
    return _dot(a_hi, b_hi) + _dot(a_hi, b_lo) + _dot(a_lo, b_hi)


def _gate_fold_kernel(win_gate_ref, w_gate_ref, weff_ref):
    weff_ref[0] = _dot_split(win_gate_ref[0], w_gate_ref[0]).astype(BF16)


def _gate_fold(win_gate, w_gate):
    depth, d_model, rank = win_gate.shape
    kdim = w_gate.shape[2]
    per_layer = lambda *s: pl.BlockSpec((1,) + s, lambda i: (i,) + (0,) * len(s))
    return pl.pallas_call(
        _gate_fold_kernel,
        grid=(depth,),
        in_specs=[per_layer(d_model, rank), per_layer(rank, kdim)],
        out_specs=per_layer(d_model, kdim),
        out_shape=jax.ShapeDtypeStruct((depth, d_model, kdim), BF16),
        compiler_params=pltpu.CompilerParams(dimension_semantics=("arbitrary",)),
        name="gate_fold",
    )(win_gate, w_gate)


def _cmul(ar, ai, br, bi):
    return ar * br - ai * bi, ar * bi + ai * br


def _discretise(lr, li, log_step):
    step = jnp.exp(log_step)
    mag = jnp.exp(lr * step)
    ar = mag * jnp.cos(li * step)
    ai = mag * jnp.sin(li * step)
    den = lr * lr + li * li
    fr = ((ar - 1.0) * lr + ai * li) / den
    fi = (ai * lr - (ar - 1.0) * li) / den
    return ar, ai, fr, fi


def _s5_prep_kernel(lam_re_row_ref, lam_im_row_ref, lam_re_col_ref, lam_im_col_ref, log_step_ref,
                    c_re_ref, c_im_ref, b_re_ref, b_im_ref, m_ref, e_ref, f_re_ref, f_im_ref, apow_ref):
    t_c = S5_CHUNK
    h_c = S5_GROUP
    p_c = lam_re_row_ref.shape[-1]

    def one_group(gi, _):
        log_step = log_step_ref[0, gi]
        ar, ai, _, _ = _discretise(lam_re_row_ref[0, gi], lam_im_row_ref[0, gi], log_step)
        arc, aic, frc, fic = _discretise(lam_re_col_ref[0, gi], lam_im_col_ref[0, gi], log_step)

        pr = jnp.ones((h_c, p_c), F32)
        pi = jnp.zeros((h_c, p_c), F32)
        blocks_r, blocks_i = [], []
        for _ in range(t_c):
            blocks_r.append(pr)
            blocks_i.append(pi)
            pr, pi = _cmul(pr, pi, ar, ai)
        l_re, l_im = _cmul(c_re_ref[0, gi], c_im_ref[0, gi], jnp.concatenate(blocks_r, axis=0),
                           jnp.concatenate(blocks_i, axis=0))
        f_re, f_im = _cmul(l_re, l_im, ar, ai)
        f_re_ref[0, gi] = f_re.astype(BF16)
        f_im_ref[0, gi] = (-f_im).astype(BF16)

        bw_re, bw_im = _cmul(frc, fic, b_re_ref[0, gi], b_im_ref[0, gi])
        k_wide = _dot_split(l_re, bw_re) - _dot_split(l_im, bw_im)
        per_tile = LANES // h_c
        lane_blk = lax.broadcasted_iota(jnp.int32, (t_c * h_c, LANES), 1) // h_c
        for lt in range(t_c // per_tile):
            k_tile = k_wide[:, lt * LANES:(lt + 1) * LANES]
            m = jnp.zeros((t_c * h_c, LANES), F32)
            for jj in range(per_tile):
                j = lt * per_tile + jj
                moved = k_tile if j == 0 else jnp.concatenate(
                    [jnp.zeros((j * h_c, LANES), F32), k_tile[:(t_c - j) * h_c]], axis=0)
                m = jnp.where(lane_blk == jj, moved, m)
            m_ref[0, gi, :, lt * LANES:(lt + 1) * LANES] = m.astype(BF16)

        lane_blk = lax.broadcasted_iota(jnp.int32, (p_c, LANES), 1) // h_c
        qr = jnp.ones((p_c, 1), F32)
        qi = jnp.zeros((p_c, 1), F32)
        for lt in range(t_c // per_tile - 1, -1, -1):
            pe_r = jnp.zeros((p_c, LANES), F32)
            pe_i = jnp.zeros((p_c, LANES), F32)
            for jj in range(per_tile - 1, -1, -1):
                pe_r = jnp.where(lane_blk == jj, qr, pe_r)
                pe_i = jnp.where(lane_blk == jj, qi, pe_i)
                qr, qi = _cmul(qr, qi, arc, aic)
            cols = slice(lt * LANES, (lt + 1) * LANES)
            e_re, e_im = _cmul(pe_r, pe_i, bw_re[:, cols], bw_im[:, cols])
            e_ref[0, gi, 0:p_c, cols] = e_re.astype(BF16)
            e_ref[0, gi, p_c:2 * p_c, cols] = e_im.astype(BF16)

        for k in range(3):
            apow_ref[0, gi, k, 0] = jnp.broadcast_to(qr, (p_c, LANES))
            apow_ref[0, gi, k, 1] = jnp.broadcast_to(qi, (p_c, LANES))
            qr, qi = _cmul(qr, qi, qr, qi)
        return 0

    lax.fori_loop(0, m_ref.shape[1], one_group, 0)


def _s5_prep(lam_re, lam_im, log_step, b_re, b_im, c_re, c_im):
    depth, groups, states = lam_re.shape
    t_c, h_c = S5_CHUNK, S5_GROUP
    row = lambda t: t.reshape(depth, groups, 1, states)
    col = lambda t: t.reshape(depth, groups, states, 1)
    c_rows = lambda t: jnp.tile(t, (1, 1, t_c, 1))
    b_lanes = lambda t: jnp.tile(t, (1, 1, 1, t_c))
    blk = lambda *s: pl.BlockSpec((1, S5_PREP_GROUPS) + s, lambda i, g: (i, g) + (0,) * len(s))
    w = t_c * h_c
    return pl.pallas_call(
        _s5_prep_kernel,
        grid=(depth, groups // S5_PREP_GROUPS),
        in_specs=[blk(1, states)] * 2 + [blk(states, 1)] * 2 + [blk(1, 1)]
        + [blk(w, states)] * 2 + [blk(states, w)] * 2,
        out_specs=[blk(w, w), blk(2 * states, w), blk(w, states), blk(w, states), blk(3, 2, states, LANES)],
        out_shape=[jax.ShapeDtypeStruct((depth, groups, w, w), BF16),
                   jax.ShapeDtypeStruct((depth, groups, 2 * states, w), BF16),
                   jax.ShapeDtypeStruct((depth, groups, w, states), BF16),
                   jax.ShapeDtypeStruct((depth, groups, w, states), BF16),
                   jax.ShapeDtypeStruct((depth, groups, 3, 2, states, LANES), F32)],
        compiler_params=pltpu.CompilerParams(dimension_semantics=("arbitrary", "arbitrary")),
        name="s5_prep",
    )(row(lam_re), row(lam_im), col(lam_re), col(lam_im), log_step.reshape(depth, groups, 1, 1),
      c_rows(c_re), c_rows(c_im), b_lanes(b_re), b_lanes(b_im))


def _s5_kernel(u_ref, m_ref, e_ref, f_ref, apow_ref, d_ref, wglu_ref, bglu_ref, onorm_ref, y_ref,
               ub_ref, xt_ref, yt_ref, y2_ref, z_ref, s_ref, carry_ref, perm_ref):
    n_b, t_blk, d_s5 = u_ref.shape
    groups = m_ref.shape[0]
    t_c, h_c = S5_CHUNK, S5_GROUP
    n_c = t_blk // t_c
    p_c = f_ref.shape[2] // 2
    samples = n_b * n_c
    assert samples == LANES and n_c == 8

    @pl.when(pl.program_id(1) == 0)
    def _():
        carry_ref[...] = jnp.zeros_like(carry_ref)

    @pl.when((pl.program_id(0) == 0) & (pl.program_id(1) == 0))
    def _():
        dst = lax.broadcasted_iota(jnp.int32, (t_blk, t_blk), 0)
        src = lax.broadcasted_iota(jnp.int32, (t_blk, t_blk), 1)
        perm_ref[0] = jnp.where(src == (dst % n_c) * t_c + dst // n_c, 1.0, 0.0).astype(BF16)
        perm_ref[1] = jnp.where(src == (dst % t_c) * n_c + dst // t_c, 1.0, 0.0).astype(BF16)

    for b in range(n_b):
        ub_ref[b] = _dot(perm_ref[0], u_ref[b].astype(BF16))
    for t in range(t_c):
        xt_ref[t] = ub_ref[:, t * n_c:(t + 1) * n_c, :].reshape(samples, d_s5).T
    for g in range(groups):
        a = xt_ref[:, g * h_c:(g + 1) * h_c, :].reshape(t_c * h_c, samples).astype(BF16)
        yt_ref[g] = _dot(m_ref[g], a)
        z_ref[g] = _dot(e_ref[g], a)

    c_idx = lax.broadcasted_iota(jnp.int32, (p_c, LANES), 1) % n_c
    for g in range(groups):
        z_re, z_im = z_ref[g, 0:p_c], z_ref[g, p_c:2 * p_c]
        v_re = jnp.where(c_idx == 0, carry_ref[g, 0:p_c], pltpu.roll(z_re, 1, 1))
        v_im = jnp.where(c_idx == 0, carry_ref[g, p_c:2 * p_c], pltpu.roll(z_im, 1, 1))
        for k in range(3):
            hop = 1 << k
            sh_re = jnp.where(c_idx >= hop, pltpu.roll(v_re, hop, 1), 0.0)
            sh_im = jnp.where(c_idx >= hop, pltpu.roll(v_im, hop, 1), 0.0)
            d_re, d_im = _cmul(apow_ref[g, k, 0], apow_ref[g, k, 1], sh_re, sh_im)
            v_re, v_im = v_re + d_re, v_im + d_im
        s_ref[g, 0:p_c] = v_re.astype(BF16)
        s_ref[g, p_c:2 * p_c] = v_im.astype(BF16)
        o_re, o_im = _cmul(apow_ref[g, 0, 0], apow_ref[g, 0, 1], v_re, v_im)
        carry_ref[g, 0:p_c] = pltpu.roll(o_re + z_re, LANES - (n_c - 1), 1)
        carry_ref[g, p_c:2 * p_c] = pltpu.roll(o_im + z_im, LANES - (n_c - 1), 1)

    for g in range(groups):
        yt_ref[g] += _dot(f_ref[g], s_ref[g])
    for t in range(t_c):
        y2_ref[t] = yt_ref[:, t * h_c:(t + 1) * h_c, :].reshape(d_s5, samples).T
    for b in range(n_b):
        y = y2_ref[:, b * n_c:(b + 1) * n_c, :].reshape(t_blk, d_s5)
        y_hi = y.astype(BF16)
        y_lo = (y - y_hi.astype(F32)).astype(BF16)
        y2 = _dot(perm_ref[1], jnp.concatenate([y_hi, y_lo], axis=-1))
        ub_ref[b] = y2[:, :d_s5] + y2[:, d_s5:] + d_ref[...] * u_ref[b]
    fin = S5_FINISH_ROWS // t_blk
    for q in range(n_b // fin):
        y = ub_ref[q * fin:(q + 1) * fin].reshape(S5_FINISH_ROWS, d_s5)
        y = jax.nn.gelu(y)
        y = y * jax.nn.sigmoid(_dot(y.astype(BF16), wglu_ref[...]) + bglu_ref[...])
        y_ref[q * fin:(q + 1) * fin] = _rms(y, onorm_ref[...]).reshape(fin, t_blk, d_s5)


def _s5_mixer(u, m, e, f, apow, d_skip, wglu, bglu, onorm):
    batch, seq, d_s5 = u.shape
    groups, w = m.shape[0], m.shape[1]
    states = f.shape[2] // 2
    t_blk = S5_CHUNK * S5_CHUNKS
    n_b = LANES // S5_CHUNKS
    assert batch % n_b == 0 and seq % t_blk == 0
    block = pl.BlockSpec((n_b, t_blk, d_s5), lambda bb, t: (bb, t, 0))
    params, param_specs = zip(*map(_resident, (m, e, f, apow, d_skip, wglu, bglu, onorm)))
    return pl.pallas_call(
        _s5_kernel,
        grid=(batch // n_b, seq // t_blk),
        in_specs=[block, *param_specs],
        out_specs=block,
        out_shape=jax.ShapeDtypeStruct((batch, seq, d_s5), F32),
        scratch_shapes=[pltpu.VMEM((n_b, t_blk, d_s5), F32),
                        pltpu.VMEM((S5_CHUNK, d_s5, LANES), F32),
                        pltpu.VMEM((groups, w, LANES), F32),
                        pltpu.VMEM((S5_CHUNK, LANES, d_s5), F32),
                        pltpu.VMEM((groups, 2 * states, LANES), F32),
                        pltpu.VMEM((groups, 2 * states, LANES), BF16),
                        pltpu.VMEM((groups, 2 * states, LANES), F32),
                        pltpu.VMEM((2, t_blk, t_blk), BF16)],
        compiler_params=pltpu.CompilerParams(dimension_semantics=("arbitrary", "arbitrary"),
                                             vmem_limit_bytes=VMEM_LIMIT_BYTES),
        name="s5_mixer",
    )(u, *params)


def _log_sigmoid(x):
    return jnp.minimum(x, 0.0) - jnp.log1p(jnp.exp(-jnp.abs(x)))


def _gla_kernel(zg_ref, la_ref, onorm_ref, y_ref, st_ref):
    rows = zg_ref.shape[0]
    d_gla = y_ref.shape[1]
    dv = d_gla // GLA_HEADS
    dk = dv // 2
    kdim = GLA_HEADS * dk
    o_k, o_v, o_g = kdim, 2 * kdim, 2 * kdim + d_gla
    c = GLA_CHUNK
    assert 2 * dk == LANES and dv == LANES

    @pl.when(pl.program_id(1) == 0)
    def _():
        st_ref[...] = jnp.zeros_like(st_ref)

    r_i = lax.broadcasted_iota(jnp.int32, (c, 3 * c), 0)
    c_i = lax.broadcasted_iota(jnp.int32, (c, 3 * c), 1)
    tri3 = jnp.where((c_i % c) <= r_i, 1.0, 0.0).astype(BF16)
    causal = (lax.broadcasted_iota(jnp.int32, (c, LANES), 0)
              >= lax.broadcasted_iota(jnp.int32, (c, LANES), 1) % c)
    low_head = lax.broadcasted_iota(jnp.int32, (c, LANES), 1) < dk
    own_block = ((lax.broadcasted_iota(jnp.int32, (2 * dv, LANES), 0) < dv)
                 == (lax.broadcasted_iota(jnp.int32, (2 * dv, LANES), 1) < dk))
    n_chunks = rows // c
    pairs = GLA_HEADS // 2
    rows_of = lambda t, n: t[n * c:(n + 1) * c]
    zeros_v = jnp.zeros((c, dv), BF16)

    la = la_ref[...]
    la_hi = la.astype(BF16)
    rem = la - la_hi.astype(F32)
    la_mid = rem.astype(BF16)
    la_lo = (rem - la_mid.astype(F32)).astype(BF16)
    cums = [_dot(tri3, jnp.concatenate([rows_of(la_hi, n), rows_of(la_mid, n), rows_of(la_lo, n)], axis=0))
            for n in range(n_chunks)]
    cum = jnp.concatenate(cums, axis=0)
    totals = [t[c - 1:c, :] for t in cums]
    total = jnp.concatenate([jnp.broadcast_to(t, (c, kdim)) for t in totals], axis=0)
    q_dec = (zg_ref[:, 0:kdim].astype(F32) * (dk ** -0.5) * jnp.exp(cum)).astype(BF16)
    k = zg_ref[:, o_k:o_k + kdim].astype(F32)
    k_inv = k * jnp.exp(-cum)
    k_end = (k * jnp.exp(total - cum)).astype(BF16)
    v = zg_ref[:, o_v:o_v + d_gla]

    def lanes_of(t, n, pair):
        return t[n * c:(n + 1) * c, pair * LANES:(pair + 1) * LANES]

    scores = []
    for n in range(n_chunks):
        for pair in range(pairs):
            kp = lanes_of(k_inv, n, pair)
            keys = jnp.concatenate([jnp.where(low_head, kp, 0.0), jnp.where(low_head, 0.0, kp)], axis=0)
            s = _dot_nt(lanes_of(q_dec, n, pair), keys.astype(BF16))
            scores.append(jnp.where(causal, s, 0.0).astype(BF16))

    intra, upds = [], []
    for n in range(n_chunks):
        for pair in range(pairs):
            v_even = v[n * c:(n + 1) * c, (2 * pair) * dv:(2 * pair + 1) * dv]
            v_odd = v[n * c:(n + 1) * c, (2 * pair + 1) * dv:(2 * pair + 2) * dv]
            v_diag = jnp.concatenate([jnp.concatenate([v_even, zeros_v], axis=1),
                                      jnp.concatenate([zeros_v, v_odd], axis=1)], axis=0)
            intra.append(_dot(scores[n * pairs + pair], v_diag))
            upd = _dot_tn(jnp.concatenate([v_even, v_odd], axis=1), lanes_of(k_end, n, pair))
            upds.append(jnp.where(own_block, upd, 0.0))

    outs = [[None] * pairs for _ in range(n_chunks)]
    for pair in range(pairs):
        st = st_ref[pair]
        for n in range(n_chunks):
            outs[n][pair] = intra[n * pairs + pair] + _dot_nt(lanes_of(q_dec, n, pair), st.astype(BF16))
            decay = jnp.exp(totals[n][:, pair * LANES:(pair + 1) * LANES])
            st = st * decay + upds[n * pairs + pair]
        st_ref[pair] = st

    for h in range(GLA_HEADS):
        o = jnp.concatenate([outs[n][h // 2][:, (h % 2) * dv:(h % 2 + 1) * dv] for n in range(n_chunks)], axis=0)
        o = o * lax.rsqrt(jnp.mean(o * o, axis=-1, keepdims=True) + EPS)
        g = zg_ref[:, o_g + h * dv:o_g + (h + 1) * dv].astype(F32)
        y = jax.nn.silu(g) * (o * onorm_ref[:, h * dv:(h + 1) * dv])
        y_ref[:, h * dv:(h + 1) * dv] = y.astype(y_ref.dtype)


def _gla_mixer(zg, la, onorm, *, batch, d_gla):
    rows_total = zg.shape[0]
    seq = rows_total // batch
    rows = min(GLA_ROWS, seq)
    n_t = seq // rows
    token_rows = lambda w: pl.BlockSpec((rows, w), lambda b, t: (b * n_t + t, 0))
    onorm, onorm_spec = _resident(onorm)
    return pl.pallas_call(
        _gla_kernel,
        grid=(batch, n_t),
        in_specs=[token_rows(zg.shape[1]), token_rows(la.shape[1]), onorm_spec],
        out_specs=token_rows(d_gla),
        out_shape=jax.ShapeDtypeStruct((rows_total, d_gla), BF16),
        scratch_shapes=[pltpu.VMEM((GLA_HEADS // 2, 2 * d_gla // GLA_HEADS, LANES), F32)],
        compiler_params=pltpu.CompilerParams(dimension_semantics=("arbitrary", "arbitrary"),
                                             vmem_limit_bytes=VMEM_LIMIT_BYTES),
        name="gla_mixer",
    )(zg, la, onorm)


def _dense_kernel(*refs, mix, proj, d_s5, d_ff):
    refs = list(refs)
    x_ref = refs.pop(0)
    sub = x_ref.shape[0] // DENSE_SPLIT
    groups = [pl.ds(s * sub, sub) for s in range(DENSE_SPLIT)]
    xs = [x_ref[r, :] for r in groups]
    if mix:
        ys5_ref, ygla_ref, wout_ref, nffn_ref, wfi_ref, wfo_ref = refs[:6]
        refs = refs[6:]
        xs = [x + _dot(ys5_ref[r, :].astype(BF16), wout_ref[0:d_s5, :]) + _dot(ygla_ref[r, :], wout_ref[d_s5:, :])
              for x, r in zip(xs, groups)]
        hs = [_rms(x, nffn_ref[...]).astype(BF16) for x in xs]
        tiles = d_ff // MXU_DIM
        bounds = [MXU_DIM * ((tiles * c + FFN_CHUNKS - 1) // FFN_CHUNKS) for c in range(FFN_CHUNKS + 1)]
        acts = [[] for _ in range(DENSE_SPLIT)]
        for lo, hi in zip(bounds[:-1], bounds[1:]):
            for s in range(DENSE_SPLIT):
                gate = _dot(hs[s], wfi_ref[:, lo:hi])
                up = _dot(hs[s], wfi_ref[:, d_ff + lo:d_ff + hi])
                acts[s].append((jax.nn.silu(gate) * up).astype(BF16))
        xs = [x + _dot(jnp.concatenate(a, axis=1), wfo_ref[...]) for x, a in zip(xs, acts)]
    norm_ref = refs.pop(0)
    hs = [_rms(x, norm_ref[...]) for x in xs]
    if not proj:
        (out_ref,) = refs
        for h, r in zip(hs, groups):
            out_ref[r, :] = h
        return
    win_ref, weff_ref, bgate_ref = refs[:3]
    u_ref, zg_ref, la_ref = refs[-3:]
    for x, h, r in zip(xs, hs, groups):
        if mix:
            refs[3][r, :] = x
        hb = h.astype(BF16)
        z = _dot(hb, win_ref[...])
        u_ref[r, :] = z[:, :d_s5]
        zg_ref[r, :] = z[:, d_s5:].astype(BF16)
        la_ref[r, :] = _log_sigmoid(_dot(hb, weff_ref[...]) + bgate_ref[...]) * (1.0 / GLA_GATE_NORM)


def _dense(x, mix_args, norm, proj_args, *, d_s5):
    rows_total, d_model = x.shape
    tm = DENSE_ROWS
    token_rows = lambda w: pl.BlockSpec((tm, w), lambda r: (r, 0))
    mix, proj = mix_args is not None, proj_args is not None
    args, in_specs = [x], [token_rows(d_model)]
    d_ff = 0
    if mix:
        y_s5, y_gla, w_out, norm_ffn, w_fi, w_fo = mix_args
        d_ff = w_fo.shape[0]
        assert d_ff % MXU_DIM == 0
        args += [y_s5, y_gla]
        in_specs += [token_rows(d_s5), token_rows(y_gla.shape[1])]
        resident = [w_out, norm_ffn, w_fi, w_fo, norm]
    else:
        resident = [norm]
    if proj:
        w_in, w_eff, b_gate = proj_args
        la_w = b_gate.shape[1]
        zg_w = w_in.shape[1] - d_s5
        resident += [w_in, w_eff, b_gate]
        out_specs = [token_rows(d_model), token_rows(d_s5), token_rows(zg_w), token_rows(la_w)]
        out_shape = [jax.ShapeDtypeStruct((rows_total, d_model), F32),
                     jax.ShapeDtypeStruct((rows_total, d_s5), F32),
                     jax.ShapeDtypeStruct((rows_total, zg_w), BF16),
                     jax.ShapeDtypeStruct((rows_total, la_w), F32)]
        if not mix:
            out_specs, out_shape = out_specs[1:], out_shape[1:]
    else:
        out_specs = token_rows(d_model)
        out_shape = jax.ShapeDtypeStruct((rows_total, d_model), F32)
    for operand, spec in map(_resident, resident):
        args.append(operand)
        in_specs.append(spec)
    return pl.pallas_call(
        functools.partial(_dense_kernel, mix=mix, proj=proj, d_s5=d_s5, d_ff=d_ff),
        grid=(rows_total // tm,),
        in_specs=in_specs,
        out_specs=out_specs,
        out_shape=out_shape,
        compiler_params=pltpu.CompilerParams(dimension_semantics=("arbitrary",),
                                             vmem_limit_bytes=VMEM_LIMIT_BYTES),
        name="dense_" + ("mix" if mix else "") + ("proj" if proj else "final"),
    )(*args)


def kernel(x, norm_mix, w_in, s5_lam_re, s5_lam_im, s5_b_re, s5_b_im, s5_c_re, s5_c_im, s5_d, s5_log_step,
           s5_w_glu, s5_b_glu, s5_out_norm, gla_w_gate, gla_b_gate, gla_out_norm, w_out, norm_ffn, w_ffn_in,
           w_ffn_out, norm_final):
    batch, seq, d_model = x.shape
    depth = w_in.shape[0]
    d_s5 = s5_d.shape[1]
    d_gla = gla_out_norm.shape[1]
    kdim = gla_w_gate.shape[2]
    o_gate = d_s5 + 2 * kdim + 2 * d_gla
    assert (batch * seq) % DENSE_ROWS == 0 and seq % GLA_CHUNK == 0 and seq % (S5_CHUNK * S5_CHUNKS) == 0

    s5_m, s5_e, s5_f_re, s5_f_im, s5_apow = _s5_prep(s5_lam_re, s5_lam_im, s5_log_step, s5_b_re, s5_b_im,
                                                     s5_c_re, s5_c_im)
    s5_f = jnp.concatenate([s5_f_re, s5_f_im], axis=-1)
    w_eff = _gate_fold(w_in[:, :, o_gate:], gla_w_gate)
    w_main_b = w_in[:, :, :o_gate].astype(BF16)
    w_out_b, w_fi_b, w_fo_b, w_glu_b = (w.astype(BF16) for w in (w_out, w_ffn_in, w_ffn_out, s5_w_glu))
    layer = _Layer
    vec = lambda v, i: _Layer(v[:, None, :], i)
    proj = lambda i: (layer(w_main_b, i), layer(w_eff, i), vec(gla_b_gate, i))

    xf = x.reshape(batch * seq, d_model)
    u, zg, la = _dense(xf, None, vec(norm_mix, 0), proj(0), d_s5=d_s5)
    for i in range(depth):
        y_s5 = _s5_mixer(u.reshape(batch, seq, d_s5), layer(s5_m, i), layer(s5_e, i), layer(s5_f, i),
                         layer(s5_apow, i), vec(s5_d, i), layer(w_glu_b, i), vec(s5_b_glu, i), vec(s5_out_norm, i))
        y_gla = _gla_mixer(zg, la, vec(gla_out_norm, i), batch=batch, d_gla=d_gla)
        mix_args = (y_s5.reshape(batch * seq, d_s5), y_gla, layer(w_out_b, i), vec(norm_ffn, i),
                    layer(w_fi_b, i), layer(w_fo_b, i))
        if i + 1 < depth:
            xf, u, zg, la = _dense(xf, mix_args, vec(norm_mix, i + 1), proj(i + 1), d_s5=d_s5)
        else:
            xf = _dense(xf, mix_args, norm_final[None, :], None, d_s5=d_s5)
    return xf.reshape(batch, seq, d_model)
```

```python
import functools
from typing import NamedTuple

import jax
import jax.numpy as jnp
from jax import lax
from jax.experimental import pallas as pl
from jax.experimental.pallas import tpu as pltpu

S5_GROUP = 16
GLA_HEADS = 4
GLA_CHUNK = 64
GLA_GATE_NORM = 16.0
EPS = 1e-6

LANES = 128
MXU_DIM = 256
VMEM_LIMIT_BYTES = 56 * 1024 * 1024

DENSE_ROWS = 512
DENSE_PROJ_ROWS = 2048
DENSE_GROUP = 256
FFN_CHUNKS = 2
S5_CHUNK = 16
S5_CHUNKS = 8
S5_FINISH_ROWS = 512
S5_PREP_GROUPS = 8
GLA_ROWS = 2048

BF16 = jnp.bfloat16
F32 = jnp.float32


def _dot(a, b):
    return jnp.dot(a, b, preferred_element_type=F32)


def _dot_nt(a, b):
    return lax.dot_general(a, b, (((1,), (1,)), ((), ())), preferred_element_type=F32)


def _dot_tn(a, b):
    return lax.dot_general(a, b, (((0,), (0,)), ((), ())), preferred_element_type=F32)


def _rms(x, gain):
    return x * lax.rsqrt(jnp.mean(x * x, axis=-1, keepdims=True) + EPS) * gain


class _Layer(NamedTuple):
    stacked: jax.Array
    index: int

    @property
    def shape(self):
        return self.stacked.shape[1:]


def _resident(x):
    if isinstance(x, _Layer):
        lead, shape, operand = (x.index,), x.shape, x.stacked
        block = (None,) + shape
    else:
        lead, shape, operand = (), x.shape, x
        block = shape
    return operand, pl.BlockSpec(block, lambda *_: lead + (0,) * len(shape), pipeline_mode=pl.Buffered(1))


def _dot_split(a, b):
    a_hi, b_hi = a.astype(BF16), b.astype(BF16)
    a_lo = (a - a_hi.astype(F32)).astype(BF16)
    b_lo = (b - b_hi.astype(F32)).astype(BF16)
    return _dot(a_hi, b_hi) + _dot(a_hi, b_lo) + _dot(a_lo, b_hi)


def _gate_fold_kernel(win_gate_ref, w_gate_ref, weff_ref):
    weff_ref[0] = _dot_split(win_gate_ref[0], w_gate_ref[0]).astype(BF16)


def _gate_fold(win_gate, w_gate):
    depth, d_model, rank = win_gate.shape
    kdim = w_gate.shape[2]
    per_layer = lambda *s: pl.BlockSpec((1,) + s, lambda i: (i,) + (0,) * len(s))
    return pl.pallas_call(
        _gate_fold_kernel,
        grid=(depth,),
        in_specs=[per_layer(d_model, rank), per_layer(rank, kdim)],
        out_specs=per_layer(d_model, kdim),
        out_shape=jax.ShapeDtypeStruct((depth, d_model, kdim), BF16),
        compiler_params=pltpu.CompilerParams(dimension_semantics=("arbitrary",)),
        name="gate_fold",
    )(win_gate, w_gate)


def _cmul(ar, ai, br, bi):
    return ar * br - ai * bi, ar * bi + ai * br


def _discretise(lr, li, log_step):
    step = jnp.exp(log_step)
    mag = jnp.exp(lr * step)
    ar = mag * jnp.cos(li * step)
    ai = mag * jnp.sin(li * step)
    den = lr * lr + li * li
    fr = ((ar - 1.0) * lr + ai * li) / den
    fi = (ai * lr - (ar - 1.0) * li) / den
    return ar, ai, fr, fi


def _s5_prep_kernel(lam_re_row_ref, lam_im_row_ref, lam_re_col_ref, lam_im_col_ref, log_step_ref,
                    c_re_ref, c_im_ref, b_re_ref, b_im_ref, m_ref, e_ref, f_re_ref, f_im_ref, apow_ref):
    t_c = S5_CHUNK
    h_c = S5_GROUP
    p_c = lam_re_row_ref.shape[-1]

    def one_group(gi, _):
        log_step = log_step_ref[0, gi]
        ar, ai, _, _ = _discretise(lam_re_row_ref[0, gi], lam_im_row_ref[0, gi], log_step)
        arc, aic, frc, fic = _discretise(lam_re_col_ref[0, gi], lam_im_col_ref[0, gi], log_step)

        pr = jnp.ones((h_c, p_c), F32)
        pi = jnp.zeros((h_c, p_c), F32)
        blocks_r, blocks_i = [], []
        for _ in range(t_c):
            blocks_r.append(pr)
            blocks_i.append(pi)
            pr, pi = _cmul(pr, pi, ar, ai)
        l_re, l_im = _cmul(c_re_ref[0, gi], c_im_ref[0, gi], jnp.concatenate(blocks_r, axis=0),
                           jnp.concatenate(blocks_i, axis=0))
        f_re, f_im = _cmul(l_re, l_im, ar, ai)
        f_re_ref[0, gi] = f_re.astype(BF16)
        f_im_ref[0, gi] = (-f_im).astype(BF16)

        bw_re, bw_im = _cmul(frc, fic, b_re_ref[0, gi], b_im_ref[0, gi])
        k_wide = _dot_split(l_re, bw_re) - _dot_split(l_im, bw_im)
        per_tile = LANES // h_c
        lane_blk = lax.broadcasted_iota(jnp.int32, (t_c * h_c, LANES), 1) // h_c
        for lt in range(t_c // per_tile):
            k_tile = k_wide[:, lt * LANES:(lt + 1) * LANES]
            m = jnp.zeros((t_c * h_c, LANES), F32)
            for jj in range(per_tile):
                j = lt * per_tile + jj
                moved = k_tile if j == 0 else jnp.concatenate(
                    [jnp.zeros((j * h_c, LANES), F32), k_tile[:(t_c - j) * h_c]], axis=0)
                m = jnp.where(lane_blk == jj, moved, m)
            m_ref[0, gi, :, lt * LANES:(lt + 1) * LANES] = m.astype(BF16)

        lane_blk = lax.broadcasted_iota(jnp.int32, (p_c, LANES), 1) // h_c
        qr = jnp.ones((p_c, 1), F32)
        qi = jnp.zeros((p_c, 1), F32)
        for lt in range(t_c // per_tile - 1, -1, -1):
            pe_r = jnp.zeros((p_c, LANES), F32)
            pe_i = jnp.zeros((p_c, LANES), F32)
            for jj in range(per_tile - 1, -1, -1):
                pe_r = jnp.where(lane_blk == jj, qr, pe_r)
                pe_i = jnp.where(lane_blk == jj, qi, pe_i)
                qr, qi = _cmul(qr, qi, arc, aic)
            cols = slice(lt * LANES, (lt + 1) * LANES)
            e_re, e_im = _cmul(pe_r, pe_i, bw_re[:, cols], bw_im[:, cols])
            e_ref[0, gi, 0:p_c, cols] = e_re.astype(BF16)
            e_ref[0, gi, p_c:2 * p_c, cols] = e_im.astype(BF16)

        for k in range(3):
            apow_ref[0, gi, k, 0] = jnp.broadcast_to(qr, (p_c, LANES))
            apow_ref[0, gi, k, 1] = jnp.broadcast_to(qi, (p_c, LANES))
            qr, qi = _cmul(qr, qi, qr, qi)
        return 0

    lax.fori_loop(0, m_ref.shape[1], one_group, 0)


def _s5_prep(lam_re, lam_im, log_step, b_re, b_im, c_re, c_im):
    depth, groups, states = lam_re.shape
    t_c, h_c = S5_CHUNK, S5_GROUP
    row = lambda t: t.reshape(depth, groups, 1, states)
    col = lambda t: t.reshape(depth, groups, states, 1)
    c_rows = lambda t: jnp.tile(t, (1, 1, t_c, 1))
    b_lanes = lambda t: jnp.tile(t, (1, 1, 1, t_c))
    blk = lambda *s: pl.BlockSpec((1, S5_PREP_GROUPS) + s, lambda i, g: (i, g) + (0,) * len(s))
    w = t_c * h_c
    return pl.pallas_call(
        _s5_prep_kernel,
        grid=(depth, groups // S5_PREP_GROUPS),
        in_specs=[blk(1, states)] * 2 + [blk(states, 1)] * 2 + [blk(1, 1)]
        + [blk(w, states)] * 2 + [blk(states, w)] * 2,
        out_specs=[blk(w, w), blk(2 * states, w), blk(w, states), blk(w, states), blk(3, 2, states, LANES)],
        out_shape=[jax.ShapeDtypeStruct((depth, groups, w, w), BF16),
                   jax.ShapeDtypeStruct((depth, groups, 2 * states, w), BF16),
                   jax.ShapeDtypeStruct((depth, groups, w, states), BF16),
                   jax.ShapeDtypeStruct((depth, groups, w, states), BF16),
                   jax.ShapeDtypeStruct((depth, groups, 3, 2, states, LANES), F32)],
        compiler_params=pltpu.CompilerParams(dimension_semantics=("arbitrary", "arbitrary")),
        name="s5_prep",
    )(row(lam_re), row(lam_im), col(lam_re), col(lam_im), log_step.reshape(depth, groups, 1, 1),
      c_rows(c_re), c_rows(c_im), b_lanes(b_re), b_lanes(b_im))


def _s5_kernel(u_ref, m_ref, e_ref, f_ref, apow_ref, d_ref, wglu_ref, bglu_ref, onorm_ref, y_ref,
               ub_ref, xt_ref, yt_ref, y2_ref, z_ref, s_ref, carry_ref, perm_ref):
    n_b, t_blk, d_s5 = u_ref.shape
    groups = m_ref.shape[0]
    t_c, h_c = S5_CHUNK, S5_GROUP
    n_c = t_blk // t_c
    p_c = f_ref.shape[2] // 2
    samples = n_b * n_c
    assert samples == LANES and n_c == 8

    @pl.when(pl.program_id(1) == 0)
    def _():
        carry_ref[...] = jnp.zeros_like(carry_ref)

    @pl.when((pl.program_id(0) == 0) & (pl.program_id(1) == 0))
    def _():
        dst = lax.broadcasted_iota(jnp.int32, (t_blk, t_blk), 0)
        src = lax.broadcasted_iota(jnp.int32, (t_blk, t_blk), 1)
        perm_ref[0] = jnp.where(src == (dst % n_c) * t_c + dst // n_c, 1.0, 0.0).astype(BF16)
        perm_ref[1] = jnp.where(src == (dst % t_c) * n_c + dst // t_c, 1.0, 0.0).astype(BF16)

    for b in range(n_b):
        ub_ref[b] = _dot(perm_ref[0], u_ref[b].astype(BF16))
    for t in range(t_c):
        xt_ref[t] = ub_ref[:, t * n_c:(t + 1) * n_c, :].reshape(samples, d_s5).T
    for g in range(groups):
        a = xt_ref[:, g * h_c:(g + 1) * h_c, :].reshape(t_c * h_c, samples).astype(BF16)
        yt_ref[g] = _dot(m_ref[g], a)
        z_ref[g] = _dot(e_ref[g], a)

    c_idx = lax.broadcasted_iota(jnp.int32, (p_c, LANES), 1) % n_c
    for g in range(groups):
        z_re, z_im = z_ref[g, 0:p_c], z_ref[g, p_c:2 * p_c]
        v_re = jnp.where(c_idx == 0, carry_ref[g, 0:p_c], pltpu.roll(z_re, 1, 1))
        v_im = jnp.where(c_idx == 0, carry_ref[g, p_c:2 * p_c], pltpu.roll(z_im, 1, 1))
        for k in range(3):
            hop = 1 << k
            sh_re = jnp.where(c_idx >= hop, pltpu.roll(v_re, hop, 1), 0.0)
            sh_im = jnp.where(c_idx >= hop, pltpu.roll(v_im, hop, 1), 0.0)
            d_re, d_im = _cmul(apow_ref[g, k, 0], apow_ref[g, k, 1], sh_re, sh_im)
            v_re, v_im = v_re + d_re, v_im + d_im
        s_ref[g, 0:p_c] = v_re.astype(BF16)
        s_ref[g, p_c:2 * p_c] = v_im.astype(BF16)
        o_re, o_im = _cmul(apow_ref[g, 0, 0], apow_ref[g, 0, 1], v_re, v_im)
        carry_ref[g, 0:p_c] = pltpu.roll(o_re + z_re, LANES - (n_c - 1), 1)
        carry_ref[g, p_c:2 * p_c] = pltpu.roll(o_im + z_im, LANES - (n_c - 1), 1)

    for g in range(groups):
        yt_ref[g] += _dot(f_ref[g], s_ref[g])
    for t in range(t_c):
        y2_ref[t] = yt_ref[:, t * h_c:(t + 1) * h_c, :].reshape(d_s5, samples).T
    for b in range(n_b):
        y = y2_ref[:, b * n_c:(b + 1) * n_c, :].reshape(t_blk, d_s5)
        y_hi = y.astype(BF16)
        y_lo = (y - y_hi.astype(F32)).astype(BF16)
        y2 = _dot(perm_ref[1], jnp.concatenate([y_hi, y_lo], axis=-1))
        ub_ref[b] = y2[:, :d_s5] + y2[:, d_s5:] + d_ref[...] * u_ref[b]
    fin = S5_FINISH_ROWS // t_blk
    for q in range(n_b // fin):
        y = ub_ref[q * fin:(q + 1) * fin].reshape(S5_FINISH_ROWS, d_s5)
        y = jax.nn.gelu(y)
        y = y * jax.nn.sigmoid(_dot(y.astype(BF16), wglu_ref[...]) + bglu_ref[...])
        y_ref[q * fin:(q + 1) * fin] = _rms(y, onorm_ref[...]).reshape(fin, t_blk, d_s5)


def _s5_mixer(u, m, e, f, apow, d_skip, wglu, bglu, onorm):
    batch, seq, d_s5 = u.shape
    groups, w = m.shape[0], m.shape[1]
    states = f.shape[2] // 2
    t_blk = S5_CHUNK * S5_CHUNKS
    n_b = LANES // S5_CHUNKS
    assert batch % n_b == 0 and seq % t_blk == 0
    block = pl.BlockSpec((n_b, t_blk, d_s5), lambda bb, t: (bb, t, 0))
    params, param_specs = zip(*map(_resident, (m, e, f, apow, d_skip, wglu, bglu, onorm)))
    return pl.pallas_call(
        _s5_kernel,
        grid=(batch // n_b, seq // t_blk),
        in_specs=[block, *param_specs],
        out_specs=block,
        out_shape=jax.ShapeDtypeStruct((batch, seq, d_s5), F32),
        scratch_shapes=[pltpu.VMEM((n_b, t_blk, d_s5), F32),
                        pltpu.VMEM((S5_CHUNK, d_s5, LANES), F32),
                        pltpu.VMEM((groups, w, LANES), F32),
                        pltpu.VMEM((S5_CHUNK, LANES, d_s5), F32),
                        pltpu.VMEM((groups, 2 * states, LANES), F32),
                        pltpu.VMEM((groups, 2 * states, LANES), BF16),
                        pltpu.VMEM((groups, 2 * states, LANES), F32),
                        pltpu.VMEM((2, t_blk, t_blk), BF16)],
        compiler_params=pltpu.CompilerParams(dimension_semantics=("arbitrary", "arbitrary"),
                                             vmem_limit_bytes=VMEM_LIMIT_BYTES),
        name="s5_mixer",
    )(u, *params)


def _log_sigmoid(x):
    return jnp.minimum(x, 0.0) - jnp.log1p(jnp.exp(-jnp.abs(x)))


def _gla_kernel(zg_ref, la_ref, onorm_ref, y_ref, st_ref):
    rows = zg_ref.shape[0]
    d_gla = y_ref.shape[1]
    dv = d_gla // GLA_HEADS
    dk = dv // 2
    kdim = GLA_HEADS * dk
    o_k, o_v, o_g = kdim, 2 * kdim, 2 * kdim + d_gla
    c = GLA_CHUNK
    assert 2 * dk == LANES and dv == LANES

    @pl.when(pl.program_id(1) == 0)
    def _():
        st_ref[...] = jnp.zeros_like(st_ref)

    r_i = lax.broadcasted_iota(jnp.int32, (c, 3 * c), 0)
    c_i = lax.broadcasted_iota(jnp.int32, (c, 3 * c), 1)
    tri3 = jnp.where((c_i % c) <= r_i, 1.0, 0.0).astype(BF16)
    causal = (lax.broadcasted_iota(jnp.int32, (c, LANES), 0)
              >= lax.broadcasted_iota(jnp.int32, (c, LANES), 1) % c)
    low_head = lax.broadcasted_iota(jnp.int32, (c, LANES), 1) < dk
    own_block = ((lax.broadcasted_iota(jnp.int32, (2 * dv, LANES), 0) < dv)
                 == (lax.broadcasted_iota(jnp.int32, (2 * dv, LANES), 1) < dk))
    n_chunks = rows // c
    pairs = GLA_HEADS // 2
    rows_of = lambda t, n: t[n * c:(n + 1) * c]
    zeros_v = jnp.zeros((c, dv), BF16)

    la = la_ref[...]
    la_hi = la.astype(BF16)
    rem = la - la_hi.astype(F32)
    la_mid = rem.astype(BF16)
    la_lo = (rem - la_mid.astype(F32)).astype(BF16)
    cums = [_dot(tri3, jnp.concatenate([rows_of(la_hi, n), rows_of(la_mid, n), rows_of(la_lo, n)], axis=0))
            for n in range(n_chunks)]
    cum = jnp.concatenate(cums, axis=0)
    totals = [t[c - 1:c, :] for t in cums]
    total = jnp.concatenate([jnp.broadcast_to(t, (c, kdim)) for t in totals], axis=0)
    q_dec = (zg_ref[:, 0:kdim].astype(F32) * (dk ** -0.5) * jnp.exp(cum)).astype(BF16)
    k = zg_ref[:, o_k:o_k + kdim].astype(F32)
    k_inv = k * jnp.exp(-cum)
    k_end = (k * jnp.exp(total - cum)).astype(BF16)
    v = zg_ref[:, o_v:o_v + d_gla]

    def lanes_of(t, n, pair):
        return t[n * c:(n + 1) * c, pair * LANES:(pair + 1) * LANES]

    scores = []
    for n in range(n_chunks):
        for pair in range(pairs):
            kp = lanes_of(k_inv, n, pair)
            keys = jnp.concatenate([jnp.where(low_head, kp, 0.0), jnp.where(low_head, 0.0, kp)], axis=0)
            s = _dot_nt(lanes_of(q_dec, n, pair), keys.astype(BF16))
            scores.append(jnp.where(causal, s, 0.0).astype(BF16))

    intra, upds = [], []
    for n in range(n_chunks):
        for pair in range(pairs):
            v_even = v[n * c:(n + 1) * c, (2 * pair) * dv:(2 * pair + 1) * dv]
            v_odd = v[n * c:(n + 1) * c, (2 * pair + 1) * dv:(2 * pair + 2) * dv]
            v_diag = jnp.concatenate([jnp.concatenate([v_even, zeros_v], axis=1),
                                      jnp.concatenate([zeros_v, v_odd], axis=1)], axis=0)
            intra.append(_dot(scores[n * pairs + pair], v_diag))
            upd = _dot_tn(jnp.concatenate([v_even, v_odd], axis=1), lanes_of(k_end, n, pair))
            upds.append(jnp.where(own_block, upd, 0.0))

    outs = [[None] * pairs for _ in range(n_chunks)]
    for pair in range(pairs):
        st = st_ref[pair]
        for n in range(n_chunks):
            outs[n][pair] = intra[n * pairs + pair] + _dot_nt(lanes_of(q_dec, n, pair), st.astype(BF16))
            decay = jnp.exp(totals[n][:, pair * LANES:(pair + 1) * LANES])
            st = st * decay + upds[n * pairs + pair]
        st_ref[pair] = st

    for h in range(GLA_HEADS):
        o = jnp.concatenate([outs[n][h // 2][:, (h % 2) * dv:(h % 2 + 1) * dv] for n in range(n_chunks)], axis=0)
        o = o * lax.rsqrt(jnp.mean(o * o, axis=-1, keepdims=True) + EPS)
        g = zg_ref[:, o_g + h * dv:o_g + (h + 1) * dv].astype(F32)
        y = jax.nn.silu(g) * (o * onorm_ref[:, h * dv:(h + 1) * dv])
        y_ref[:, h * dv:(h + 1) * dv] = y.astype(y_ref.dtype)


def _gla_mixer(zg, la, onorm, *, batch, d_gla):
    rows_total = zg.shape[0]
    seq = rows_total // batch
    rows = min(GLA_ROWS, seq)
    n_t = seq // rows
    token_rows = lambda w: pl.BlockSpec((rows, w), lambda b, t: (b * n_t + t, 0))
    onorm, onorm_spec = _resident(onorm)
    return pl.pallas_call(
        _gla_kernel,
        grid=(batch, n_t),
        in_specs=[token_rows(zg.shape[1]), token_rows(la.shape[1]), onorm_spec],
        out_specs=token_rows(d_gla),
        out_shape=jax.ShapeDtypeStruct((rows_total, d_gla), BF16),
        scratch_shapes=[pltpu.VMEM((GLA_HEADS // 2, 2 * d_gla // GLA_HEADS, LANES), F32)],
        compiler_params=pltpu.CompilerParams(dimension_semantics=("arbitrary", "arbitrary"),
                                             vmem_limit_bytes=VMEM_LIMIT_BYTES),
        name="gla_mixer",
    )(zg, la, onorm)


def _dense_kernel(*refs, mix, proj, d_s5, d_ff):
    refs = list(refs)
    x_ref = refs.pop(0)
    n_groups = x_ref.shape[0] // DENSE_GROUP
    groups = [pl.ds(s * DENSE_GROUP, DENSE_GROUP) for s in range(n_groups)]
    xs = [x_ref[r, :] for r in groups]
    if mix:
        ys5_ref, ygla_ref, wout_ref, nffn_ref, wfi_ref, wfo_ref = refs[:6]
        refs = refs[6:]
        xs = [x + _dot(ys5_ref[r, :].astype(BF16), wout_ref[0:d_s5, :]) + _dot(ygla_ref[r, :], wout_ref[d_s5:, :])
              for x, r in zip(xs, groups)]
        hs = [_rms(x, nffn_ref[...]).astype(BF16) for x in xs]
        tiles = d_ff // MXU_DIM
        bounds = [MXU_DIM * ((tiles * c + FFN_CHUNKS - 1) // FFN_CHUNKS) for c in range(FFN_CHUNKS + 1)]
        acts = [[] for _ in range(n_groups)]
        for lo, hi in zip(bounds[:-1], bounds[1:]):
            for s in range(n_groups):
                gate = _dot(hs[s], wfi_ref[:, lo:hi])
                up = _dot(hs[s], wfi_ref[:, d_ff + lo:d_ff + hi])
                acts[s].append((jax.nn.silu(gate) * up).astype(BF16))
        xs = [x + _dot(jnp.concatenate(a, axis=1), wfo_ref[...]) for x, a in zip(xs, acts)]
    norm_ref = refs.pop(0)
    hs = [_rms(x, norm_ref[...]) for x in xs]
    if not proj:
        (out_ref,) = refs
        for h, r in zip(hs, groups):
            out_ref[r, :] = h
        return
    win_ref, bgate_ref = refs[:2]
    u_ref, zg_ref, la_ref = refs[-3:]
    o_a = d_s5 + zg_ref.shape[1]
    for x, h, r in zip(xs, hs, groups):
        if mix:
            refs[2][r, :] = x
        z = _dot(h.astype(BF16), win_ref[...])
        u_ref[r, :] = z[:, :d_s5]
        zg_ref[r, :] = z[:, d_s5:o_a].astype(BF16)
        la_ref[r, :] = _log_sigmoid(z[:, o_a:] + bgate_ref[...]) * (1.0 / GLA_GATE_NORM)


def _dense(x, mix_args, norm, proj_args, *, d_s5):
    rows_total, d_model = x.shape
    tm = DENSE_ROWS if mix_args is not None else DENSE_PROJ_ROWS
    token_rows = lambda w: pl.BlockSpec((tm, w), lambda r: (r, 0))
    mix, proj = mix_args is not None, proj_args is not None
    args, in_specs = [x], [token_rows(d_model)]
    d_ff = 0
    if mix:
        y_s5, y_gla, w_out, norm_ffn, w_fi, w_fo = mix_args
        d_ff = w_fo.shape[0]
        assert d_ff % MXU_DIM == 0
        args += [y_s5, y_gla]
        in_specs += [token_rows(d_s5), token_rows(y_gla.shape[1])]
        resident = [w_out, norm_ffn, w_fi, w_fo, norm]
    else:
        resident = [norm]
    if proj:
        w_in, b_gate = proj_args
        la_w = b_gate.shape[1]
        zg_w = w_in.shape[1] - d_s5 - la_w
        resident += [w_in, b_gate]
        out_specs = [token_rows(d_model), token_rows(d_s5), token_rows(zg_w), token_rows(la_w)]
        out_shape = [jax.ShapeDtypeStruct((rows_total, d_model), F32),
                     jax.ShapeDtypeStruct((rows_total, d_s5), F32),
                     jax.ShapeDtypeStruct((rows_total, zg_w), BF16),
                     jax.ShapeDtypeStruct((rows_total, la_w), F32)]
        if not mix:
            out_specs, out_shape = out_specs[1:], out_shape[1:]
    else:
        out_specs = token_rows(d_model)
        out_shape = jax.ShapeDtypeStruct((rows_total, d_model), F32)
    for operand, spec in map(_resident, resident):
        args.append(operand)
        in_specs.append(spec)
    return pl.pallas_call(
        functools.partial(_dense_kernel, mix=mix, proj=proj, d_s5=d_s5, d_ff=d_ff),
        grid=(rows_total // tm,),
        in_specs=in_specs,
        out_specs=out_specs,
        out_shape=out_shape,
        compiler_params=pltpu.CompilerParams(dimension_semantics=("arbitrary",),
                                             vmem_limit_bytes=VMEM_LIMIT_BYTES),
        name="dense_" + ("mix" if mix else "") + ("proj" if proj else "final"),
    )(*args)


def kernel(x, norm_mix, w_in, s5_lam_re, s5_lam_im, s5_b_re, s5_b_im, s5_c_re, s5_c_im, s5_d, s5_log_step,
           s5_w_glu, s5_b_glu, s5_out_norm, gla_w_gate, gla_b_gate, gla_out_norm, w_out, norm_ffn, w_ffn_in,
           w_ffn_out, norm_final):
    batch, seq, d_model = x.shape
    depth = w_in.shape[0]
    d_s5 = s5_d.shape[1]
    d_gla = gla_out_norm.shape[1]
    kdim = gla_w_gate.shape[2]
    o_gate = d_s5 + 2 * kdim + 2 * d_gla
    assert (batch * seq) % DENSE_PROJ_ROWS == 0 and DENSE_PROJ_ROWS % DENSE_ROWS == 0 and seq % GLA_CHUNK == 0 and seq % (S5_CHUNK * S5_CHUNKS) == 0

    s5_m, s5_e, s5_f_re, s5_f_im, s5_apow = _s5_prep(s5_lam_re, s5_lam_im, s5_log_step, s5_b_re, s5_b_im,
                                                     s5_c_re, s5_c_im)
    s5_f = jnp.concatenate([s5_f_re, s5_f_im], axis=-1)
    w_eff = _gate_fold(w_in[:, :, o_gate:], gla_w_gate)
    w_in_eff = jnp.concatenate([w_in[:, :, :o_gate].astype(BF16), w_eff], axis=-1)
    w_out_b, w_fi_b, w_fo_b, w_glu_b = (w.astype(BF16) for w in (w_out, w_ffn_in, w_ffn_out, s5_w_glu))
    layer = _Layer
    vec = lambda v, i: _Layer(v[:, None, :], i)

    xf = x.reshape(batch * seq, d_model)
    u, zg, la = _dense(xf, None, vec(norm_mix, 0), (layer(w_in_eff, 0), vec(gla_b_gate, 0)), d_s5=d_s5)
    for i in range(depth):
        y_s5 = _s5_mixer(u.reshape(batch, seq, d_s5), layer(s5_m, i), layer(s5_e, i), layer(s5_f, i),
                         layer(s5_apow, i), vec(s5_d, i), layer(w_glu_b, i), vec(s5_b_glu, i), vec(s5_out_norm, i))
        y_gla = _gla_mixer(zg, la, vec(gla_out_norm, i), batch=batch, d_gla=d_gla)
        mix_args = (y_s5.reshape(batch * seq, d_s5), y_gla, layer(w_out_b, i), vec(norm_ffn, i),
                    layer(w_fi_b, i), layer(w_fo_b, i))
        if i + 1 < depth:
            xf, u, zg, la = _dense(xf, mix_args, vec(norm_mix, i + 1),
                                   (layer(w_in_eff, i + 1), vec(gla_b_gate, i + 1)), d_s5=d_s5)
        else:
            xf = _dense(xf, mix_args, norm_final[None, :], None, d_s5=d_s5)
    return xf.reshape(batch, seq, d_model)
```

```python
import functools
from typing import NamedTuple

import jax
import jax.numpy as jnp
from jax import lax
from jax.experimental import pallas as pl
from jax.experimental.pallas import tpu as pltpu

S5_GROUP = 16
GLA_HEADS = 4
GLA_CHUNK = 64
GLA_GATE_NORM = 16.0
EPS = 1e-6

LANES = 128
MXU_DIM = 256
VMEM_LIMIT_BYTES = 56 * 1024 * 1024

DENSE_ROWS = 512
DENSE_PROJ_ROWS = 2048
DENSE_FINAL_ROWS = 1024
DENSE_GROUP = 256
FFN_CHUNKS = 2
S5_CHUNK = 16
S5_CHUNKS = 8
S5_FINISH_ROWS = 512
S5_PREP_GROUPS = 8
GLA_ROWS = 2048

BF16 = jnp.bfloat16
F32 = jnp.float32


def _dot(a, b):
    return jnp.dot(a, b, preferred_element_type=F32)


def _dot_nt(a, b):
    return lax.dot_general(a, b, (((1,), (1,)), ((), ())), preferred_element_type=F32)


def _dot_tn(a, b):
    return lax.dot_general(a, b, (((0,), (0,)), ((), ())), preferred_element_type=F32)


def _rms(x, gain):
    return x * lax.rsqrt(jnp.mean(x * x, axis=-1, keepdims=True) + EPS) * gain


class _Layer(NamedTuple):
    stacked: jax.Array
    index: int

    @property
    def shape(self):
        return self.stacked.shape[1:]


def _resident(x):
    if isinstance(x, _Layer):
        lead, shape, operand = (x.index,), x.shape, x.stacked
        block = (None,) + shape
    else:
        lead, shape, operand = (), x.shape, x
        block = shape
    return operand, pl.BlockSpec(block, lambda *_: lead + (0,) * len(shape), pipeline_mode=pl.Buffered(1))


def _dot_split(a, b):
    a_hi, b_hi = a.astype(BF16), b.astype(BF16)
    a_lo = (a - a_hi.astype(F32)).astype(BF16)
    b_lo = (b - b_hi.astype(F32)).astype(BF16)
    return _dot(a_hi, b_hi) + _dot(a_hi, b_lo) + _dot(a_lo, b_hi)


def _gate_fold_kernel(win_gate_ref, w_gate_ref, weff_ref):
    weff_ref[0] = _dot_split(win_gate_ref[0], w_gate_ref[0]).astype(BF16)


def _gate_fold(win_gate, w_gate):
    depth, d_model, rank = win_gate.shape
    kdim = w_gate.shape[2]
    per_layer = lambda *s: pl.BlockSpec((1,) + s, lambda i: (i,) + (0,) * len(s))
    return pl.pallas_call(
        _gate_fold_kernel,
        grid=(depth,),
        in_specs=[per_layer(d_model, rank), per_layer(rank, kdim)],
        out_specs=per_layer(d_model, kdim),
        out_shape=jax.ShapeDtypeStruct((depth, d_model, kdim), BF16),
        compiler_params=pltpu.CompilerParams(dimension_semantics=("arbitrary",)),
        name="gate_fold",
    )(win_gate, w_gate)


def _cmul(ar, ai, br, bi):
    return ar * br - ai * bi, ar * bi + ai * br


def _discretise(lr, li, log_step):
    step = jnp.exp(log_step)
    mag = jnp.exp(lr * step)
    ar = mag * jnp.cos(li * step)
    ai = mag * jnp.sin(li * step)
    den = lr * lr + li * li
    fr = ((ar - 1.0) * lr + ai * li) / den
    fi = (ai * lr - (ar - 1.0) * li) / den
    return ar, ai, fr, fi


def _s5_prep_kernel(lam_re_row_ref, lam_im_row_ref, lam_re_col_ref, lam_im_col_ref, log_step_ref,
                    c_re_ref, c_im_ref, b_re_ref, b_im_ref, m_ref, e_ref, f_re_ref, f_im_ref, apow_ref):
    t_c = S5_CHUNK
    h_c = S5_GROUP
    p_c = lam_re_row_ref.shape[-1]

    def one_group(gi, _):
        log_step = log_step_ref[0, gi]
        ar, ai, _, _ = _discretise(lam_re_row_ref[0, gi], lam_im_row_ref[0, gi], log_step)
        arc, aic, frc, fic = _discretise(lam_re_col_ref[0, gi], lam_im_col_ref[0, gi], log_step)

        pr = jnp.ones((h_c, p_c), F32)
        pi = jnp.zeros((h_c, p_c), F32)
        blocks_r, blocks_i = [], []
        for _ in range(t_c):
            blocks_r.append(pr)
            blocks_i.append(pi)
            pr, pi = _cmul(pr, pi, ar, ai)
        l_re, l_im = _cmul(c_re_ref[0, gi], c_im_ref[0, gi], jnp.concatenate(blocks_r, axis=0),
                           jnp.concatenate(blocks_i, axis=0))
        f_re, f_im = _cmul(l_re, l_im, ar, ai)
        f_re_ref[0, gi] = f_re.astype(BF16)
        f_im_ref[0, gi] = (-f_im).astype(BF16)

        bw_re, bw_im = _cmul(frc, fic, b_re_ref[0, gi], b_im_ref[0, gi])
        k_wide = _dot_split(l_re, bw_re) - _dot_split(l_im, bw_im)
        per_tile = LANES // h_c
        lane_blk = lax.broadcasted_iota(jnp.int32, (t_c * h_c, LANES), 1) // h_c
        for lt in range(t_c // per_tile):
            k_tile = k_wide[:, lt * LANES:(lt + 1) * LANES]
            m = jnp.zeros((t_c * h_c, LANES), F32)
            for jj in range(per_tile):
                j = lt * per_tile + jj
                moved = k_tile if j == 0 else jnp.concatenate(
                    [jnp.zeros((j * h_c, LANES), F32), k_tile[:(t_c - j) * h_c]], axis=0)
                m = jnp.where(lane_blk == jj, moved, m)
            m_ref[0, gi, :, lt * LANES:(lt + 1) * LANES] = m.astype(BF16)

        lane_blk = lax.broadcasted_iota(jnp.int32, (p_c, LANES), 1) // h_c
        qr = jnp.ones((p_c, 1), F32)
        qi = jnp.zeros((p_c, 1), F32)
        for lt in range(t_c // per_tile - 1, -1, -1):
            pe_r = jnp.zeros((p_c, LANES), F32)
            pe_i = jnp.zeros((p_c, LANES), F32)
            for jj in range(per_tile - 1, -1, -1):
                pe_r = jnp.where(lane_blk == jj, qr, pe_r)
                pe_i = jnp.where(lane_blk == jj, qi, pe_i)
                qr, qi = _cmul(qr, qi, arc, aic)
            cols = slice(lt * LANES, (lt + 1) * LANES)
            e_re, e_im = _cmul(pe_r, pe_i, bw_re[:, cols], bw_im[:, cols])
            e_ref[0, gi, 0:p_c, cols] = e_re.astype(BF16)
            e_ref[0, gi, p_c:2 * p_c, cols] = e_im.astype(BF16)

        for k in range(3):
            apow_ref[0, gi, k, 0] = jnp.broadcast_to(qr, (p_c, LANES))
            apow_ref[0, gi, k, 1] = jnp.broadcast_to(qi, (p_c, LANES))
            qr, qi = _cmul(qr, qi, qr, qi)
        return 0

    lax.fori_loop(0, m_ref.shape[1], one_group, 0)


def _s5_prep(lam_re, lam_im, log_step, b_re, b_im, c_re, c_im):
    depth, groups, states = lam_re.shape
    t_c, h_c = S5_CHUNK, S5_GROUP
    row = lambda t: t.reshape(depth, groups, 1, states)
    col = lambda t: t.reshape(depth, groups, states, 1)
    c_rows = lambda t: jnp.tile(t, (1, 1, t_c, 1))
    b_lanes = lambda t: jnp.tile(t, (1, 1, 1, t_c))
    blk = lambda *s: pl.BlockSpec((1, S5_PREP_GROUPS) + s, lambda i, g: (i, g) + (0,) * len(s))
    w = t_c * h_c
    return pl.pallas_call(
        _s5_prep_kernel,
        grid=(depth, groups // S5_PREP_GROUPS),
        in_specs=[blk(1, states)] * 2 + [blk(states, 1)] * 2 + [blk(1, 1)]
        + [blk(w, states)] * 2 + [blk(states, w)] * 2,
        out_specs=[blk(w, w), blk(2 * states, w), blk(w, states), blk(w, states), blk(3, 2, states, LANES)],
        out_shape=[jax.ShapeDtypeStruct((depth, groups, w, w), BF16),
                   jax.ShapeDtypeStruct((depth, groups, 2 * states, w), BF16),
                   jax.ShapeDtypeStruct((depth, groups, w, states), BF16),
                   jax.ShapeDtypeStruct((depth, groups, w, states), BF16),
                   jax.ShapeDtypeStruct((depth, groups, 3, 2, states, LANES), F32)],
        compiler_params=pltpu.CompilerParams(dimension_semantics=("arbitrary", "arbitrary")),
        name="s5_prep",
    )(row(lam_re), row(lam_im), col(lam_re), col(lam_im), log_step.reshape(depth, groups, 1, 1),
      c_rows(c_re), c_rows(c_im), b_lanes(b_re), b_lanes(b_im))


def _s5_kernel(u_ref, m_ref, e_ref, f_ref, apow_ref, d_ref, wglu_ref, bglu_ref, onorm_ref, y_ref,
               ub_ref, xt_ref, yt_ref, y2_ref, z_ref, s_ref, carry_ref, perm_ref):
    n_b, t_blk, d_s5 = u_ref.shape
    groups = m_ref.shape[0]
    t_c, h_c = S5_CHUNK, S5_GROUP
    n_c = t_blk // t_c
    p_c = f_ref.shape[2] // 2
    samples = n_b * n_c
    assert samples == LANES and n_c == 8

    @pl.when(pl.program_id(1) == 0)
    def _():
        carry_ref[...] = jnp.zeros_like(carry_ref)

    @pl.when((pl.program_id(0) == 0) & (pl.program_id(1) == 0))
    def _():
        dst = lax.broadcasted_iota(jnp.int32, (t_blk, t_blk), 0)
        src = lax.broadcasted_iota(jnp.int32, (t_blk, t_blk), 1)
        perm_ref[0] = jnp.where(src == (dst % n_c) * t_c + dst // n_c, 1.0, 0.0).astype(BF16)
        perm_ref[1] = jnp.where(src == (dst % t_c) * n_c + dst // t_c, 1.0, 0.0).astype(BF16)

    for b in range(n_b):
        ub_ref[b] = _dot(perm_ref[0], u_ref[b].astype(BF16))
    for t in range(t_c):
        xt_ref[t] = ub_ref[:, t * n_c:(t + 1) * n_c, :].reshape(samples, d_s5).T
    for g in range(groups):
        a = xt_ref[:, g * h_c:(g + 1) * h_c, :].reshape(t_c * h_c, samples).astype(BF16)
        yt_ref[g] = _dot(m_ref[g], a)
        z_ref[g] = _dot(e_ref[g], a)

    c_idx = lax.broadcasted_iota(jnp.int32, (p_c, LANES), 1) % n_c
    for g in range(groups):
        z_re, z_im = z_ref[g, 0:p_c], z_ref[g, p_c:2 * p_c]
        v_re = jnp.where(c_idx == 0, carry_ref[g, 0:p_c], pltpu.roll(z_re, 1, 1))
        v_im = jnp.where(c_idx == 0, carry_ref[g, p_c:2 * p_c], pltpu.roll(z_im, 1, 1))
        for k in range(3):
            hop = 1 << k
            sh_re = jnp.where(c_idx >= hop, pltpu.roll(v_re, hop, 1), 0.0)
            sh_im = jnp.where(c_idx >= hop, pltpu.roll(v_im, hop, 1), 0.0)
            d_re, d_im = _cmul(apow_ref[g, k, 0], apow_ref[g, k, 1], sh_re, sh_im)
            v_re, v_im = v_re + d_re, v_im + d_im
        s_ref[g, 0:p_c] = v_re.astype(BF16)
        s_ref[g, p_c:2 * p_c] = v_im.astype(BF16)
        o_re, o_im = _cmul(apow_ref[g, 0, 0], apow_ref[g, 0, 1], v_re, v_im)
        carry_ref[g, 0:p_c] = pltpu.roll(o_re + z_re, LANES - (n_c - 1), 1)
        carry_ref[g, p_c:2 * p_c] = pltpu.roll(o_im + z_im, LANES - (n_c - 1), 1)

    for g in range(groups):
        yt_ref[g] += _dot(f_ref[g], s_ref[g])
    for t in range(t_c):
        y2_ref[t] = yt_ref[:, t * h_c:(t + 1) * h_c, :].reshape(d_s5, samples).T
    for b in range(n_b):
        y = y2_ref[:, b * n_c:(b + 1) * n_c, :].reshape(t_blk, d_s5)
        y_hi = y.astype(BF16)
        y_lo = (y - y_hi.astype(F32)).astype(BF16)
        y2 = _dot(perm_ref[1], jnp.concatenate([y_hi, y_lo], axis=-1))
        ub_ref[b] = y2[:, :d_s5] + y2[:, d_s5:] + d_ref[...] * u_ref[b]
    fin = S5_FINISH_ROWS // t_blk
    for q in range(n_b // fin):
        y = ub_ref[q * fin:(q + 1) * fin].reshape(S5_FINISH_ROWS, d_s5)
        y = jax.nn.gelu(y)
        y = y * jax.nn.sigmoid(_dot(y.astype(BF16), wglu_ref[...]) + bglu_ref[...])
        y_ref[q * fin:(q + 1) * fin] = _rms(y, onorm_ref[...]).reshape(fin, t_blk, d_s5)


def _s5_mixer(u, m, e, f, apow, d_skip, wglu, bglu, onorm):
    batch, seq, d_s5 = u.shape
    groups, w = m.shape[0], m.shape[1]
    states = f.shape[2] // 2
    t_blk = S5_CHUNK * S5_CHUNKS
    n_b = LANES // S5_CHUNKS
    assert batch % n_b == 0 and seq % t_blk == 0
    block = pl.BlockSpec((n_b, t_blk, d_s5), lambda bb, t: (bb, t, 0))
    params, param_specs = zip(*map(_resident, (m, e, f, apow, d_skip, wglu, bglu, onorm)))
    return pl.pallas_call(
        _s5_kernel,
        grid=(batch // n_b, seq // t_blk),
        in_specs=[block, *param_specs],
        out_specs=block,
        out_shape=jax.ShapeDtypeStruct((batch, seq, d_s5), F32),
        scratch_shapes=[pltpu.VMEM((n_b, t_blk, d_s5), F32),
                        pltpu.VMEM((S5_CHUNK, d_s5, LANES), F32),
                        pltpu.VMEM((groups, w, LANES), F32),
                        pltpu.VMEM((S5_CHUNK, LANES, d_s5), F32),
                        pltpu.VMEM((groups, 2 * states, LANES), F32),
                        pltpu.VMEM((groups, 2 * states, LANES), BF16),
                        pltpu.VMEM((groups, 2 * states, LANES), F32),
                        pltpu.VMEM((2, t_blk, t_blk), BF16)],
        compiler_params=pltpu.CompilerParams(dimension_semantics=("arbitrary", "arbitrary"),
                                             vmem_limit_bytes=VMEM_LIMIT_BYTES),
        name="s5_mixer",
    )(u, *params)


def _log_sigmoid(x):
    return jnp.minimum(x, 0.0) - jnp.log1p(jnp.exp(-jnp.abs(x)))


def _gla_kernel(zg_ref, la_ref, onorm_ref, y_ref, st_ref):
    rows = zg_ref.shape[0]
    d_gla = y_ref.shape[1]
    dv = d_gla // GLA_HEADS
    dk = dv // 2
    kdim = GLA_HEADS * dk
    o_k, o_v, o_g = kdim, 2 * kdim, 2 * kdim + d_gla
    c = GLA_CHUNK
    assert 2 * dk == LANES and dv == LANES

    @pl.when(pl.program_id(1) == 0)
    def _():
        st_ref[...] = jnp.zeros_like(st_ref)

    r_i = lax.broadcasted_iota(jnp.int32, (c, 3 * c), 0)
    c_i = lax.broadcasted_iota(jnp.int32, (c, 3 * c), 1)
    tri3 = jnp.where((c_i % c) <= r_i, 1.0, 0.0).astype(BF16)
    causal = (lax.broadcasted_iota(jnp.int32, (c, LANES), 0)
              >= lax.broadcasted_iota(jnp.int32, (c, LANES), 1) % c)
    low_head = lax.broadcasted_iota(jnp.int32, (c, LANES), 1) < dk
    own_block = ((lax.broadcasted_iota(jnp.int32, (2 * dv, LANES), 0) < dv)
                 == (lax.broadcasted_iota(jnp.int32, (2 * dv, LANES), 1) < dk))
    n_chunks = rows // c
    pairs = GLA_HEADS // 2
    rows_of = lambda t, n: t[n * c:(n + 1) * c]
    zeros_v = jnp.zeros((c, dv), BF16)

    la = la_ref[...]
    la_hi = la.astype(BF16)
    rem = la - la_hi.astype(F32)
    la_mid = rem.astype(BF16)
    la_lo = (rem - la_mid.astype(F32)).astype(BF16)
    cums = [_dot(tri3, jnp.concatenate([rows_of(la_hi, n), rows_of(la_mid, n), rows_of(la_lo, n)], axis=0))
            for n in range(n_chunks)]
    cum = jnp.concatenate(cums, axis=0)
    totals = [t[c - 1:c, :] for t in cums]
    total = jnp.concatenate([jnp.broadcast_to(t, (c, kdim)) for t in totals], axis=0)
    q_dec = (zg_ref[:, 0:kdim].astype(F32) * (dk ** -0.5) * jnp.exp(cum)).astype(BF16)
    k = zg_ref[:, o_k:o_k + kdim].astype(F32)
    k_inv = k * jnp.exp(-cum)
    k_end = (k * jnp.exp(total - cum)).astype(BF16)
    v = zg_ref[:, o_v:o_v + d_gla]

    def lanes_of(t, n, pair):
        return t[n * c:(n + 1) * c, pair * LANES:(pair + 1) * LANES]

    scores = []
    for n in range(n_chunks):
        for pair in range(pairs):
            kp = lanes_of(k_inv, n, pair)
            keys = jnp.concatenate([jnp.where(low_head, kp, 0.0), jnp.where(low_head, 0.0, kp)], axis=0)
            s = _dot_nt(lanes_of(q_dec, n, pair), keys.astype(BF16))
            scores.append(jnp.where(causal, s, 0.0).astype(BF16))

    intra, upds = [], []
    for n in range(n_chunks):
        for pair in range(pairs):
            v_even = v[n * c:(n + 1) * c, (2 * pair) * dv:(2 * pair + 1) * dv]
            v_odd = v[n * c:(n + 1) * c, (2 * pair + 1) * dv:(2 * pair + 2) * dv]
            v_diag = jnp.concatenate([jnp.concatenate([v_even, zeros_v], axis=1),
                                      jnp.concatenate([zeros_v, v_odd], axis=1)], axis=0)
            intra.append(_dot(scores[n * pairs + pair], v_diag))
            upd = _dot_tn(jnp.concatenate([v_even, v_odd], axis=1), lanes_of(k_end, n, pair))
            upds.append(jnp.where(own_block, upd, 0.0))

    outs = [[None] * pairs for _ in range(n_chunks)]
    for pair in range(pairs):
        st = st_ref[pair]
        for n in range(n_chunks):
            outs[n][pair] = intra[n * pairs + pair] + _dot_nt(lanes_of(q_dec, n, pair), st.astype(BF16))
            decay = jnp.exp(totals[n][:, pair * LANES:(pair + 1) * LANES])
            st = st * decay + upds[n * pairs + pair]
        st_ref[pair] = st

    for h in range(GLA_HEADS):
        o = jnp.concatenate([outs[n][h // 2][:, (h % 2) * dv:(h % 2 + 1) * dv] for n in range(n_chunks)], axis=0)
        o = o * lax.rsqrt(jnp.mean(o * o, axis=-1, keepdims=True) + EPS)
        g = zg_ref[:, o_g + h * dv:o_g + (h + 1) * dv].astype(F32)
        y = jax.nn.silu(g) * (o * onorm_ref[:, h * dv:(h + 1) * dv])
        y_ref[:, h * dv:(h + 1) * dv] = y.astype(y_ref.dtype)


def _gla_mixer(zg, la, onorm, *, batch, d_gla):
    rows_total = zg.shape[0]
    seq = rows_total // batch
    rows = min(GLA_ROWS, seq)
    n_t = seq // rows
    token_rows = lambda w: pl.BlockSpec((rows, w), lambda b, t: (b * n_t + t, 0))
    onorm, onorm_spec = _resident(onorm)
    return pl.pallas_call(
        _gla_kernel,
        grid=(batch, n_t),
        in_specs=[token_rows(zg.shape[1]), token_rows(la.shape[1]), onorm_spec],
        out_specs=token_rows(d_gla),
        out_shape=jax.ShapeDtypeStruct((rows_total, d_gla), BF16),
        scratch_shapes=[pltpu.VMEM((GLA_HEADS // 2, 2 * d_gla // GLA_HEADS, LANES), F32)],
        compiler_params=pltpu.CompilerParams(dimension_semantics=("arbitrary", "arbitrary"),
                                             vmem_limit_bytes=VMEM_LIMIT_BYTES),
        name="gla_mixer",
    )(zg, la, onorm)


def _dense_kernel(*refs, mix, proj, d_s5, d_ff):
    refs = list(refs)
    x_ref = refs.pop(0)
    n_groups = x_ref.shape[0] // DENSE_GROUP
    groups = [pl.ds(s * DENSE_GROUP, DENSE_GROUP) for s in range(n_groups)]
    xs = [x_ref[r, :] for r in groups]
    if mix:
        ys5_ref, ygla_ref, wout_ref, nffn_ref, wfi_ref, wfo_ref = refs[:6]
        refs = refs[6:]
        xs = [x + _dot(ys5_ref[r, :].astype(BF16), wout_ref[0:d_s5, :]) + _dot(ygla_ref[r, :], wout_ref[d_s5:, :])
              for x, r in zip(xs, groups)]
        hs = [_rms(x, nffn_ref[...]).astype(BF16) for x in xs]
        tiles = d_ff // MXU_DIM
        bounds = [MXU_DIM * ((tiles * c + FFN_CHUNKS - 1) // FFN_CHUNKS) for c in range(FFN_CHUNKS + 1)]
        acts = [[] for _ in range(n_groups)]
        for lo, hi in zip(bounds[:-1], bounds[1:]):
            for s in range(n_groups):
                gate = _dot(hs[s], wfi_ref[:, lo:hi])
                up = _dot(hs[s], wfi_ref[:, d_ff + lo:d_ff + hi])
                acts[s].append((jax.nn.silu(gate) * up).astype(BF16))
        xs = [x + _dot(jnp.concatenate(a, axis=1), wfo_ref[...]) for x, a in zip(xs, acts)]
    norm_ref = refs.pop(0)
    hs = [_rms(x, norm_ref[...]) for x in xs]
    if not proj:
        (out_ref,) = refs
        for h, r in zip(hs, groups):
            out_ref[r, :] = h
        return
    win_ref, bgate_ref = refs[:2]
    u_ref, zg_ref, la_ref = refs[-3:]
    o_a = d_s5 + zg_ref.shape[1]
    for x, h, r in zip(xs, hs, groups):
        if mix:
            refs[2][r, :] = x
        z = _dot(h.astype(BF16), win_ref[...])
        u_ref[r, :] = z[:, :d_s5]
        zg_ref[r, :] = z[:, d_s5:o_a].astype(BF16)
        la_ref[r, :] = _log_sigmoid(z[:, o_a:] + bgate_ref[...]) * (1.0 / GLA_GATE_NORM)


def _dense(x, mix_args, norm, proj_args, *, d_s5):
    rows_total, d_model = x.shape
    tm = DENSE_PROJ_ROWS if mix_args is None else DENSE_ROWS if proj_args is not None else DENSE_FINAL_ROWS
    token_rows = lambda w: pl.BlockSpec((tm, w), lambda r: (r, 0))
    mix, proj = mix_args is not None, proj_args is not None
    args, in_specs = [x], [token_rows(d_model)]
    d_ff = 0
    if mix:
        y_s5, y_gla, w_out, norm_ffn, w_fi, w_fo = mix_args
        d_ff = w_fo.shape[0]
        assert d_ff % MXU_DIM == 0
        args += [y_s5, y_gla]
        in_specs += [token_rows(d_s5), token_rows(y_gla.shape[1])]
        resident = [w_out, norm_ffn, w_fi, w_fo, norm]
    else:
        resident = [norm]
    if proj:
        w_in, b_gate = proj_args
        la_w = b_gate.shape[1]
        zg_w = w_in.shape[1] - d_s5 - la_w
        resident += [w_in, b_gate]
        out_specs = [token_rows(d_model), token_rows(d_s5), token_rows(zg_w), token_rows(la_w)]
        out_shape = [jax.ShapeDtypeStruct((rows_total, d_model), F32),
                     jax.ShapeDtypeStruct((rows_total, d_s5), F32),
                     jax.ShapeDtypeStruct((rows_total, zg_w), BF16),
                     jax.ShapeDtypeStruct((rows_total, la_w), F32)]
        if not mix:
            out_specs, out_shape = out_specs[1:], out_shape[1:]
    else:
        out_specs = token_rows(d_model)
        out_shape = jax.ShapeDtypeStruct((rows_total, d_model), F32)
    for operand, spec in map(_resident, resident):
        args.append(operand)
        in_specs.append(spec)
    return pl.pallas_call(
        functools.partial(_dense_kernel, mix=mix, proj=proj, d_s5=d_s5, d_ff=d_ff),
        grid=(rows_total // tm,),
        in_specs=in_specs,
        out_specs=out_specs,
        out_shape=out_shape,
        compiler_params=pltpu.CompilerParams(dimension_semantics=("arbitrary",),
                                             vmem_limit_bytes=VMEM_LIMIT_BYTES),
        name="dense_" + ("mix" if mix else "") + ("proj" if proj else "final"),
    )(*args)


def kernel(x, norm_mix, w_in, s5_lam_re, s5_lam_im, s5_b_re, s5_b_im, s5_c_re, s5_c_im, s5_d, s5_log_step,
           s5_w_glu, s5_b_glu, s5_out_norm, gla_w_gate, gla_b_gate, gla_out_norm, w_out, norm_ffn, w_ffn_in,
           w_ffn_out, norm_final):
    batch, seq, d_model = x.shape
    depth = w_in.shape[0]
    d_s5 = s5_d.shape[1]
    d_gla = gla_out_norm.shape[1]
    kdim = gla_w_gate.shape[2]
    o_gate = d_s5 + 2 * kdim + 2 * d_gla
    assert (batch * seq) % DENSE_PROJ_ROWS == 0 and DENSE_PROJ_ROWS % DENSE_FINAL_ROWS == 0 and DENSE_FINAL_ROWS % DENSE_ROWS == 0 and seq % GLA_CHUNK == 0 and seq % (S5_CHUNK * S5_CHUNKS) == 0

    s5_m, s5_e, s5_f_re, s5_f_im, s5_apow = _s5_prep(s5_lam_re, s5_lam_im, s5_log_step, s5_b_re, s5_b_im,
                                                     s5_c_re, s5_c_im)
    s5_f = jnp.concatenate([s5_f_re, s5_f_im], axis=-1)
    w_eff = _gate_fold(w_in[:, :, o_gate:], gla_w_gate)
    w_in_eff = jnp.concatenate([w_in[:, :, :o_gate].astype(BF16), w_eff], axis=-1)
    w_out_b, w_fi_b, w_fo_b, w_glu_b = (w.astype(BF16) for w in (w_out, w_ffn_in, w_ffn_out, s5_w_glu))
    layer = _Layer
    vec = lambda v, i: _Layer(v[:, None, :], i)

    xf = x.reshape(batch * seq, d_model)
    u, zg, la = _dense(xf, None, vec(norm_mix, 0), (layer(w_in_eff, 0), vec(gla_b_gate, 0)), d_s5=d_s5)
    for i in range(depth):
        y_s5 = _s5_mixer(u.reshape(batch, seq, d_s5), layer(s5_m, i), layer(s5_e, i), layer(s5_f, i),
                         layer(s5_apow, i), vec(s5_d, i), layer(w_glu_b, i), vec(s5_b_glu, i), vec(s5_out_norm, i))
        y_gla = _gla_mixer(zg, la, vec(gla_out_norm, i), batch=batch, d_gla=d_gla)
        mix_args = (y_s5.reshape(batch * seq, d_s5), y_gla, layer(w_out_b, i), vec(norm_ffn, i),
                    layer(w_fi_b, i), layer(w_fo_b, i))
        if i + 1 < depth:
            xf, u, zg, la = _dense(xf, mix_args, vec(norm_mix, i + 1),
                                   (layer(w_in_eff, i + 1), vec(gla_b_gate, i + 1)), d_s5=d_s5)
        else:
            xf = _dense(xf, mix_args, norm_final[None, :], None, d_s5=d_s5)
    return xf.reshape(batch, seq, d_model)
```

```python
import functools
from typing import NamedTuple

import jax
import jax.numpy as jnp
from jax import lax
from jax.experimental import pallas as pl
from jax.experimental.pallas import tpu as pltpu

S5_GROUP = 16
GLA_HEADS = 4
GLA_CHUNK = 64
GLA_GATE_NORM = 16.0
EPS = 1e-6

LANES = 128
MXU_DIM = 256
VMEM_LIMIT_BYTES = 56 * 1024 * 1024

DENSE_ROWS = 512
DENSE_PROJ_ROWS = 2048
DENSE_FINAL_ROWS = 1024
DENSE_GROUP = 256
FFN_CHUNKS = 2
S5_CHUNK = 16
S5_CHUNKS = 8
S5_FINISH_ROWS = 512
S5_PREP_GROUPS = 8
GLA_ROWS = 2048

BF16 = jnp.bfloat16
F32 = jnp.float32


def _dot(a, b):
    return jnp.dot(a, b, preferred_element_type=F32)


def _dot_nt(a, b):
    return lax.dot_general(a, b, (((1,), (1,)), ((), ())), preferred_element_type=F32)


def _dot_tn(a, b):
    return lax.dot_general(a, b, (((0,), (0,)), ((), ())), preferred_element_type=F32)


def _rms(x, gain):
    return x * lax.rsqrt(jnp.mean(x * x, axis=-1, keepdims=True) + EPS) * gain


class _Layer(NamedTuple):
    stacked: jax.Array
    index: int

    @property
    def shape(self):
        return self.stacked.shape[1:]


def _resident(x):
    if isinstance(x, _Layer):
        lead, shape, operand = (x.index,), x.shape, x.stacked
        block = (None,) + shape
    else:
        lead, shape, operand = (), x.shape, x
        block = shape
    return operand, pl.BlockSpec(block, lambda *_: lead + (0,) * len(shape), pipeline_mode=pl.Buffered(1))


def _dot_split(a, b):
    a_hi, b_hi = a.astype(BF16), b.astype(BF16)
    a_lo = (a - a_hi.astype(F32)).astype(BF16)
    b_lo = (b - b_hi.astype(F32)).astype(BF16)
    return _dot(a_hi, b_hi) + _dot(a_hi, b_lo) + _dot(a_lo, b_hi)


def _gate_fold_kernel(win_gate_ref, w_gate_ref, weff_ref):
    weff_ref[0] = _dot_split(win_gate_ref[0], w_gate_ref[0]).astype(BF16)


def _gate_fold(win_gate, w_gate):
    depth, d_model, rank = win_gate.shape
    kdim = w_gate.shape[2]
    per_layer = lambda *s: pl.BlockSpec((1,) + s, lambda i: (i,) + (0,) * len(s))
    return pl.pallas_call(
        _gate_fold_kernel,
        grid=(depth,),
        in_specs=[per_layer(d_model, rank), per_layer(rank, kdim)],
        out_specs=per_layer(d_model, kdim),
        out_shape=jax.ShapeDtypeStruct((depth, d_model, kdim), BF16),
        compiler_params=pltpu.CompilerParams(dimension_semantics=("arbitrary",)),
        name="gate_fold",
    )(win_gate, w_gate)


def _cmul(ar, ai, br, bi):
    return ar * br - ai * bi, ar * bi + ai * br


def _discretise(lr, li, log_step):
    step = jnp.exp(log_step)
    mag = jnp.exp(lr * step)
    ar = mag * jnp.cos(li * step)
    ai = mag * jnp.sin(li * step)
    den = lr * lr + li * li
    fr = ((ar - 1.0) * lr + ai * li) / den
    fi = (ai * lr - (ar - 1.0) * li) / den
    return ar, ai, fr, fi


def _s5_prep_kernel(lam_re_row_ref, lam_im_row_ref, lam_re_col_ref, lam_im_col_ref, log_step_ref,
                    c_re_ref, c_im_ref, b_re_ref, b_im_ref, m_ref, e_ref, f_ref, apow_ref):
    t_c = S5_CHUNK
    h_c = S5_GROUP
    p_c = lam_re_row_ref.shape[-1]

    def one_group(gi, _):
        log_step = log_step_ref[0, gi]
        ar, ai, _, _ = _discretise(lam_re_row_ref[0, gi], lam_im_row_ref[0, gi], log_step)
        arc, aic, frc, fic = _discretise(lam_re_col_ref[0, gi], lam_im_col_ref[0, gi], log_step)

        pr = jnp.ones((h_c, p_c), F32)
        pi = jnp.zeros((h_c, p_c), F32)
        blocks_r, blocks_i = [], []
        for _ in range(t_c):
            blocks_r.append(pr)
            blocks_i.append(pi)
            pr, pi = _cmul(pr, pi, ar, ai)
        l_re, l_im = _cmul(c_re_ref[0, gi], c_im_ref[0, gi], jnp.concatenate(blocks_r, axis=0),
                           jnp.concatenate(blocks_i, axis=0))
        f_re, f_im = _cmul(l_re, l_im, ar, ai)
        f_ref[0, gi] = jnp.concatenate([f_re, -f_im], axis=1).astype(BF16)

        bw_re, bw_im = _cmul(frc, fic, b_re_ref[0, gi], b_im_ref[0, gi])
        k_wide = _dot_split(l_re, bw_re) - _dot_split(l_im, bw_im)
        per_tile = LANES // h_c
        lane_blk = lax.broadcasted_iota(jnp.int32, (t_c * h_c, LANES), 1) // h_c
        for lt in range(t_c // per_tile):
            k_tile = k_wide[:, lt * LANES:(lt + 1) * LANES]
            m = jnp.zeros((t_c * h_c, LANES), F32)
            for jj in range(per_tile):
                j = lt * per_tile + jj
                moved = k_tile if j == 0 else jnp.concatenate(
                    [jnp.zeros((j * h_c, LANES), F32), k_tile[:(t_c - j) * h_c]], axis=0)
                m = jnp.where(lane_blk == jj, moved, m)
            m_ref[0, gi, :, lt * LANES:(lt + 1) * LANES] = m.astype(BF16)

        lane_blk = lax.broadcasted_iota(jnp.int32, (p_c, LANES), 1) // h_c
        qr = jnp.ones((p_c, 1), F32)
        qi = jnp.zeros((p_c, 1), F32)
        for lt in range(t_c // per_tile - 1, -1, -1):
            pe_r = jnp.zeros((p_c, LANES), F32)
            pe_i = jnp.zeros((p_c, LANES), F32)
            for jj in range(per_tile - 1, -1, -1):
                pe_r = jnp.where(lane_blk == jj, qr, pe_r)
                pe_i = jnp.where(lane_blk == jj, qi, pe_i)
                qr, qi = _cmul(qr, qi, arc, aic)
            cols = slice(lt * LANES, (lt + 1) * LANES)
            e_re, e_im = _cmul(pe_r, pe_i, bw_re[:, cols], bw_im[:, cols])
            e_ref[0, gi, 0:p_c, cols] = e_re.astype(BF16)
            e_ref[0, gi, p_c:2 * p_c, cols] = e_im.astype(BF16)

        for k in range(3):
            apow_ref[0, gi, k, 0] = jnp.broadcast_to(qr, (p_c, LANES))
            apow_ref[0, gi, k, 1] = jnp.broadcast_to(qi, (p_c, LANES))
            qr, qi = _cmul(qr, qi, qr, qi)
        return 0

    lax.fori_loop(0, m_ref.shape[1], one_group, 0)


def _s5_prep(lam_re, lam_im, log_step, b_re, b_im, c_re, c_im):
    depth, groups, states = lam_re.shape
    t_c, h_c = S5_CHUNK, S5_GROUP
    row = lambda t: t.reshape(depth, groups, 1, states)
    col = lambda t: t.reshape(depth, groups, states, 1)
    c_rows = lambda t: jnp.tile(t, (1, 1, t_c, 1))
    b_lanes = lambda t: jnp.tile(t, (1, 1, 1, t_c))
    blk = lambda *s: pl.BlockSpec((1, S5_PREP_GROUPS) + s, lambda i, g: (i, g) + (0,) * len(s))
    w = t_c * h_c
    return pl.pallas_call(
        _s5_prep_kernel,
        grid=(depth, groups // S5_PREP_GROUPS),
        in_specs=[blk(1, states)] * 2 + [blk(states, 1)] * 2 + [blk(1, 1)]
        + [blk(w, states)] * 2 + [blk(states, w)] * 2,
        out_specs=[blk(w, w), blk(2 * states, w), blk(w, 2 * states), blk(3, 2, states, LANES)],
        out_shape=[jax.ShapeDtypeStruct((depth, groups, w, w), BF16),
                   jax.ShapeDtypeStruct((depth, groups, 2 * states, w), BF16),
                   jax.ShapeDtypeStruct((depth, groups, w, 2 * states), BF16),
                   jax.ShapeDtypeStruct((depth, groups, 3, 2, states, LANES), F32)],
        compiler_params=pltpu.CompilerParams(dimension_semantics=("arbitrary", "arbitrary")),
        name="s5_prep",
    )(row(lam_re), row(lam_im), col(lam_re), col(lam_im), log_step.reshape(depth, groups, 1, 1),
      c_rows(c_re), c_rows(c_im), b_lanes(b_re), b_lanes(b_im))


def _s5_kernel(u_ref, m_ref, e_ref, f_ref, apow_ref, d_ref, wglu_ref, bglu_ref, onorm_ref, y_ref,
               ub_ref, xt_ref, yt_ref, y2_ref, z_ref, s_ref, carry_ref, perm_ref):
    n_b, t_blk, d_s5 = u_ref.shape
    groups = m_ref.shape[0]
    t_c, h_c = S5_CHUNK, S5_GROUP
    n_c = t_blk // t_c
    p_c = f_ref.shape[2] // 2
    samples = n_b * n_c
    assert samples == LANES and n_c == 8

    @pl.when(pl.program_id(1) == 0)
    def _():
        carry_ref[...] = jnp.zeros_like(carry_ref)

    @pl.when((pl.program_id(0) == 0) & (pl.program_id(1) == 0))
    def _():
        dst = lax.broadcasted_iota(jnp.int32, (t_blk, t_blk), 0)
        src = lax.broadcasted_iota(jnp.int32, (t_blk, t_blk), 1)
        perm_ref[0] = jnp.where(src == (dst % n_c) * t_c + dst // n_c, 1.0, 0.0).astype(BF16)
        perm_ref[1] = jnp.where(src == (dst % t_c) * n_c + dst // t_c, 1.0, 0.0).astype(BF16)

    for b in range(n_b):
        ub_ref[b] = _dot(perm_ref[0], u_ref[b].astype(BF16))
    for t in range(t_c):
        xt_ref[t] = ub_ref[:, t * n_c:(t + 1) * n_c, :].reshape(samples, d_s5).T
    for g in range(groups):
        a = xt_ref[:, g * h_c:(g + 1) * h_c, :].reshape(t_c * h_c, samples).astype(BF16)
        yt_ref[g] = _dot(m_ref[g], a)
        z_ref[g] = _dot(e_ref[g], a)

    c_idx = lax.broadcasted_iota(jnp.int32, (p_c, LANES), 1) % n_c
    for g in range(groups):
        z_re, z_im = z_ref[g, 0:p_c], z_ref[g, p_c:2 * p_c]
        v_re = jnp.where(c_idx == 0, carry_ref[g, 0:p_c], pltpu.roll(z_re, 1, 1))
        v_im = jnp.where(c_idx == 0, carry_ref[g, p_c:2 * p_c], pltpu.roll(z_im, 1, 1))
        for k in range(3):
            hop = 1 << k
            sh_re = jnp.where(c_idx >= hop, pltpu.roll(v_re, hop, 1), 0.0)
            sh_im = jnp.where(c_idx >= hop, pltpu.roll(v_im, hop, 1), 0.0)
            d_re, d_im = _cmul(apow_ref[g, k, 0], apow_ref[g, k, 1], sh_re, sh_im)
            v_re, v_im = v_re + d_re, v_im + d_im
        s_ref[g, 0:p_c] = v_re.astype(BF16)
        s_ref[g, p_c:2 * p_c] = v_im.astype(BF16)
        o_re, o_im = _cmul(apow_ref[g, 0, 0], apow_ref[g, 0, 1], v_re, v_im)
        carry_ref[g, 0:p_c] = pltpu.roll(o_re + z_re, LANES - (n_c - 1), 1)
        carry_ref[g, p_c:2 * p_c] = pltpu.roll(o_im + z_im, LANES - (n_c - 1), 1)

    for g in range(groups):
        yt_ref[g] += _dot(f_ref[g], s_ref[g])
    for t in range(t_c):
        y2_ref[t] = yt_ref[:, t * h_c:(t + 1) * h_c, :].reshape(d_s5, samples).T
    for b in range(n_b):
        y = y2_ref[:, b * n_c:(b + 1) * n_c, :].reshape(t_blk, d_s5)
        y_hi = y.astype(BF16)
        y_lo = (y - y_hi.astype(F32)).astype(BF16)
        y2 = _dot(perm_ref[1], jnp.concatenate([y_hi, y_lo], axis=-1))
        ub_ref[b] = y2[:, :d_s5] + y2[:, d_s5:] + d_ref[...] * u_ref[b]
    fin = S5_FINISH_ROWS // t_blk
    for q in range(n_b // fin):
        y = ub_ref[q * fin:(q + 1) * fin].reshape(S5_FINISH_ROWS, d_s5)
        y = jax.nn.gelu(y)
        y = y * jax.nn.sigmoid(_dot(y.astype(BF16), wglu_ref[...]) + bglu_ref[...])
        y_ref[q * fin:(q + 1) * fin] = _rms(y, onorm_ref[...]).reshape(fin, t_blk, d_s5)


def _s5_mixer(u, m, e, f, apow, d_skip, wglu, bglu, onorm):
    batch, seq, d_s5 = u.shape
    groups, w = m.shape[0], m.shape[1]
    states = f.shape[2] // 2
    t_blk = S5_CHUNK * S5_CHUNKS
    n_b = LANES // S5_CHUNKS
    assert batch % n_b == 0 and seq % t_blk == 0
    block = pl.BlockSpec((n_b, t_blk, d_s5), lambda bb, t: (bb, t, 0))
    params, param_specs = zip(*map(_resident, (m, e, f, apow, d_skip, wglu, bglu, onorm)))
    return pl.pallas_call(
        _s5_kernel,
        grid=(batch // n_b, seq // t_blk),
        in_specs=[block, *param_specs],
        out_specs=block,
        out_shape=jax.ShapeDtypeStruct((batch, seq, d_s5), F32),
        scratch_shapes=[pltpu.VMEM((n_b, t_blk, d_s5), F32),
                        pltpu.VMEM((S5_CHUNK, d_s5, LANES), F32),
                        pltpu.VMEM((groups, w, LANES), F32),
                        pltpu.VMEM((S5_CHUNK, LANES, d_s5), F32),
                        pltpu.VMEM((groups, 2 * states, LANES), F32),
                        pltpu.VMEM((groups, 2 * states, LANES), BF16),
                        pltpu.VMEM((groups, 2 * states, LANES), F32),
                        pltpu.VMEM((2, t_blk, t_blk), BF16)],
        compiler_params=pltpu.CompilerParams(dimension_semantics=("arbitrary", "arbitrary"),
                                             vmem_limit_bytes=VMEM_LIMIT_BYTES),
        name="s5_mixer",
    )(u, *params)


def _log_sigmoid(x):
    return jnp.minimum(x, 0.0) - jnp.log1p(jnp.exp(-jnp.abs(x)))


def _gla_kernel(zg_ref, la_ref, onorm_ref, y_ref, st_ref):
    rows = zg_ref.shape[0]
    d_gla = y_ref.shape[1]
    dv = d_gla // GLA_HEADS
    dk = dv // 2
    kdim = GLA_HEADS * dk
    o_k, o_v, o_g = kdim, 2 * kdim, 2 * kdim + d_gla
    c = GLA_CHUNK
    assert 2 * dk == LANES and dv == LANES

    @pl.when(pl.program_id(1) == 0)
    def _():
        st_ref[...] = jnp.zeros_like(st_ref)

    r_i = lax.broadcasted_iota(jnp.int32, (c, 3 * c), 0)
    c_i = lax.broadcasted_iota(jnp.int32, (c, 3 * c), 1)
    tri3 = jnp.where((c_i % c) <= r_i, 1.0, 0.0).astype(BF16)
    causal = (lax.broadcasted_iota(jnp.int32, (c, LANES), 0)
              >= lax.broadcasted_iota(jnp.int32, (c, LANES), 1) % c)
    low_head = lax.broadcasted_iota(jnp.int32, (c, LANES), 1) < dk
    own_block = ((lax.broadcasted_iota(jnp.int32, (2 * dv, LANES), 0) < dv)
                 == (lax.broadcasted_iota(jnp.int32, (2 * dv, LANES), 1) < dk))
    n_chunks = rows // c
    pairs = GLA_HEADS // 2
    rows_of = lambda t, n: t[n * c:(n + 1) * c]
    zeros_v = jnp.zeros((c, dv), BF16)

    la = la_ref[...]
    la_hi = la.astype(BF16)
    rem = la - la_hi.astype(F32)
    la_mid = rem.astype(BF16)
    la_lo = (rem - la_mid.astype(F32)).astype(BF16)
    cums = [_dot(tri3, jnp.concatenate([rows_of(la_hi, n), rows_of(la_mid, n), rows_of(la_lo, n)], axis=0))
            for n in range(n_chunks)]
    cum = jnp.concatenate(cums, axis=0)
    totals = [t[c - 1:c, :] for t in cums]
    total = jnp.concatenate([jnp.broadcast_to(t, (c, kdim)) for t in totals], axis=0)
    q_dec = (zg_ref[:, 0:kdim].astype(F32) * (dk ** -0.5) * jnp.exp(cum)).astype(BF16)
    k = zg_ref[:, o_k:o_k + kdim].astype(F32)
    k_inv = k * jnp.exp(-cum)
    k_end = (k * jnp.exp(total - cum)).astype(BF16)
    v = zg_ref[:, o_v:o_v + d_gla]

    def lanes_of(t, n, pair):
        return t[n * c:(n + 1) * c, pair * LANES:(pair + 1) * LANES]

    scores = []
    for n in range(n_chunks):
        for pair in range(pairs):
            kp = lanes_of(k_inv, n, pair)
            keys = jnp.concatenate([jnp.where(low_head, kp, 0.0), jnp.where(low_head, 0.0, kp)], axis=0)
            s = _dot_nt(lanes_of(q_dec, n, pair), keys.astype(BF16))
            scores.append(jnp.where(causal, s, 0.0).astype(BF16))

    intra, upds = [], []
    for n in range(n_chunks):
        for pair in range(pairs):
            v_even = v[n * c:(n + 1) * c, (2 * pair) * dv:(2 * pair + 1) * dv]
            v_odd = v[n * c:(n + 1) * c, (2 * pair + 1) * dv:(2 * pair + 2) * dv]
            v_diag = jnp.concatenate([jnp.concatenate([v_even, zeros_v], axis=1),
                                      jnp.concatenate([zeros_v, v_odd], axis=1)], axis=0)
            intra.append(_dot(scores[n * pairs + pair], v_diag))
            upd = _dot_tn(jnp.concatenate([v_even, v_odd], axis=1), lanes_of(k_end, n, pair))
            upds.append(jnp.where(own_block, upd, 0.0))

    outs = [[None] * pairs for _ in range(n_chunks)]
    for pair in range(pairs):
        st = st_ref[pair]
        for n in range(n_chunks):
            outs[n][pair] = intra[n * pairs + pair] + _dot_nt(lanes_of(q_dec, n, pair), st.astype(BF16))
            decay = jnp.exp(totals[n][:, pair * LANES:(pair + 1) * LANES])
            st = st * decay + upds[n * pairs + pair]
        st_ref[pair] = st

    for h in range(GLA_HEADS):
        o = jnp.concatenate([outs[n][h // 2][:, (h % 2) * dv:(h % 2 + 1) * dv] for n in range(n_chunks)], axis=0)
        o = o * lax.rsqrt(jnp.mean(o * o, axis=-1, keepdims=True) + EPS)
        g = zg_ref[:, o_g + h * dv:o_g + (h + 1) * dv].astype(F32)
        y = jax.nn.silu(g) * (o * onorm_ref[:, h * dv:(h + 1) * dv])
        y_ref[:, h * dv:(h + 1) * dv] = y.astype(y_ref.dtype)


def _gla_mixer(zg, la, onorm, *, batch, d_gla):
    rows_total = zg.shape[0]
    seq = rows_total // batch
    rows = min(GLA_ROWS, seq)
    n_t = seq // rows
    token_rows = lambda w: pl.BlockSpec((rows, w), lambda b, t: (b * n_t + t, 0))
    onorm, onorm_spec = _resident(onorm)
    return pl.pallas_call(
        _gla_kernel,
        grid=(batch, n_t),
        in_specs=[token_rows(zg.shape[1]), token_rows(la.shape[1]), onorm_spec],
        out_specs=token_rows(d_gla),
        out_shape=jax.ShapeDtypeStruct((rows_total, d_gla), BF16),
        scratch_shapes=[pltpu.VMEM((GLA_HEADS // 2, 2 * d_gla // GLA_HEADS, LANES), F32)],
        compiler_params=pltpu.CompilerParams(dimension_semantics=("arbitrary", "arbitrary"),
                                             vmem_limit_bytes=VMEM_LIMIT_BYTES),
        name="gla_mixer",
    )(zg, la, onorm)


def _dense_kernel(*refs, mix, proj, d_s5, d_ff):
    refs = list(refs)
    x_ref = refs.pop(0)
    n_groups = x_ref.shape[0] // DENSE_GROUP
    groups = [pl.ds(s * DENSE_GROUP, DENSE_GROUP) for s in range(n_groups)]
    xs = [x_ref[r, :] for r in groups]
    if mix:
        ys5_ref, ygla_ref, wout_ref, nffn_ref, wfi_ref, wfo_ref = refs[:6]
        refs = refs[6:]
        xs = [x + _dot(ys5_ref[r, :].astype(BF16), wout_ref[0:d_s5, :]) + _dot(ygla_ref[r, :], wout_ref[d_s5:, :])
              for x, r in zip(xs, groups)]
        hs = [_rms(x, nffn_ref[...]).astype(BF16) for x in xs]
        tiles = d_ff // MXU_DIM
        bounds = [MXU_DIM * ((tiles * c + FFN_CHUNKS - 1) // FFN_CHUNKS) for c in range(FFN_CHUNKS + 1)]
        acts = [[] for _ in range(n_groups)]
        for lo, hi in zip(bounds[:-1], bounds[1:]):
            for s in range(n_groups):
                gate = _dot(hs[s], wfi_ref[:, lo:hi])
                up = _dot(hs[s], wfi_ref[:, d_ff + lo:d_ff + hi])
                acts[s].append((jax.nn.silu(gate) * up).astype(BF16))
        xs = [x + _dot(jnp.concatenate(a, axis=1), wfo_ref[...]) for x, a in zip(xs, acts)]
    norm_ref = refs.pop(0)
    hs = [_rms(x, norm_ref[...]) for x in xs]
    if not proj:
        (out_ref,) = refs
        for h, r in zip(hs, groups):
            out_ref[r, :] = h
        return
    win_ref, bgate_ref = refs[:2]
    u_ref, zg_ref, la_ref = refs[-3:]
    o_a = d_s5 + zg_ref.shape[1]
    for x, h, r in zip(xs, hs, groups):
        if mix:
            refs[2][r, :] = x
        z = _dot(h.astype(BF16), win_ref[...])
        u_ref[r, :] = z[:, :d_s5]
        zg_ref[r, :] = z[:, d_s5:o_a].astype(BF16)
        la_ref[r, :] = _log_sigmoid(z[:, o_a:] + bgate_ref[...]) * (1.0 / GLA_GATE_NORM)


def _dense(x, mix_args, norm, proj_args, *, d_s5):
    rows_total, d_model = x.shape
    tm = DENSE_PROJ_ROWS if mix_args is None else DENSE_ROWS if proj_args is not None else DENSE_FINAL_ROWS
    token_rows = lambda w: pl.BlockSpec((tm, w), lambda r: (r, 0))
    mix, proj = mix_args is not None, proj_args is not None
    args, in_specs = [x], [token_rows(d_model)]
    d_ff = 0
    if mix:
        y_s5, y_gla, w_out, norm_ffn, w_fi, w_fo = mix_args
        d_ff = w_fo.shape[0]
        assert d_ff % MXU_DIM == 0
        args += [y_s5, y_gla]
        in_specs += [token_rows(d_s5), token_rows(y_gla.shape[1])]
        resident = [w_out, norm_ffn, w_fi, w_fo, norm]
    else:
        resident = [norm]
    if proj:
        w_in, b_gate = proj_args
        la_w = b_gate.shape[1]
        zg_w = w_in.shape[1] - d_s5 - la_w
        resident += [w_in, b_gate]
        out_specs = [token_rows(d_model), token_rows(d_s5), token_rows(zg_w), token_rows(la_w)]
        out_shape = [jax.ShapeDtypeStruct((rows_total, d_model), F32),
                     jax.ShapeDtypeStruct((rows_total, d_s5), F32),
                     jax.ShapeDtypeStruct((rows_total, zg_w), BF16),
                     jax.ShapeDtypeStruct((rows_total, la_w), F32)]
        if not mix:
            out_specs, out_shape = out_specs[1:], out_shape[1:]
    else:
        out_specs = token_rows(d_model)
        out_shape = jax.ShapeDtypeStruct((rows_total, d_model), F32)
    for operand, spec in map(_resident, resident):
        args.append(operand)
        in_specs.append(spec)
    return pl.pallas_call(
        functools.partial(_dense_kernel, mix=mix, proj=proj, d_s5=d_s5, d_ff=d_ff),
        grid=(rows_total // tm,),
        in_specs=in_specs,
        out_specs=out_specs,
        out_shape=out_shape,
        compiler_params=pltpu.CompilerParams(dimension_semantics=("arbitrary",),
                                             vmem_limit_bytes=VMEM_LIMIT_BYTES),
        name="dense_" + ("mix" if mix else "") + ("proj" if proj else "final"),
    )(*args)


def kernel(x, norm_mix, w_in, s5_lam_re, s5_lam_im, s5_b_re, s5_b_im, s5_c_re, s5_c_im, s5_d, s5_log_step,
           s5_w_glu, s5_b_glu, s5_out_norm, gla_w_gate, gla_b_gate, gla_out_norm, w_out, norm_ffn, w_ffn_in,
           w_ffn_out, norm_final):
    batch, seq, d_model = x.shape
    depth = w_in.shape[0]
    d_s5 = s5_d.shape[1]
    d_gla = gla_out_norm.shape[1]
    kdim = gla_w_gate.shape[2]
    o_gate = d_s5 + 2 * kdim + 2 * d_gla
    assert (batch * seq) % DENSE_PROJ_ROWS == 0 and DENSE_PROJ_ROWS % DENSE_FINAL_ROWS == 0 and DENSE_FINAL_ROWS % DENSE_ROWS == 0 and seq % GLA_CHUNK == 0 and seq % (S5_CHUNK * S5_CHUNKS) == 0

    s5_m, s5_e, s5_f, s5_apow = _s5_prep(s5_lam_re, s5_lam_im, s5_log_step, s5_b_re, s5_b_im, s5_c_re, s5_c_im)
    w_eff = _gate_fold(w_in[:, :, o_gate:], gla_w_gate)
    w_in_eff = jnp.concatenate([w_in[:, :, :o_gate].astype(BF16), w_eff], axis=-1)
    w_out_b, w_fi_b, w_fo_b, w_glu_b = (w.astype(BF16) for w in (w_out, w_ffn_in, w_ffn_out, s5_w_glu))
    layer = _Layer
    vec = lambda v, i: _Layer(v[:, None, :], i)

    xf = x.reshape(batch * seq, d_model)
    u, zg, la = _dense(xf, None, vec(norm_mix, 0), (layer(w_in_eff, 0), vec(gla_b_gate, 0)), d_s5=d_s5)
    for i in range(depth):
        y_s5 = _s5_mixer(u.reshape(batch, seq, d_s5), layer(s5_m, i), layer(s5_e, i), layer(s5_f, i),
                         layer(s5_apow, i), vec(s5_d, i), layer(w_glu_b, i), vec(s5_b_glu, i), vec(s5_out_norm, i))
        y_gla = _gla_mixer(zg, la, vec(gla_out_norm, i), batch=batch, d_gla=d_gla)
        mix_args = (y_s5.reshape(batch * seq, d_s5), y_gla, layer(w_out_b, i), vec(norm_ffn, i),
                    layer(w_fi_b, i), layer(w_fo_b, i))
        if i + 1 < depth:
            xf, u, zg, la = _dense(xf, mix_args, vec(norm_mix, i + 1),
                                   (layer(w_in_eff, i + 1), vec(gla_b_gate, i + 1)), d_s5=d_s5)
        else:
            xf = _dense(xf, mix_args, norm_final[None, :], None, d_s5=d_s5)
    return xf.reshape(batch, seq, d_model)
```

```python
import functools
from typing import NamedTuple

import jax
import jax.numpy as jnp
from jax import lax
from jax.experimental import pallas as pl
from jax.experimental.pallas import tpu as pltpu

S5_GROUP = 16
GLA_HEADS = 4
GLA_CHUNK = 64
GLA_GATE_NORM = 16.0
EPS = 1e-6

LANES = 128
MXU_DIM = 256
VMEM_LIMIT_BYTES = 56 * 1024 * 1024

DENSE_ROWS = 512
DENSE_PROJ_ROWS = 2048
DENSE_FINAL_ROWS = 1024
DENSE_GROUP = 256
FFN_CHUNKS = 2
S5_CHUNK = 16
S5_CHUNKS = 8
S5_FINISH_ROWS = 512
S5_PREP_GROUPS = 8
GLA_ROWS = 2048

BF16 = jnp.bfloat16
F32 = jnp.float32


def _dot(a, b):
    return jnp.dot(a, b, preferred_element_type=F32)


def _dot_nt(a, b):
    return lax.dot_general(a, b, (((1,), (1,)), ((), ())), preferred_element_type=F32)


def _dot_tn(a, b):
    return lax.dot_general(a, b, (((0,), (0,)), ((), ())), preferred_element_type=F32)


def _rms(x, gain):
    return x * lax.rsqrt(jnp.mean(x * x, axis=-1, keepdims=True) + EPS) * gain


class _Layer(NamedTuple):
    stacked: jax.Array
    index: int

    @property
    def shape(self):
        return self.stacked.shape[1:]


def _resident(x):
    if isinstance(x, _Layer):
        lead, shape, operand = (x.index,), x.shape, x.stacked
        block = (None,) + shape
    else:
        lead, shape, operand = (), x.shape, x
        block = shape
    return operand, pl.BlockSpec(block, lambda *_: lead + (0,) * len(shape), pipeline_mode=pl.Buffered(1))


def _dot_split(a, b):
    a_hi, b_hi = a.astype(BF16), b.astype(BF16)
    a_lo = (a - a_hi.astype(F32)).astype(BF16)
    b_lo = (b - b_hi.astype(F32)).astype(BF16)
    return _dot(a_hi, b_hi) + _dot(a_hi, b_lo) + _dot(a_lo, b_hi)


def _gate_fold_kernel(win_gate_ref, w_gate_ref, weff_ref):
    weff_ref[0] = _dot_split(win_gate_ref[0], w_gate_ref[0]).astype(BF16)


def _gate_fold(win_gate, w_gate):
    depth, d_model, rank = win_gate.shape
    kdim = w_gate.shape[2]
    per_layer = lambda *s: pl.BlockSpec((1,) + s, lambda i: (i,) + (0,) * len(s))
    return pl.pallas_call(
        _gate_fold_kernel,
        grid=(depth,),
        in_specs=[per_layer(d_model, rank), per_layer(rank, kdim)],
        out_specs=per_layer(d_model, kdim),
        out_shape=jax.ShapeDtypeStruct((depth, d_model, kdim), BF16),
        compiler_params=pltpu.CompilerParams(dimension_semantics=("arbitrary",)),
        name="gate_fold",
    )(win_gate, w_gate)


def _cmul(ar, ai, br, bi):
    return ar * br - ai * bi, ar * bi + ai * br


def _discretise(lr, li, log_step):
    step = jnp.exp(log_step)
    mag = jnp.exp(lr * step)
    ar = mag * jnp.cos(li * step)
    ai = mag * jnp.sin(li * step)
    den = lr * lr + li * li
    fr = ((ar - 1.0) * lr + ai * li) / den
    fi = (ai * lr - (ar - 1.0) * li) / den
    return ar, ai, fr, fi


def _s5_prep_kernel(lam_re_row_ref, lam_im_row_ref, lam_re_col_ref, lam_im_col_ref, log_step_ref,
                    c_re_ref, c_im_ref, b_re_ref, b_im_ref, m_ref, e_ref, f_ref, apow_ref):
    t_c = S5_CHUNK
    h_c = S5_GROUP
    p_c = lam_re_row_ref.shape[-1]

    def one_group(gi, _):
        log_step = log_step_ref[0, gi]
        ar, ai, _, _ = _discretise(lam_re_row_ref[0, gi], lam_im_row_ref[0, gi], log_step)
        arc, aic, frc, fic = _discretise(lam_re_col_ref[0, gi], lam_im_col_ref[0, gi], log_step)

        pr = jnp.ones((h_c, p_c), F32)
        pi = jnp.zeros((h_c, p_c), F32)
        blocks_r, blocks_i = [], []
        for _ in range(t_c):
            blocks_r.append(pr)
            blocks_i.append(pi)
            pr, pi = _cmul(pr, pi, ar, ai)
        c_re = jnp.concatenate([c_re_ref[0, gi]] * t_c, axis=0)
        c_im = jnp.concatenate([c_im_ref[0, gi]] * t_c, axis=0)
        l_re, l_im = _cmul(c_re, c_im, jnp.concatenate(blocks_r, axis=0),
                           jnp.concatenate(blocks_i, axis=0))
        f_re, f_im = _cmul(l_re, l_im, ar, ai)
        f_ref[0, gi] = jnp.concatenate([f_re, -f_im], axis=1).astype(BF16)

        rep = (lax.broadcasted_iota(jnp.int32, (h_c, t_c * h_c), 0)
               == lax.broadcasted_iota(jnp.int32, (h_c, t_c * h_c), 1) % h_c).astype(F32)
        bb_re, bb_im = _cmul(frc, fic, b_re_ref[0, gi], b_im_ref[0, gi])
        bw_re, bw_im = _dot_split(bb_re, rep), _dot_split(bb_im, rep)
        k_wide = _dot_split(l_re, bw_re) - _dot_split(l_im, bw_im)
        per_tile = LANES // h_c
        lane_blk = lax.broadcasted_iota(jnp.int32, (t_c * h_c, LANES), 1) // h_c
        for lt in range(t_c // per_tile):
            k_tile = k_wide[:, lt * LANES:(lt + 1) * LANES]
            m = jnp.zeros((t_c * h_c, LANES), F32)
            for jj in range(per_tile):
                j = lt * per_tile + jj
                moved = k_tile if j == 0 else jnp.concatenate(
                    [jnp.zeros((j * h_c, LANES), F32), k_tile[:(t_c - j) * h_c]], axis=0)
                m = jnp.where(lane_blk == jj, moved, m)
            m_ref[0, gi, :, lt * LANES:(lt + 1) * LANES] = m.astype(BF16)

        lane_blk = lax.broadcasted_iota(jnp.int32, (p_c, LANES), 1) // h_c
        qr = jnp.ones((p_c, 1), F32)
        qi = jnp.zeros((p_c, 1), F32)
        for lt in range(t_c // per_tile - 1, -1, -1):
            pe_r = jnp.zeros((p_c, LANES), F32)
            pe_i = jnp.zeros((p_c, LANES), F32)
            for jj in range(per_tile - 1, -1, -1):
                pe_r = jnp.where(lane_blk == jj, qr, pe_r)
                pe_i = jnp.where(lane_blk == jj, qi, pe_i)
                qr, qi = _cmul(qr, qi, arc, aic)
            cols = slice(lt * LANES, (lt + 1) * LANES)
            e_re, e_im = _cmul(pe_r, pe_i, bw_re[:, cols], bw_im[:, cols])
            e_ref[0, gi, 0:p_c, cols] = e_re.astype(BF16)
            e_ref[0, gi, p_c:2 * p_c, cols] = e_im.astype(BF16)

        for k in range(3):
            apow_ref[0, gi, k, 0] = jnp.broadcast_to(qr, (p_c, LANES))
            apow_ref[0, gi, k, 1] = jnp.broadcast_to(qi, (p_c, LANES))
            qr, qi = _cmul(qr, qi, qr, qi)
        return 0

    lax.fori_loop(0, m_ref.shape[1], one_group, 0)


def _s5_prep(lam_re, lam_im, log_step, b_re, b_im, c_re, c_im):
    depth, groups, states = lam_re.shape
    t_c, h_c = S5_CHUNK, S5_GROUP
    row = lambda t: t.reshape(depth, groups, 1, states)
    col = lambda t: t.reshape(depth, groups, states, 1)
    blk = lambda *s: pl.BlockSpec((1, S5_PREP_GROUPS) + s, lambda i, g: (i, g) + (0,) * len(s))
    w = t_c * h_c
    return pl.pallas_call(
        _s5_prep_kernel,
        grid=(depth, groups // S5_PREP_GROUPS),
        in_specs=[blk(1, states)] * 2 + [blk(states, 1)] * 2 + [blk(1, 1)]
        + [blk(h_c, states)] * 2 + [blk(states, h_c)] * 2,
        out_specs=[blk(w, w), blk(2 * states, w), blk(w, 2 * states), blk(3, 2, states, LANES)],
        out_shape=[jax.ShapeDtypeStruct((depth, groups, w, w), BF16),
                   jax.ShapeDtypeStruct((depth, groups, 2 * states, w), BF16),
                   jax.ShapeDtypeStruct((depth, groups, w, 2 * states), BF16),
                   jax.ShapeDtypeStruct((depth, groups, 3, 2, states, LANES), F32)],
        compiler_params=pltpu.CompilerParams(dimension_semantics=("arbitrary", "arbitrary")),
        name="s5_prep",
    )(row(lam_re), row(lam_im), col(lam_re), col(lam_im), log_step.reshape(depth, groups, 1, 1),
      c_re, c_im, b_re, b_im)


def _s5_kernel(u_ref, m_ref, e_ref, f_ref, apow_ref, d_ref, wglu_ref, bglu_ref, onorm_ref, y_ref,
               ub_ref, xt_ref, yt_ref, y2_ref, z_ref, s_ref, carry_ref, perm_ref):
    n_b, t_blk, d_s5 = u_ref.shape
    groups = m_ref.shape[0]
    t_c, h_c = S5_CHUNK, S5_GROUP
    n_c = t_blk // t_c
    p_c = f_ref.shape[2] // 2
    samples = n_b * n_c
    assert samples == LANES and n_c == 8

    @pl.when(pl.program_id(1) == 0)
    def _():
        carry_ref[...] = jnp.zeros_like(carry_ref)

    @pl.when((pl.program_id(0) == 0) & (pl.program_id(1) == 0))
    def _():
        dst = lax.broadcasted_iota(jnp.int32, (t_blk, t_blk), 0)
        src = lax.broadcasted_iota(jnp.int32, (t_blk, t_blk), 1)
        perm_ref[0] = jnp.where(src == (dst % n_c) * t_c + dst // n_c, 1.0, 0.0).astype(BF16)
        perm_ref[1] = jnp.where(src == (dst % t_c) * n_c + dst // t_c, 1.0, 0.0).astype(BF16)

    for b in range(n_b):
        ub_ref[b] = _dot(perm_ref[0], u_ref[b].astype(BF16))
    for t in range(t_c):
        xt_ref[t] = ub_ref[:, t * n_c:(t + 1) * n_c, :].reshape(samples, d_s5).T
    for g in range(groups):
        a = xt_ref[:, g * h_c:(g + 1) * h_c, :].reshape(t_c * h_c, samples).astype(BF16)
        yt_ref[g] = _dot(m_ref[g], a)
        z_ref[g] = _dot(e_ref[g], a)

    c_idx = lax.broadcasted_iota(jnp.int32, (p_c, LANES), 1) % n_c
    for g in range(groups):
        z_re, z_im = z_ref[g, 0:p_c], z_ref[g, p_c:2 * p_c]
        v_re = jnp.where(c_idx == 0, carry_ref[g, 0:p_c], pltpu.roll(z_re, 1, 1))
        v_im = jnp.where(c_idx == 0, carry_ref[g, p_c:2 * p_c], pltpu.roll(z_im, 1, 1))
        for k in range(3):
            hop = 1 << k
            sh_re = jnp.where(c_idx >= hop, pltpu.roll(v_re, hop, 1), 0.0)
            sh_im = jnp.where(c_idx >= hop, pltpu.roll(v_im, hop, 1), 0.0)
            d_re, d_im = _cmul(apow_ref[g, k, 0], apow_ref[g, k, 1], sh_re, sh_im)
            v_re, v_im = v_re + d_re, v_im + d_im
        s_ref[g, 0:p_c] = v_re.astype(BF16)
        s_ref[g, p_c:2 * p_c] = v_im.astype(BF16)
        o_re, o_im = _cmul(apow_ref[g, 0, 0], apow_ref[g, 0, 1], v_re, v_im)
        carry_ref[g, 0:p_c] = pltpu.roll(o_re + z_re, LANES - (n_c - 1), 1)
        carry_ref[g, p_c:2 * p_c] = pltpu.roll(o_im + z_im, LANES - (n_c - 1), 1)

    for g in range(groups):
        yt_ref[g] += _dot(f_ref[g], s_ref[g])
    for t in range(t_c):
        y2_ref[t] = yt_ref[:, t * h_c:(t + 1) * h_c, :].reshape(d_s5, samples).T
    for b in range(n_b):
        y = y2_ref[:, b * n_c:(b + 1) * n_c, :].reshape(t_blk, d_s5)
        y_hi = y.astype(BF16)
        y_lo = (y - y_hi.astype(F32)).astype(BF16)
        y2 = _dot(perm_ref[1], jnp.concatenate([y_hi, y_lo], axis=-1))
        ub_ref[b] = y2[:, :d_s5] + y2[:, d_s5:] + d_ref[...] * u_ref[b]
    fin = S5_FINISH_ROWS // t_blk
    for q in range(n_b // fin):
        y = ub_ref[q * fin:(q + 1) * fin].reshape(S5_FINISH_ROWS, d_s5)
        y = jax.nn.gelu(y)
        y = y * jax.nn.sigmoid(_dot(y.astype(BF16), wglu_ref[...]) + bglu_ref[...])
        y_ref[q * fin:(q + 1) * fin] = _rms(y, onorm_ref[...]).reshape(fin, t_blk, d_s5)


def _s5_mixer(u, m, e, f, apow, d_skip, wglu, bglu, onorm):
    batch, seq, d_s5 = u.shape
    groups, w = m.shape[0], m.shape[1]
    states = f.shape[2] // 2
    t_blk = S5_CHUNK * S5_CHUNKS
    n_b = LANES // S5_CHUNKS
    assert batch % n_b == 0 and seq % t_blk == 0
    block = pl.BlockSpec((n_b, t_blk, d_s5), lambda bb, t: (bb, t, 0))
    params, param_specs = zip(*map(_resident, (m, e, f, apow, d_skip, wglu, bglu, onorm)))
    return pl.pallas_call(
        _s5_kernel,
        grid=(batch // n_b, seq // t_blk),
        in_specs=[block, *param_specs],
        out_specs=block,
        out_shape=jax.ShapeDtypeStruct((batch, seq, d_s5), F32),
        scratch_shapes=[pltpu.VMEM((n_b, t_blk, d_s5), F32),
                        pltpu.VMEM((S5_CHUNK, d_s5, LANES), F32),
                        pltpu.VMEM((groups, w, LANES), F32),
                        pltpu.VMEM((S5_CHUNK, LANES, d_s5), F32),
                        pltpu.VMEM((groups, 2 * states, LANES), F32),
                        pltpu.VMEM((groups, 2 * states, LANES), BF16),
                        pltpu.VMEM((groups, 2 * states, LANES), F32),
                        pltpu.VMEM((2, t_blk, t_blk), BF16)],
        compiler_params=pltpu.CompilerParams(dimension_semantics=("arbitrary", "arbitrary"),
                                             vmem_limit_bytes=VMEM_LIMIT_BYTES),
        name="s5_mixer",
    )(u, *params)


def _log_sigmoid(x):
    return jnp.minimum(x, 0.0) - jnp.log1p(jnp.exp(-jnp.abs(x)))


def _gla_kernel(zg_ref, la_ref, onorm_ref, y_ref, st_ref):
    rows = zg_ref.shape[0]
    d_gla = y_ref.shape[1]
    dv = d_gla // GLA_HEADS
    dk = dv // 2
    kdim = GLA_HEADS * dk
    o_k, o_v, o_g = kdim, 2 * kdim, 2 * kdim + d_gla
    c = GLA_CHUNK
    assert 2 * dk == LANES and dv == LANES

    @pl.when(pl.program_id(1) == 0)
    def _():
        st_ref[...] = jnp.zeros_like(st_ref)

    r_i = lax.broadcasted_iota(jnp.int32, (c, 3 * c), 0)
    c_i = lax.broadcasted_iota(jnp.int32, (c, 3 * c), 1)
    tri3 = jnp.where((c_i % c) <= r_i, 1.0, 0.0).astype(BF16)
    causal = (lax.broadcasted_iota(jnp.int32, (c, LANES), 0)
              >= lax.broadcasted_iota(jnp.int32, (c, LANES), 1) % c)
    low_head = lax.broadcasted_iota(jnp.int32, (c, LANES), 1) < dk
    own_block = ((lax.broadcasted_iota(jnp.int32, (2 * dv, LANES), 0) < dv)
                 == (lax.broadcasted_iota(jnp.int32, (2 * dv, LANES), 1) < dk))
    n_chunks = rows // c
    pairs = GLA_HEADS // 2
    rows_of = lambda t, n: t[n * c:(n + 1) * c]
    zeros_v = jnp.zeros((c, dv), BF16)

    la = la_ref[...]
    la_hi = la.astype(BF16)
    rem = la - la_hi.astype(F32)
    la_mid = rem.astype(BF16)
    la_lo = (rem - la_mid.astype(F32)).astype(BF16)
    cums = [_dot(tri3, jnp.concatenate([rows_of(la_hi, n), rows_of(la_mid, n), rows_of(la_lo, n)], axis=0))
            for n in range(n_chunks)]
    cum = jnp.concatenate(cums, axis=0)
    totals = [t[c - 1:c, :] for t in cums]
    total = jnp.concatenate([jnp.broadcast_to(t, (c, kdim)) for t in totals], axis=0)
    q_dec = (zg_ref[:, 0:kdim].astype(F32) * (dk ** -0.5) * jnp.exp(cum)).astype(BF16)
    k = zg_ref[:, o_k:o_k + kdim].astype(F32)
    k_inv = k * jnp.exp(-cum)
    k_end = (k * jnp.exp(total - cum)).astype(BF16)
    v = zg_ref[:, o_v:o_v + d_gla]

    def lanes_of(t, n, pair):
        return t[n * c:(n + 1) * c, pair * LANES:(pair + 1) * LANES]

    scores = []
    for n in range(n_chunks):
        for pair in range(pairs):
            kp = lanes_of(k_inv, n, pair)
            keys = jnp.concatenate([jnp.where(low_head, kp, 0.0), jnp.where(low_head, 0.0, kp)], axis=0)
            s = _dot_nt(lanes_of(q_dec, n, pair), keys.astype(BF16))
            scores.append(jnp.where(causal, s, 0.0).astype(BF16))

    intra, upds = [], []
    for n in range(n_chunks):
        for pair in range(pairs):
            v_even = v[n * c:(n + 1) * c, (2 * pair) * dv:(2 * pair + 1) * dv]
            v_odd = v[n * c:(n + 1) * c, (2 * pair + 1) * dv:(2 * pair + 2) * dv]
            v_diag = jnp.concatenate([jnp.concatenate([v_even, zeros_v], axis=1),
                                      jnp.concatenate([zeros_v, v_odd], axis=1)], axis=0)
            intra.append(_dot(scores[n * pairs + pair], v_diag))
            upd = _dot_tn(jnp.concatenate([v_even, v_odd], axis=1), lanes_of(k_end, n, pair))
            upds.append(jnp.where(own_block, upd, 0.0))

    outs = [[None] * pairs for _ in range(n_chunks)]
    for pair in range(pairs):
        st = st_ref[pair]
        for n in range(n_chunks):
            outs[n][pair] = intra[n * pairs + pair] + _dot_nt(lanes_of(q_dec, n, pair), st.astype(BF16))
            decay = jnp.exp(totals[n][:, pair * LANES:(pair + 1) * LANES])
            st = st * decay + upds[n * pairs + pair]
        st_ref[pair] = st

    for h in range(GLA_HEADS):
        o = jnp.concatenate([outs[n][h // 2][:, (h % 2) * dv:(h % 2 + 1) * dv] for n in range(n_chunks)], axis=0)
        o = o * lax.rsqrt(jnp.mean(o * o, axis=-1, keepdims=True) + EPS)
        g = zg_ref[:, o_g + h * dv:o_g + (h + 1) * dv].astype(F32)
        y = jax.nn.silu(g) * (o * onorm_ref[:, h * dv:(h + 1) * dv])
        y_ref[:, h * dv:(h + 1) * dv] = y.astype(y_ref.dtype)


def _gla_mixer(zg, la, onorm, *, batch, d_gla):
    rows_total = zg.shape[0]
    seq = rows_total // batch
    rows = min(GLA_ROWS, seq)
    n_t = seq // rows
    token_rows = lambda w: pl.BlockSpec((rows, w), lambda b, t: (b * n_t + t, 0))
    onorm, onorm_spec = _resident(onorm)
    return pl.pallas_call(
        _gla_kernel,
        grid=(batch, n_t),
        in_specs=[token_rows(zg.shape[1]), token_rows(la.shape[1]), onorm_spec],
        out_specs=token_rows(d_gla),
        out_shape=jax.ShapeDtypeStruct((rows_total, d_gla), BF16),
        scratch_shapes=[pltpu.VMEM((GLA_HEADS // 2, 2 * d_gla // GLA_HEADS, LANES), F32)],
        compiler_params=pltpu.CompilerParams(dimension_semantics=("arbitrary", "arbitrary"),
                                             vmem_limit_bytes=VMEM_LIMIT_BYTES),
        name="gla_mixer",
    )(zg, la, onorm)


def _dense_kernel(*refs, mix, proj, d_s5, d_ff):
    refs = list(refs)
    x_ref = refs.pop(0)
    n_groups = x_ref.shape[0] // DENSE_GROUP
    groups = [pl.ds(s * DENSE_GROUP, DENSE_GROUP) for s in range(n_groups)]
    xs = [x_ref[r, :] for r in groups]
    if mix:
        ys5_ref, ygla_ref, wout_ref, nffn_ref, wfi_ref, wfo_ref = refs[:6]
        refs = refs[6:]
        xs = [x + _dot(ys5_ref[r, :].astype(BF16), wout_ref[0:d_s5, :]) + _dot(ygla_ref[r, :], wout_ref[d_s5:, :])
              for x, r in zip(xs, groups)]
        hs = [_rms(x, nffn_ref[...]).astype(BF16) for x in xs]
        tiles = d_ff // MXU_DIM
        bounds = [MXU_DIM * ((tiles * c + FFN_CHUNKS - 1) // FFN_CHUNKS) for c in range(FFN_CHUNKS + 1)]
        acts = [[] for _ in range(n_groups)]
        for lo, hi in zip(bounds[:-1], bounds[1:]):
            for s in range(n_groups):
                gate = _dot(hs[s], wfi_ref[:, lo:hi])
                up = _dot(hs[s], wfi_ref[:, d_ff + lo:d_ff + hi])
                acts[s].append((jax.nn.silu(gate) * up).astype(BF16))
        xs = [x + _dot(jnp.concatenate(a, axis=1), wfo_ref[...]) for x, a in zip(xs, acts)]
    norm_ref = refs.pop(0)
    hs = [_rms(x, norm_ref[...]) for x in xs]
    if not proj:
        (out_ref,) = refs
        for h, r in zip(hs, groups):
            out_ref[r, :] = h
        return
    win_ref, bgate_ref = refs[:2]
    u_ref, zg_ref, la_ref = refs[-3:]
    o_a = d_s5 + zg_ref.shape[1]
    for x, h, r in zip(xs, hs, groups):
        if mix:
            refs[2][r, :] = x
        z = _dot(h.astype(BF16), win_ref[...])
        u_ref[r, :] = z[:, :d_s5]
        zg_ref[r, :] = z[:, d_s5:o_a].astype(BF16)
        la_ref[r, :] = _log_sigmoid(z[:, o_a:] + bgate_ref[...]) * (1.0 / GLA_GATE_NORM)


def _dense(x, mix_args, norm, proj_args, *, d_s5):
    rows_total, d_model = x.shape
    tm = DENSE_PROJ_ROWS if mix_args is None else DENSE_ROWS if proj_args is not None else DENSE_FINAL_ROWS
    token_rows = lambda w: pl.BlockSpec((tm, w), lambda r: (r, 0))
    mix, proj = mix_args is not None, proj_args is not None
    args, in_specs = [x], [token_rows(d_model)]
    d_ff = 0
    if mix:
        y_s5, y_gla, w_out, norm_ffn, w_fi, w_fo = mix_args
        d_ff = w_fo.shape[0]
        assert d_ff % MXU_DIM == 0
        args += [y_s5, y_gla]
        in_specs += [token_rows(d_s5), token_rows(y_gla.shape[1])]
        resident = [w_out, norm_ffn, w_fi, w_fo, norm]
    else:
        resident = [norm]
    if proj:
        w_in, b_gate = proj_args
        la_w = b_gate.shape[1]
        zg_w = w_in.shape[1] - d_s5 - la_w
        resident += [w_in, b_gate]
        out_specs = [token_rows(d_model), token_rows(d_s5), token_rows(zg_w), token_rows(la_w)]
        out_shape = [jax.ShapeDtypeStruct((rows_total, d_model), F32),
                     jax.ShapeDtypeStruct((rows_total, d_s5), F32),
                     jax.ShapeDtypeStruct((rows_total, zg_w), BF16),
                     jax.ShapeDtypeStruct((rows_total, la_w), F32)]
        if not mix:
            out_specs, out_shape = out_specs[1:], out_shape[1:]
    else:
        out_specs = token_rows(d_model)
        out_shape = jax.ShapeDtypeStruct((rows_total, d_model), F32)
    for operand, spec in map(_resident, resident):
        args.append(operand)
        in_specs.append(spec)
    return pl.pallas_call(
        functools.partial(_dense_kernel, mix=mix, proj=proj, d_s5=d_s5, d_ff=d_ff),
        grid=(rows_total // tm,),
        in_specs=in_specs,
        out_specs=out_specs,
        out_shape=out_shape,
        compiler_params=pltpu.CompilerParams(dimension_semantics=("arbitrary",),
                                             vmem_limit_bytes=VMEM_LIMIT_BYTES),
        name="dense_" + ("mix" if mix else "") + ("proj" if proj else "final"),
    )(*args)


def kernel(x, norm_mix, w_in, s5_lam_re, s5_lam_im, s5_b_re, s5_b_im, s5_c_re, s5_c_im, s5_d, s5_log_step,
           s5_w_glu, s5_b_glu, s5_out_norm, gla_w_gate, gla_b_gate, gla_out_norm, w_out, norm_ffn, w_ffn_in,
           w_ffn_out, norm_final):
    batch, seq, d_model = x.shape
    depth = w_in.shape[0]
    d_s5 = s5_d.shape[1]
    d_gla = gla_out_norm.shape[1]
    kdim = gla_w_gate.shape[2]
    o_gate = d_s5 + 2 * kdim + 2 * d_gla
    assert (batch * seq) % DENSE_PROJ_ROWS == 0 and DENSE_PROJ_ROWS % DENSE_FINAL_ROWS == 0 and DENSE_FINAL_ROWS % DENSE_ROWS == 0 and seq % GLA_CHUNK == 0 and seq % (S5_CHUNK * S5_CHUNKS) == 0

    s5_m, s5_e, s5_f, s5_apow = _s5_prep(s5_lam_re, s5_lam_im, s5_log_step, s5_b_re, s5_b_im, s5_c_re, s5_c_im)
    w_eff = _gate_fold(w_in[:, :, o_gate:], gla_w_gate)
    w_in_eff = jnp.concatenate([w_in[:, :, :o_gate].astype(BF16), w_eff], axis=-1)
    w_out_b, w_fi_b, w_fo_b, w_glu_b = (w.astype(BF16) for w in (w_out, w_ffn_in, w_ffn_out, s5_w_glu))
    layer = _Layer
    vec = lambda v, i: _Layer(v[:, None, :], i)

    xf = x.reshape(batch * seq, d_model)
    u, zg, la = _dense(xf, None, vec(norm_mix, 0), (layer(w_in_eff, 0), vec(gla_b_gate, 0)), d_s5=d_s5)
    for i in range(depth):
        y_s5 = _s5_mixer(u.reshape(batch, seq, d_s5), layer(s5_m, i), layer(s5_e, i), layer(s5_f, i),
                         layer(s5_apow, i), vec(s5_d, i), layer(w_glu_b, i), vec(s5_b_glu, i), vec(s5_out_norm, i))
        y_gla = _gla_mixer(zg, la, vec(gla_out_norm, i), batch=batch, d_gla=d_gla)
        mix_args = (y_s5.reshape(batch * seq, d_s5), y_gla, layer(w_out_b, i), vec(norm_ffn, i),
                    layer(w_fi_b, i), layer(w_fo_b, i))
        if i + 1 < depth:
            xf, u, zg, la = _dense(xf, mix_args, vec(norm_mix, i + 1),
                                   (layer(w_in_eff, i + 1), vec(gla_b_gate, i + 1)), d_s5=d_s5)
        else:
            xf = _dense(xf, mix_args, norm_final[None, :], None, d_s5=d_s5)
    return xf.reshape(batch, seq, d_model)
```

```python
import functools
from typing import NamedTuple

import jax
import jax.numpy as jnp
from jax import lax
from jax.experimental import pallas as pl
from jax.experimental.pallas import tpu as pltpu

S5_GROUP = 16
GLA_HEADS = 4
GLA_CHUNK = 64
GLA_GATE_NORM = 16.0
EPS = 1e-6

LANES = 128
MXU_DIM = 256
VMEM_LIMIT_BYTES = 56 * 1024 * 1024

DENSE_ROWS = 512
DENSE_PROJ_ROWS = 2048
DENSE_FINAL_ROWS = 1024
DENSE_GROUP = 256
FFN_CHUNKS = 2
S5_CHUNK = 16
S5_CHUNKS = 8
S5_FINISH_ROWS = 512
S5_PREP_GROUPS = 8
GLA_ROWS = 2048

BF16 = jnp.bfloat16
F32 = jnp.float32


def _dot(a, b):
    return jnp.dot(a, b, preferred_element_type=F32)


def _dot_nt(a, b):
    return lax.dot_general(a, b, (((1,), (1,)), ((), ())), preferred_element_type=F32)


def _dot_tn(a, b):
    return lax.dot_general(a, b, (((0,), (0,)), ((), ())), preferred_element_type=F32)


def _rms(x, gain):
    return x * lax.rsqrt(jnp.mean(x * x, axis=-1, keepdims=True) + EPS) * gain


class _Layer(NamedTuple):
    stacked: jax.Array
    index: int

    @property
    def shape(self):
        return self.stacked.shape[1:]


def _resident(x):
    if isinstance(x, _Layer):
        lead, shape, operand = (x.index,), x.shape, x.stacked
        block = (None,) + shape
    else:
        lead, shape, operand = (), x.shape, x
        block = shape
    return operand, pl.BlockSpec(block, lambda *_: lead + (0,) * len(shape), pipeline_mode=pl.Buffered(1))


def _dot_split(a, b):
    a_hi, b_hi = a.astype(BF16), b.astype(BF16)
    a_lo = (a - a_hi.astype(F32)).astype(BF16)
    b_lo = (b - b_hi.astype(F32)).astype(BF16)
    return _dot(a_hi, b_hi) + _dot(a_hi, b_lo) + _dot(a_lo, b_hi)


def _gate_fold_kernel(win_ref, w_gate_ref, weff_ref):
    o_gate = weff_ref.shape[2] - w_gate_ref.shape[2]
    w = win_ref[0]
    weff_ref[0, :, :o_gate] = w[:, :o_gate].astype(BF16)
    weff_ref[0, :, o_gate:] = _dot_split(w[:, o_gate:], w_gate_ref[0]).astype(BF16)


def _gate_fold(w_in, w_gate):
    depth, d_model, d_in = w_in.shape
    rank, kdim = w_gate.shape[1:]
    per_layer = lambda *s: pl.BlockSpec((1,) + s, lambda i: (i,) + (0,) * len(s))
    return pl.pallas_call(
        _gate_fold_kernel,
        grid=(depth,),
        in_specs=[per_layer(d_model, d_in), per_layer(rank, kdim)],
        out_specs=per_layer(d_model, d_in - rank + kdim),
        out_shape=jax.ShapeDtypeStruct((depth, d_model, d_in - rank + kdim), BF16),
        compiler_params=pltpu.CompilerParams(dimension_semantics=("arbitrary",), vmem_limit_bytes=VMEM_LIMIT_BYTES),
        name="gate_fold",
    )(w_in, w_gate)


def _cmul(ar, ai, br, bi):
    return ar * br - ai * bi, ar * bi + ai * br


def _discretise(lr, li, log_step):
    step = jnp.exp(log_step)
    mag = jnp.exp(lr * step)
    ar = mag * jnp.cos(li * step)
    ai = mag * jnp.sin(li * step)
    den = lr * lr + li * li
    fr = ((ar - 1.0) * lr + ai * li) / den
    fi = (ai * lr - (ar - 1.0) * li) / den
    return ar, ai, fr, fi


def _s5_prep_kernel(lam_re_row_ref, lam_im_row_ref, lam_re_col_ref, lam_im_col_ref, log_step_ref,
                    c_re_ref, c_im_ref, b_re_ref, b_im_ref, m_ref, e_ref, f_ref, apow_ref):
    t_c = S5_CHUNK
    h_c = S5_GROUP
    p_c = lam_re_row_ref.shape[-1]

    def one_group(gi, _):
        log_step = log_step_ref[0, gi]
        ar, ai, _, _ = _discretise(lam_re_row_ref[0, gi], lam_im_row_ref[0, gi], log_step)
        arc, aic, frc, fic = _discretise(lam_re_col_ref[0, gi], lam_im_col_ref[0, gi], log_step)

        pr = jnp.ones((h_c, p_c), F32)
        pi = jnp.zeros((h_c, p_c), F32)
        blocks_r, blocks_i = [], []
        for _ in range(t_c):
            blocks_r.append(pr)
            blocks_i.append(pi)
            pr, pi = _cmul(pr, pi, ar, ai)
        c_re = jnp.concatenate([c_re_ref[0, gi]] * t_c, axis=0)
        c_im = jnp.concatenate([c_im_ref[0, gi]] * t_c, axis=0)
        l_re, l_im = _cmul(c_re, c_im, jnp.concatenate(blocks_r, axis=0),
                           jnp.concatenate(blocks_i, axis=0))
        f_re, f_im = _cmul(l_re, l_im, ar, ai)
        f_ref[0, gi] = jnp.concatenate([f_re, -f_im], axis=1).astype(BF16)

        rep = (lax.broadcasted_iota(jnp.int32, (h_c, t_c * h_c), 0)
               == lax.broadcasted_iota(jnp.int32, (h_c, t_c * h_c), 1) % h_c).astype(F32)
        bb_re, bb_im = _cmul(frc, fic, b_re_ref[0, gi], b_im_ref[0, gi])
        bw_re, bw_im = _dot_split(bb_re, rep), _dot_split(bb_im, rep)
        k_wide = _dot_split(l_re, bw_re) - _dot_split(l_im, bw_im)
        per_tile = LANES // h_c
        lane_blk = lax.broadcasted_iota(jnp.int32, (t_c * h_c, LANES), 1) // h_c
        for lt in range(t_c // per_tile):
            k_tile = k_wide[:, lt * LANES:(lt + 1) * LANES]
            m = jnp.zeros((t_c * h_c, LANES), F32)
            for jj in range(per_tile):
                j = lt * per_tile + jj
                moved = k_tile if j == 0 else jnp.concatenate(
                    [jnp.zeros((j * h_c, LANES), F32), k_tile[:(t_c - j) * h_c]], axis=0)
                m = jnp.where(lane_blk == jj, moved, m)
            m_ref[0, gi, :, lt * LANES:(lt + 1) * LANES] = m.astype(BF16)

        lane_blk = lax.broadcasted_iota(jnp.int32, (p_c, LANES), 1) // h_c
        qr = jnp.ones((p_c, 1), F32)
        qi = jnp.zeros((p_c, 1), F32)
        for lt in range(t_c // per_tile - 1, -1, -1):
            pe_r = jnp.zeros((p_c, LANES), F32)
            pe_i = jnp.zeros((p_c, LANES), F32)
            for jj in range(per_tile - 1, -1, -1):
                pe_r = jnp.where(lane_blk == jj, qr, pe_r)
                pe_i = jnp.where(lane_blk == jj, qi, pe_i)
                qr, qi = _cmul(qr, qi, arc, aic)
            cols = slice(lt * LANES, (lt + 1) * LANES)
            e_re, e_im = _cmul(pe_r, pe_i, bw_re[:, cols], bw_im[:, cols])
            e_ref[0, gi, 0:p_c, cols] = e_re.astype(BF16)
            e_ref[0, gi, p_c:2 * p_c, cols] = e_im.astype(BF16)

        for k in range(3):
            apow_ref[0, gi, k, 0] = jnp.broadcast_to(qr, (p_c, LANES))
            apow_ref[0, gi, k, 1] = jnp.broadcast_to(qi, (p_c, LANES))
            qr, qi = _cmul(qr, qi, qr, qi)
        return 0

    lax.fori_loop(0, m_ref.shape[1], one_group, 0)


def _s5_prep(lam_re, lam_im, log_step, b_re, b_im, c_re, c_im):
    depth, groups, states = lam_re.shape
    t_c, h_c = S5_CHUNK, S5_GROUP
    row = lambda t: t.reshape(depth, groups, 1, states)
    col = lambda t: t.reshape(depth, groups, states, 1)
    blk = lambda *s: pl.BlockSpec((1, S5_PREP_GROUPS) + s, lambda i, g: (i, g) + (0,) * len(s))
    w = t_c * h_c
    return pl.pallas_call(
        _s5_prep_kernel,
        grid=(depth, groups // S5_PREP_GROUPS),
        in_specs=[blk(1, states)] * 2 + [blk(states, 1)] * 2 + [blk(1, 1)]
        + [blk(h_c, states)] * 2 + [blk(states, h_c)] * 2,
        out_specs=[blk(w, w), blk(2 * states, w), blk(w, 2 * states), blk(3, 2, states, LANES)],
        out_shape=[jax.ShapeDtypeStruct((depth, groups, w, w), BF16),
                   jax.ShapeDtypeStruct((depth, groups, 2 * states, w), BF16),
                   jax.ShapeDtypeStruct((depth, groups, w, 2 * states), BF16),
                   jax.ShapeDtypeStruct((depth, groups, 3, 2, states, LANES), F32)],
        compiler_params=pltpu.CompilerParams(dimension_semantics=("arbitrary", "arbitrary")),
        name="s5_prep",
    )(row(lam_re), row(lam_im), col(lam_re), col(lam_im), log_step.reshape(depth, groups, 1, 1),
      c_re, c_im, b_re, b_im)


def _s5_kernel(u_ref, m_ref, e_ref, f_ref, apow_ref, d_ref, wglu_ref, bglu_ref, onorm_ref, y_ref,
               ub_ref, xt_ref, yt_ref, y2_ref, z_ref, s_ref, carry_ref, perm_ref):
    n_b, t_blk, d_s5 = u_ref.shape
    groups = m_ref.shape[0]
    t_c, h_c = S5_CHUNK, S5_GROUP
    n_c = t_blk // t_c
    p_c = f_ref.shape[2] // 2
    samples = n_b * n_c
    assert samples == LANES and n_c == 8

    @pl.when(pl.program_id(1) == 0)
    def _():
        carry_ref[...] = jnp.zeros_like(carry_ref)

    @pl.when((pl.program_id(0) == 0) & (pl.program_id(1) == 0))
    def _():
        dst = lax.broadcasted_iota(jnp.int32, (t_blk, t_blk), 0)
        src = lax.broadcasted_iota(jnp.int32, (t_blk, t_blk), 1)
        perm_ref[0] = jnp.where(src == (dst % n_c) * t_c + dst // n_c, 1.0, 0.0).astype(BF16)
        perm_ref[1] = jnp.where(src == (dst % t_c) * n_c + dst // t_c, 1.0, 0.0).astype(BF16)

    for b in range(n_b):
        ub_ref[b] = _dot(perm_ref[0], u_ref[b].astype(BF16))
    for t in range(t_c):
        xt_ref[t] = ub_ref[:, t * n_c:(t + 1) * n_c, :].reshape(samples, d_s5).T
    for g in range(groups):
        a = xt_ref[:, g * h_c:(g + 1) * h_c, :].reshape(t_c * h_c, samples).astype(BF16)
        yt_ref[g] = _dot(m_ref[g], a)
        z_ref[g] = _dot(e_ref[g], a)

    c_idx = lax.broadcasted_iota(jnp.int32, (p_c, LANES), 1) % n_c
    for g in range(groups):
        z_re, z_im = z_ref[g, 0:p_c], z_ref[g, p_c:2 * p_c]
        v_re = jnp.where(c_idx == 0, carry_ref[g, 0:p_c], pltpu.roll(z_re, 1, 1))
        v_im = jnp.where(c_idx == 0, carry_ref[g, p_c:2 * p_c], pltpu.roll(z_im, 1, 1))
        for k in range(3):
            hop = 1 << k
            sh_re = jnp.where(c_idx >= hop, pltpu.roll(v_re, hop, 1), 0.0)
            sh_im = jnp.where(c_idx >= hop, pltpu.roll(v_im, hop, 1), 0.0)
            d_re, d_im = _cmul(apow_ref[g, k, 0], apow_ref[g, k, 1], sh_re, sh_im)
            v_re, v_im = v_re + d_re, v_im + d_im
        s_ref[g, 0:p_c] = v_re.astype(BF16)
        s_ref[g, p_c:2 * p_c] = v_im.astype(BF16)
        o_re, o_im = _cmul(apow_ref[g, 0, 0], apow_ref[g, 0, 1], v_re, v_im)
        carry_ref[g, 0:p_c] = pltpu.roll(o_re + z_re, LANES - (n_c - 1), 1)
        carry_ref[g, p_c:2 * p_c] = pltpu.roll(o_im + z_im, LANES - (n_c - 1), 1)

    for g in range(groups):
        yt_ref[g] += _dot(f_ref[g], s_ref[g])
    for t in range(t_c):
        y2_ref[t] = yt_ref[:, t * h_c:(t + 1) * h_c, :].reshape(d_s5, samples).T
    for b in range(n_b):
        y = y2_ref[:, b * n_c:(b + 1) * n_c, :].reshape(t_blk, d_s5)
        y_hi = y.astype(BF16)
        y_lo = (y - y_hi.astype(F32)).astype(BF16)
        y2 = _dot(perm_ref[1], jnp.concatenate([y_hi, y_lo], axis=-1))
        ub_ref[b] = y2[:, :d_s5] + y2[:, d_s5:] + d_ref[...] * u_ref[b]
    fin = S5_FINISH_ROWS // t_blk
    for q in range(n_b // fin):
        y = ub_ref[q * fin:(q + 1) * fin].reshape(S5_FINISH_ROWS, d_s5)
        y = jax.nn.gelu(y)
        y = y * jax.nn.sigmoid(_dot(y.astype(BF16), wglu_ref[...]) + bglu_ref[...])
        y_ref[q * fin:(q + 1) * fin] = _rms(y, onorm_ref[...]).reshape(fin, t_blk, d_s5)


def _s5_mixer(u, m, e, f, apow, d_skip, wglu, bglu, onorm):
    batch, seq, d_s5 = u.shape
    groups, w = m.shape[0], m.shape[1]
    states = f.shape[2] // 2
    t_blk = S5_CHUNK * S5_CHUNKS
    n_b = LANES // S5_CHUNKS
    assert batch % n_b == 0 and seq % t_blk == 0
    block = pl.BlockSpec((n_b, t_blk, d_s5), lambda bb, t: (bb, t, 0))
    params, param_specs = zip(*map(_resident, (m, e, f, apow, d_skip, wglu, bglu, onorm)))
    return pl.pallas_call(
        _s5_kernel,
        grid=(batch // n_b, seq // t_blk),
        in_specs=[block, *param_specs],
        out_specs=block,
        out_shape=jax.ShapeDtypeStruct((batch, seq, d_s5), F32),
        scratch_shapes=[pltpu.VMEM((n_b, t_blk, d_s5), F32),
                        pltpu.VMEM((S5_CHUNK, d_s5, LANES), F32),
                        pltpu.VMEM((groups, w, LANES), F32),
                        pltpu.VMEM((S5_CHUNK, LANES, d_s5), F32),
                        pltpu.VMEM((groups, 2 * states, LANES), F32),
                        pltpu.VMEM((groups, 2 * states, LANES), BF16),
                        pltpu.VMEM((groups, 2 * states, LANES), F32),
                        pltpu.VMEM((2, t_blk, t_blk), BF16)],
        compiler_params=pltpu.CompilerParams(dimension_semantics=("arbitrary", "arbitrary"),
                                             vmem_limit_bytes=VMEM_LIMIT_BYTES),
        name="s5_mixer",
    )(u, *params)


def _log_sigmoid(x):
    return jnp.minimum(x, 0.0) - jnp.log1p(jnp.exp(-jnp.abs(x)))


def _gla_kernel(zg_ref, la_ref, onorm_ref, y_ref, st_ref):
    rows = zg_ref.shape[0]
    d_gla = y_ref.shape[1]
    dv = d_gla // GLA_HEADS
    dk = dv // 2
    kdim = GLA_HEADS * dk
    o_k, o_v, o_g = kdim, 2 * kdim, 2 * kdim + d_gla
    c = GLA_CHUNK
    assert 2 * dk == LANES and dv == LANES

    @pl.when(pl.program_id(1) == 0)
    def _():
        st_ref[...] = jnp.zeros_like(st_ref)

    r_i = lax.broadcasted_iota(jnp.int32, (c, 3 * c), 0)
    c_i = lax.broadcasted_iota(jnp.int32, (c, 3 * c), 1)
    tri3 = jnp.where((c_i % c) <= r_i, 1.0, 0.0).astype(BF16)
    causal = (lax.broadcasted_iota(jnp.int32, (c, LANES), 0)
              >= lax.broadcasted_iota(jnp.int32, (c, LANES), 1) % c)
    low_head = lax.broadcasted_iota(jnp.int32, (c, LANES), 1) < dk
    own_block = ((lax.broadcasted_iota(jnp.int32, (2 * dv, LANES), 0) < dv)
                 == (lax.broadcasted_iota(jnp.int32, (2 * dv, LANES), 1) < dk))
    n_chunks = rows // c
    pairs = GLA_HEADS // 2
    rows_of = lambda t, n: t[n * c:(n + 1) * c]
    zeros_v = jnp.zeros((c, dv), BF16)

    la = la_ref[...]
    la_hi = la.astype(BF16)
    rem = la - la_hi.astype(F32)
    la_mid = rem.astype(BF16)
    la_lo = (rem - la_mid.astype(F32)).astype(BF16)
    cums = [_dot(tri3, jnp.concatenate([rows_of(la_hi, n), rows_of(la_mid, n), rows_of(la_lo, n)], axis=0))
            for n in range(n_chunks)]
    cum = jnp.concatenate(cums, axis=0)
    totals = [t[c - 1:c, :] for t in cums]
    total = jnp.concatenate([jnp.broadcast_to(t, (c, kdim)) for t in totals], axis=0)
    q_dec = (zg_ref[:, 0:kdim].astype(F32) * (dk ** -0.5) * jnp.exp(cum)).astype(BF16)
    k = zg_ref[:, o_k:o_k + kdim].astype(F32)
    k_inv = k * jnp.exp(-cum)
    k_end = (k * jnp.exp(total - cum)).astype(BF16)
    v = zg_ref[:, o_v:o_v + d_gla]

    def lanes_of(t, n, pair):
        return t[n * c:(n + 1) * c, pair * LANES:(pair + 1) * LANES]

    scores = []
    for n in range(n_chunks):
        for pair in range(pairs):
            kp = lanes_of(k_inv, n, pair)
            keys = jnp.concatenate([jnp.where(low_head, kp, 0.0), jnp.where(low_head, 0.0, kp)], axis=0)
            s = _dot_nt(lanes_of(q_dec, n, pair), keys.astype(BF16))
            scores.append(jnp.where(causal, s, 0.0).astype(BF16))

    intra, upds = [], []
    for n in range(n_chunks):
        for pair in range(pairs):
            v_even = v[n * c:(n + 1) * c, (2 * pair) * dv:(2 * pair + 1) * dv]
            v_odd = v[n * c:(n + 1) * c, (2 * pair + 1) * dv:(2 * pair + 2) * dv]
            v_diag = jnp.concatenate([jnp.concatenate([v_even, zeros_v], axis=1),
                                      jnp.concatenate([zeros_v, v_odd], axis=1)], axis=0)
            intra.append(_dot(scores[n * pairs + pair], v_diag))
            upd = _dot_tn(jnp.concatenate([v_even, v_odd], axis=1), lanes_of(k_end, n, pair))
            upds.append(jnp.where(own_block, upd, 0.0))

    outs = [[None] * pairs for _ in range(n_chunks)]
    for pair in range(pairs):
        st = st_ref[pair]
        for n in range(n_chunks):
            outs[n][pair] = intra[n * pairs + pair] + _dot_nt(lanes_of(q_dec, n, pair), st.astype(BF16))
            decay = jnp.exp(totals[n][:, pair * LANES:(pair + 1) * LANES])
            st = st * decay + upds[n * pairs + pair]
        st_ref[pair] = st

    for h in range(GLA_HEADS):
        o = jnp.concatenate([outs[n][h // 2][:, (h % 2) * dv:(h % 2 + 1) * dv] for n in range(n_chunks)], axis=0)
        o = o * lax.rsqrt(jnp.mean(o * o, axis=-1, keepdims=True) + EPS)
        g = zg_ref[:, o_g + h * dv:o_g + (h + 1) * dv].astype(F32)
        y = jax.nn.silu(g) * (o * onorm_ref[:, h * dv:(h + 1) * dv])
        y_ref[:, h * dv:(h + 1) * dv] = y.astype(y_ref.dtype)


def _gla_mixer(zg, la, onorm, *, batch, d_gla):
    rows_total = zg.shape[0]
    seq = rows_total // batch
    rows = min(GLA_ROWS, seq)
    n_t = seq // rows
    token_rows = lambda w: pl.BlockSpec((rows, w), lambda b, t: (b * n_t + t, 0))
    onorm, onorm_spec = _resident(onorm)
    return pl.pallas_call(
        _gla_kernel,
        grid=(batch, n_t),
        in_specs=[token_rows(zg.shape[1]), token_rows(la.shape[1]), onorm_spec],
        out_specs=token_rows(d_gla),
        out_shape=jax.ShapeDtypeStruct((rows_total, d_gla), BF16),
        scratch_shapes=[pltpu.VMEM((GLA_HEADS // 2, 2 * d_gla // GLA_HEADS, LANES), F32)],
        compiler_params=pltpu.CompilerParams(dimension_semantics=("arbitrary", "arbitrary"),
                                             vmem_limit_bytes=VMEM_LIMIT_BYTES),
        name="gla_mixer",
    )(zg, la, onorm)


def _dense_kernel(*refs, mix, proj, d_s5, d_ff):
    refs = list(refs)
    x_ref = refs.pop(0)
    n_groups = x_ref.shape[0] // DENSE_GROUP
    groups = [pl.ds(s * DENSE_GROUP, DENSE_GROUP) for s in range(n_groups)]
    xs = [x_ref[r, :] for r in groups]
    if mix:
        ys5_ref, ygla_ref, wout_ref, nffn_ref, wfi_ref, wfo_ref = refs[:6]
        refs = refs[6:]
        xs = [x + _dot(ys5_ref[r, :].astype(BF16), wout_ref[0:d_s5, :]) + _dot(ygla_ref[r, :], wout_ref[d_s5:, :])
              for x, r in zip(xs, groups)]
        hs = [_rms(x, nffn_ref[...]).astype(BF16) for x in xs]
        tiles = d_ff // MXU_DIM
        bounds = [MXU_DIM * ((tiles * c + FFN_CHUNKS - 1) // FFN_CHUNKS) for c in range(FFN_CHUNKS + 1)]
        acts = [[] for _ in range(n_groups)]
        for lo, hi in zip(bounds[:-1], bounds[1:]):
            for s in range(n_groups):
                gate = _dot(hs[s], wfi_ref[:, lo:hi])
                up = _dot(hs[s], wfi_ref[:, d_ff + lo:d_ff + hi])
                acts[s].append((jax.nn.silu(gate) * up).astype(BF16))
        xs = [x + _dot(jnp.concatenate(a, axis=1), wfo_ref[...]) for x, a in zip(xs, acts)]
    norm_ref = refs.pop(0)
    hs = [_rms(x, norm_ref[...]) for x in xs]
    if not proj:
        (out_ref,) = refs
        for h, r in zip(hs, groups):
            out_ref[r, :] = h
        return
    win_ref, bgate_ref = refs[:2]
    u_ref, zg_ref, la_ref = refs[-3:]
    o_a = d_s5 + zg_ref.shape[1]
    for x, h, r in zip(xs, hs, groups):
        if mix:
            refs[2][r, :] = x
        z = _dot(h.astype(BF16), win_ref[...])
        u_ref[r, :] = z[:, :d_s5]
        zg_ref[r, :] = z[:, d_s5:o_a].astype(BF16)
        la_ref[r, :] = _log_sigmoid(z[:, o_a:] + bgate_ref[...]) * (1.0 / GLA_GATE_NORM)


def _dense(x, mix_args, norm, proj_args, *, d_s5):
    rows_total, d_model = x.shape
    tm = DENSE_PROJ_ROWS if mix_args is None else DENSE_ROWS if proj_args is not None else DENSE_FINAL_ROWS
    token_rows = lambda w: pl.BlockSpec((tm, w), lambda r: (r, 0))
    mix, proj = mix_args is not None, proj_args is not None
    args, in_specs = [x], [token_rows(d_model)]
    d_ff = 0
    if mix:
        y_s5, y_gla, w_out, norm_ffn, w_fi, w_fo = mix_args
        d_ff = w_fo.shape[0]
        assert d_ff % MXU_DIM == 0
        args += [y_s5, y_gla]
        in_specs += [token_rows(d_s5), token_rows(y_gla.shape[1])]
        resident = [w_out, norm_ffn, w_fi, w_fo, norm]
    else:
        resident = [norm]
    if proj:
        w_in, b_gate = proj_args
        la_w = b_gate.shape[1]
        zg_w = w_in.shape[1] - d_s5 - la_w
        resident += [w_in, b_gate]
        out_specs = [token_rows(d_model), token_rows(d_s5), token_rows(zg_w), token_rows(la_w)]
        out_shape = [jax.ShapeDtypeStruct((rows_total, d_model), F32),
                     jax.ShapeDtypeStruct((rows_total, d_s5), F32),
                     jax.ShapeDtypeStruct((rows_total, zg_w), BF16),
                     jax.ShapeDtypeStruct((rows_total, la_w), F32)]
        if not mix:
            out_specs, out_shape = out_specs[1:], out_shape[1:]
    else:
        out_specs = token_rows(d_model)
        out_shape = jax.ShapeDtypeStruct((rows_total, d_model), F32)
    for operand, spec in map(_resident, resident):
        args.append(operand)
        in_specs.append(spec)
    return pl.pallas_call(
        functools.partial(_dense_kernel, mix=mix, proj=proj, d_s5=d_s5, d_ff=d_ff),
        grid=(rows_total // tm,),
        in_specs=in_specs,
        out_specs=out_specs,
        out_shape=out_shape,
        compiler_params=pltpu.CompilerParams(dimension_semantics=("arbitrary",),
                                             vmem_limit_bytes=VMEM_LIMIT_BYTES),
        name="dense_" + ("mix" if mix else "") + ("proj" if proj else "final"),
    )(*args)


def kernel(x, norm_mix, w_in, s5_lam_re, s5_lam_im, s5_b_re, s5_b_im, s5_c_re, s5_c_im, s5_d, s5_log_step,
           s5_w_glu, s5_b_glu, s5_out_norm, gla_w_gate, gla_b_gate, gla_out_norm, w_out, norm_ffn, w_ffn_in,
           w_ffn_out, norm_final):
    batch, seq, d_model = x.shape
    depth = w_in.shape[0]
    d_s5 = s5_d.shape[1]
    d_gla = gla_out_norm.shape[1]
    assert (batch * seq) % DENSE_PROJ_ROWS == 0 and DENSE_PROJ_ROWS % DENSE_FINAL_ROWS == 0 and DENSE_FINAL_ROWS % DENSE_ROWS == 0 and seq % GLA_CHUNK == 0 and seq % (S5_CHUNK * S5_CHUNKS) == 0

    s5_m, s5_e, s5_f, s5_apow = _s5_prep(s5_lam_re, s5_lam_im, s5_log_step, s5_b_re, s5_b_im, s5_c_re, s5_c_im)
    w_in_eff = _gate_fold(w_in, gla_w_gate)
    w_out_b, w_fi_b, w_fo_b, w_glu_b = (w.astype(BF16) for w in (w_out, w_ffn_in, w_ffn_out, s5_w_glu))
    layer = _Layer
    vec = lambda v, i: _Layer(v[:, None, :], i)

    xf = x.reshape(batch * seq, d_model)
    u, zg, la = _dense(xf, None, vec(norm_mix, 0), (layer(w_in_eff, 0), vec(gla_b_gate, 0)), d_s5=d_s5)
    for i in range(depth):
        y_s5 = _s5_mixer(u.reshape(batch, seq, d_s5), layer(s5_m, i), layer(s5_e, i), layer(s5_f, i),
                         layer(s5_apow, i), vec(s5_d, i), layer(w_glu_b, i), vec(s5_b_glu, i), vec(s5_out_norm, i))
        y_gla = _gla_mixer(zg, la, vec(gla_out_norm, i), batch=batch, d_gla=d_gla)
        mix_args = (y_s5.reshape(batch * seq, d_s5), y_gla, layer(w_out_b, i), vec(norm_ffn, i),
                    layer(w_fi_b, i), layer(w_fo_b, i))
        if i + 1 < depth:
            xf, u, zg, la = _dense(xf, mix_args, vec(norm_mix, i + 1),
                                   (layer(w_in_eff, i + 1), vec(gla_b_gate, i + 1)), d_s5=d_s5)
        else:
            xf = _dense(xf, mix_args, norm_final[None, :], None, d_s5=d_s5)
    return xf.reshape(batch, seq, d_model)
```

```python
import functools
from typing import NamedTuple

import jax
import jax.numpy as jnp
from jax import lax
from jax.experimental import pallas as pl
from jax.experimental.pallas import tpu as pltpu

S5_GROUP = 16
GLA_HEADS = 4
GLA_CHUNK = 64
GLA_GATE_NORM = 16.0
EPS = 1e-6

LANES = 128
MXU_DIM = 256
VMEM_LIMIT_BYTES = 56 * 1024 * 1024

DENSE_ROWS = 512
DENSE_PROJ_ROWS = 2048
DENSE_FINAL_ROWS = 1024
DENSE_GROUP = 256
FFN_CHUNKS = 2
S5_CHUNK = 16
S5_CHUNKS = 8
S5_FINISH_ROWS = 512
S5_PREP_GROUPS = 8
GLA_ROWS = 2048

BF16 = jnp.bfloat16
F32 = jnp.float32


def _dot(a, b):
    return jnp.dot(a, b, preferred_element_type=F32)


def _dot_nt(a, b):
    return lax.dot_general(a, b, (((1,), (1,)), ((), ())), preferred_element_type=F32)


def _dot_tn(a, b):
    return lax.dot_general(a, b, (((0,), (0,)), ((), ())), preferred_element_type=F32)


def _rms(x, gain):
    return x * lax.rsqrt(jnp.mean(x * x, axis=-1, keepdims=True) + EPS) * gain


class _Layer(NamedTuple):
    stacked: jax.Array
    index: int

    @property
    def shape(self):
        return self.stacked.shape[1:]


def _resident(x):
    if isinstance(x, _Layer):
        lead, shape, operand = (x.index,), x.shape, x.stacked
        block = (None,) + shape
    else:
        lead, shape, operand = (), x.shape, x
        block = shape
    return operand, pl.BlockSpec(block, lambda *_: lead + (0,) * len(shape), pipeline_mode=pl.Buffered(1))


def _dot_split(a, b):
    a_hi, b_hi = a.astype(BF16), b.astype(BF16)
    a_lo = (a - a_hi.astype(F32)).astype(BF16)
    b_lo = (b - b_hi.astype(F32)).astype(BF16)
    return _dot(a_hi, b_hi) + _dot(a_hi, b_lo) + _dot(a_lo, b_hi)


def _gate_fold_kernel(win_ref, w_gate_ref, weff_ref):
    o_gate = weff_ref.shape[2] - w_gate_ref.shape[2]
    w = win_ref[0]
    weff_ref[0, :, :o_gate] = w[:, :o_gate].astype(BF16)
    weff_ref[0, :, o_gate:] = _dot_split(w[:, o_gate:], w_gate_ref[0]).astype(BF16)


def _gate_fold(w_in, w_gate):
    depth, d_model, d_in = w_in.shape
    rank, kdim = w_gate.shape[1:]
    per_layer = lambda *s: pl.BlockSpec((1,) + s, lambda i: (i,) + (0,) * len(s))
    return pl.pallas_call(
        _gate_fold_kernel,
        grid=(depth,),
        in_specs=[per_layer(d_model, d_in), per_layer(rank, kdim)],
        out_specs=per_layer(d_model, d_in - rank + kdim),
        out_shape=jax.ShapeDtypeStruct((depth, d_model, d_in - rank + kdim), BF16),
        compiler_params=pltpu.CompilerParams(dimension_semantics=("arbitrary",), vmem_limit_bytes=VMEM_LIMIT_BYTES),
        name="gate_fold",
    )(w_in, w_gate)


def _cmul(ar, ai, br, bi):
    return ar * br - ai * bi, ar * bi + ai * br


def _discretise(lr, li, log_step):
    step = jnp.exp(log_step)
    mag = jnp.exp(lr * step)
    ar = mag * jnp.cos(li * step)
    ai = mag * jnp.sin(li * step)
    den = lr * lr + li * li
    fr = ((ar - 1.0) * lr + ai * li) / den
    fi = (ai * lr - (ar - 1.0) * li) / den
    return ar, ai, fr, fi


def _s5_prep_kernel(lam_re_row_ref, lam_im_row_ref, lam_re_col_ref, lam_im_col_ref, log_step_ref,
                    c_re_ref, c_im_ref, b_re_ref, b_im_ref, m_ref, e_ref, f_ref, apow_ref):
    t_c = S5_CHUNK
    h_c = S5_GROUP
    p_c = lam_re_row_ref.shape[-1]

    def one_group(gi, _):
        log_step = log_step_ref[0, gi]
        ar, ai, _, _ = _discretise(lam_re_row_ref[0, gi], lam_im_row_ref[0, gi], log_step)
        arc, aic, frc, fic = _discretise(lam_re_col_ref[0, gi], lam_im_col_ref[0, gi], log_step)

        pr = jnp.ones((h_c, p_c), F32)
        pi = jnp.zeros((h_c, p_c), F32)
        blocks_r, blocks_i = [], []
        for _ in range(t_c):
            blocks_r.append(pr)
            blocks_i.append(pi)
            pr, pi = _cmul(pr, pi, ar, ai)
        c_re = jnp.concatenate([c_re_ref[0, gi]] * t_c, axis=0)
        c_im = jnp.concatenate([c_im_ref[0, gi]] * t_c, axis=0)
        l_re, l_im = _cmul(c_re, c_im, jnp.concatenate(blocks_r, axis=0),
                           jnp.concatenate(blocks_i, axis=0))
        f_re, f_im = _cmul(l_re, l_im, ar, ai)
        f_ref[0, gi] = jnp.concatenate([f_re, -f_im], axis=1).astype(BF16)

        rep = (lax.broadcasted_iota(jnp.int32, (h_c, t_c * h_c), 0)
               == lax.broadcasted_iota(jnp.int32, (h_c, t_c * h_c), 1) % h_c).astype(F32)
        bb_re, bb_im = _cmul(frc, fic, b_re_ref[0, gi], b_im_ref[0, gi])
        bw_re, bw_im = _dot_split(bb_re, rep), _dot_split(bb_im, rep)
        k_wide = _dot_split(l_re, bw_re) - _dot_split(l_im, bw_im)
        per_tile = LANES // h_c
        lane_blk = lax.broadcasted_iota(jnp.int32, (t_c * h_c, LANES), 1) // h_c
        for lt in range(t_c // per_tile):
            k_tile = k_wide[:, lt * LANES:(lt + 1) * LANES]
            m = jnp.zeros((t_c * h_c, LANES), F32)
            for jj in range(per_tile):
                j = lt * per_tile + jj
                moved = k_tile if j == 0 else jnp.concatenate(
                    [jnp.zeros((j * h_c, LANES), F32), k_tile[:(t_c - j) * h_c]], axis=0)
                m = jnp.where(lane_blk == jj, moved, m)
            m_ref[0, gi, :, lt * LANES:(lt + 1) * LANES] = m.astype(BF16)

        lane_blk = lax.broadcasted_iota(jnp.int32, (p_c, LANES), 1) // h_c
        qr = jnp.ones((p_c, 1), F32)
        qi = jnp.zeros((p_c, 1), F32)
        for lt in range(t_c // per_tile - 1, -1, -1):
            pe_r = jnp.zeros((p_c, LANES), F32)
            pe_i = jnp.zeros((p_c, LANES), F32)
            for jj in range(per_tile - 1, -1, -1):
                pe_r = jnp.where(lane_blk == jj, qr, pe_r)
                pe_i = jnp.where(lane_blk == jj, qi, pe_i)
                qr, qi = _cmul(qr, qi, arc, aic)
            cols = slice(lt * LANES, (lt + 1) * LANES)
            e_re, e_im = _cmul(pe_r, pe_i, bw_re[:, cols], bw_im[:, cols])
            e_ref[0, gi, 0:p_c, cols] = e_re.astype(BF16)
            e_ref[0, gi, p_c:2 * p_c, cols] = e_im.astype(BF16)

        for k in range(3):
            apow_ref[0, gi, k, 0] = jnp.broadcast_to(qr, (p_c, LANES))
            apow_ref[0, gi, k, 1] = jnp.broadcast_to(qi, (p_c, LANES))
            qr, qi = _cmul(qr, qi, qr, qi)
        return 0

    lax.fori_loop(0, m_ref.shape[1], one_group, 0)


def _s5_prep(lam_re, lam_im, log_step, b_re, b_im, c_re, c_im):
    depth, groups, states = lam_re.shape
    t_c, h_c = S5_CHUNK, S5_GROUP
    row = lambda t: t.reshape(depth, groups, 1, states)
    col = lambda t: t.reshape(depth, groups, states, 1)
    blk = lambda *s: pl.BlockSpec((1, S5_PREP_GROUPS) + s, lambda i, g: (i, g) + (0,) * len(s))
    w = t_c * h_c
    return pl.pallas_call(
        _s5_prep_kernel,
        grid=(depth, groups // S5_PREP_GROUPS),
        in_specs=[blk(1, states)] * 2 + [blk(states, 1)] * 2 + [blk(1, 1)]
        + [blk(h_c, states)] * 2 + [blk(states, h_c)] * 2,
        out_specs=[blk(w, w), blk(2 * states, w), blk(w, 2 * states), blk(3, 2, states, LANES)],
        out_shape=[jax.ShapeDtypeStruct((depth, groups, w, w), BF16),
                   jax.ShapeDtypeStruct((depth, groups, 2 * states, w), BF16),
                   jax.ShapeDtypeStruct((depth, groups, w, 2 * states), BF16),
                   jax.ShapeDtypeStruct((depth, groups, 3, 2, states, LANES), F32)],
        compiler_params=pltpu.CompilerParams(dimension_semantics=("arbitrary", "arbitrary")),
        name="s5_prep",
    )(row(lam_re), row(lam_im), col(lam_re), col(lam_im), log_step.reshape(depth, groups, 1, 1),
      c_re, c_im, b_re, b_im)


def _s5_kernel(u_ref, m_ref, e_ref, f_ref, apow_ref, d_ref, wglu_ref, bglu_ref, onorm_ref, y_ref,
               ub_ref, xt_ref, yt_ref, y2_ref, z_ref, s_ref, carry_ref, perm_ref):
    n_b, t_blk, d_s5 = u_ref.shape
    groups = m_ref.shape[0]
    t_c, h_c = S5_CHUNK, S5_GROUP
    n_c = t_blk // t_c
    p_c = f_ref.shape[2] // 2
    samples = n_b * n_c
    assert samples == LANES and n_c == 8

    @pl.when(pl.program_id(1) == 0)
    def _():
        carry_ref[...] = jnp.zeros_like(carry_ref)

    @pl.when((pl.program_id(0) == 0) & (pl.program_id(1) == 0))
    def _():
        dst = lax.broadcasted_iota(jnp.int32, (t_blk, t_blk), 0)
        src = lax.broadcasted_iota(jnp.int32, (t_blk, t_blk), 1)
        perm_ref[0] = jnp.where(src == (dst % n_c) * t_c + dst // n_c, 1.0, 0.0).astype(BF16)
        perm_ref[1] = jnp.where(src == (dst % t_c) * n_c + dst // t_c, 1.0, 0.0).astype(BF16)

    for b in range(n_b):
        ub_ref[b] = _dot(perm_ref[0], u_ref[b].astype(BF16))
    for t in range(t_c):
        xt_ref[t] = ub_ref[:, t * n_c:(t + 1) * n_c, :].reshape(samples, d_s5).T
    for g in range(groups):
        a = xt_ref[:, g * h_c:(g + 1) * h_c, :].reshape(t_c * h_c, samples).astype(BF16)
        yt_ref[g] = _dot(m_ref[g], a)
        z_ref[g] = _dot(e_ref[g], a)

    c_idx = lax.broadcasted_iota(jnp.int32, (p_c, LANES), 1) % n_c
    for g in range(groups):
        z_re, z_im = z_ref[g, 0:p_c], z_ref[g, p_c:2 * p_c]
        v_re = jnp.where(c_idx == 0, carry_ref[g, 0:p_c], pltpu.roll(z_re, 1, 1))
        v_im = jnp.where(c_idx == 0, carry_ref[g, p_c:2 * p_c], pltpu.roll(z_im, 1, 1))
        for k in range(3):
            hop = 1 << k
            sh_re = jnp.where(c_idx >= hop, pltpu.roll(v_re, hop, 1), 0.0)
            sh_im = jnp.where(c_idx >= hop, pltpu.roll(v_im, hop, 1), 0.0)
            d_re, d_im = _cmul(apow_ref[g, k, 0], apow_ref[g, k, 1], sh_re, sh_im)
            v_re, v_im = v_re + d_re, v_im + d_im
        s_ref[g, 0:p_c] = v_re.astype(BF16)
        s_ref[g, p_c:2 * p_c] = v_im.astype(BF16)
        o_re, o_im = _cmul(apow_ref[g, 0, 0], apow_ref[g, 0, 1], v_re, v_im)
        carry_ref[g, 0:p_c] = pltpu.roll(o_re + z_re, LANES - (n_c - 1), 1)
        carry_ref[g, p_c:2 * p_c] = pltpu.roll(o_im + z_im, LANES - (n_c - 1), 1)

    for g in range(groups):
        yt_ref[g] += _dot(f_ref[g], s_ref[g])
    for t in range(t_c):
        y2_ref[t] = yt_ref[:, t * h_c:(t + 1) * h_c, :].reshape(d_s5, samples).T
    for b in range(n_b):
        y = y2_ref[:, b * n_c:(b + 1) * n_c, :].reshape(t_blk, d_s5)
        y_hi = y.astype(BF16)
        y_lo = (y - y_hi.astype(F32)).astype(BF16)
        y2 = _dot(perm_ref[1], jnp.concatenate([y_hi, y_lo], axis=-1))
        ub_ref[b] = y2[:, :d_s5] + y2[:, d_s5:] + d_ref[...] * u_ref[b]
    fin = S5_FINISH_ROWS // t_blk
    for q in range(n_b // fin):
        y = ub_ref[q * fin:(q + 1) * fin].reshape(S5_FINISH_ROWS, d_s5)
        y = jax.nn.gelu(y)
        y = y * jax.nn.sigmoid(_dot(y.astype(BF16), wglu_ref[...]) + bglu_ref[...])
        y_ref[q * fin:(q + 1) * fin] = _rms(y, onorm_ref[...]).reshape(fin, t_blk, d_s5)


def _s5_mixer(u, m, e, f, apow, d_skip, wglu, bglu, onorm):
    batch, seq, d_s5 = u.shape
    groups, w = m.shape[0], m.shape[1]
    states = f.shape[2] // 2
    t_blk = S5_CHUNK * S5_CHUNKS
    n_b = LANES // S5_CHUNKS
    assert batch % n_b == 0 and seq % t_blk == 0
    block = pl.BlockSpec((n_b, t_blk, d_s5), lambda bb, t: (bb, t, 0))
    params, param_specs = zip(*map(_resident, (m, e, f, apow, d_skip, wglu, bglu, onorm)))
    return pl.pallas_call(
        _s5_kernel,
        grid=(batch // n_b, seq // t_blk),
        in_specs=[block, *param_specs],
        out_specs=block,
        out_shape=jax.ShapeDtypeStruct((batch, seq, d_s5), F32),
        scratch_shapes=[pltpu.VMEM((n_b, t_blk, d_s5), F32),
                        pltpu.VMEM((S5_CHUNK, d_s5, LANES), F32),
                        pltpu.VMEM((groups, w, LANES), F32),
                        pltpu.VMEM((S5_CHUNK, LANES, d_s5), F32),
                        pltpu.VMEM((groups, 2 * states, LANES), F32),
                        pltpu.VMEM((groups, 2 * states, LANES), BF16),
                        pltpu.VMEM((groups, 2 * states, LANES), F32),
                        pltpu.VMEM((2, t_blk, t_blk), BF16)],
        compiler_params=pltpu.CompilerParams(dimension_semantics=("arbitrary", "arbitrary"),
                                             vmem_limit_bytes=VMEM_LIMIT_BYTES),
        name="s5_mixer",
    )(u, *params)


def _log_sigmoid(x):
    return jnp.minimum(x, 0.0) - jnp.log1p(jnp.exp(-jnp.abs(x)))


def _gla_kernel(zg_ref, la_ref, onorm_ref, y_ref, st_ref):
    rows = zg_ref.shape[0]
    d_gla = y_ref.shape[1]
    dv = d_gla // GLA_HEADS
    dk = dv // 2
    kdim = GLA_HEADS * dk
    o_k, o_v, o_g = kdim, 2 * kdim, 2 * kdim + d_gla
    c = GLA_CHUNK
    assert 2 * dk == LANES and dv == LANES

    @pl.when(pl.program_id(1) == 0)
    def _():
        st_ref[...] = jnp.zeros_like(st_ref)

    r_i = lax.broadcasted_iota(jnp.int32, (c, 2 * c), 0)
    c_i = lax.broadcasted_iota(jnp.int32, (c, 2 * c), 1)
    tri2 = jnp.where((c_i % c) <= r_i, 1.0, 0.0).astype(BF16)
    causal = (lax.broadcasted_iota(jnp.int32, (c, LANES), 0)
              >= lax.broadcasted_iota(jnp.int32, (c, LANES), 1) % c)
    low_head = lax.broadcasted_iota(jnp.int32, (c, LANES), 1) < dk
    own_block = ((lax.broadcasted_iota(jnp.int32, (2 * dv, LANES), 0) < dv)
                 == (lax.broadcasted_iota(jnp.int32, (2 * dv, LANES), 1) < dk))
    n_chunks = rows // c
    pairs = GLA_HEADS // 2
    rows_of = lambda t, n: t[n * c:(n + 1) * c]
    zeros_v = jnp.zeros((c, dv), BF16)

    la = la_ref[...]
    la_hi = la.astype(BF16)
    la_lo = (la - la_hi.astype(F32)).astype(BF16)
    cums = [_dot(tri2, jnp.concatenate([rows_of(la_hi, n), rows_of(la_lo, n)], axis=0)) for n in range(n_chunks)]
    cum = jnp.concatenate(cums, axis=0)
    totals = [t[c - 1:c, :] for t in cums]
    total = jnp.concatenate([jnp.broadcast_to(t, (c, kdim)) for t in totals], axis=0)
    q_dec = (zg_ref[:, 0:kdim].astype(F32) * (dk ** -0.5) * jnp.exp(cum)).astype(BF16)
    k = zg_ref[:, o_k:o_k + kdim].astype(F32)
    k_inv = k * jnp.exp(-cum)
    k_end = (k * jnp.exp(total - cum)).astype(BF16)
    v = zg_ref[:, o_v:o_v + d_gla]

    def lanes_of(t, n, pair):
        return t[n * c:(n + 1) * c, pair * LANES:(pair + 1) * LANES]

    scores = []
    for n in range(n_chunks):
        for pair in range(pairs):
            kp = lanes_of(k_inv, n, pair)
            keys = jnp.concatenate([jnp.where(low_head, kp, 0.0), jnp.where(low_head, 0.0, kp)], axis=0)
            s = _dot_nt(lanes_of(q_dec, n, pair), keys.astype(BF16))
            scores.append(jnp.where(causal, s, 0.0).astype(BF16))

    intra, upds = [], []
    for n in range(n_chunks):
        for pair in range(pairs):
            v_even = v[n * c:(n + 1) * c, (2 * pair) * dv:(2 * pair + 1) * dv]
            v_odd = v[n * c:(n + 1) * c, (2 * pair + 1) * dv:(2 * pair + 2) * dv]
            v_diag = jnp.concatenate([jnp.concatenate([v_even, zeros_v], axis=1),
                                      jnp.concatenate([zeros_v, v_odd], axis=1)], axis=0)
            intra.append(_dot(scores[n * pairs + pair], v_diag))
            upd = _dot_tn(jnp.concatenate([v_even, v_odd], axis=1), lanes_of(k_end, n, pair))
            upds.append(jnp.where(own_block, upd, 0.0))

    outs = [[None] * pairs for _ in range(n_chunks)]
    for pair in range(pairs):
        st = st_ref[pair]
        for n in range(n_chunks):
            outs[n][pair] = intra[n * pairs + pair] + _dot_nt(lanes_of(q_dec, n, pair), st.astype(BF16))
            decay = jnp.exp(totals[n][:, pair * LANES:(pair + 1) * LANES])
            st = st * decay + upds[n * pairs + pair]
        st_ref[pair] = st

    for h in range(GLA_HEADS):
        o = jnp.concatenate([outs[n][h // 2][:, (h % 2) * dv:(h % 2 + 1) * dv] for n in range(n_chunks)], axis=0)
        o = o * lax.rsqrt(jnp.mean(o * o, axis=-1, keepdims=True) + EPS)
        g = zg_ref[:, o_g + h * dv:o_g + (h + 1) * dv].astype(F32)
        y = jax.nn.silu(g) * (o * onorm_ref[:, h * dv:(h + 1) * dv])
        y_ref[:, h * dv:(h + 1) * dv] = y.astype(y_ref.dtype)


def _gla_mixer(zg, la, onorm, *, batch, d_gla):
    rows_total = zg.shape[0]
    seq = rows_total // batch
    rows = min(GLA_ROWS, seq)
    n_t = seq // rows
    token_rows = lambda w: pl.BlockSpec((rows, w), lambda b, t: (b * n_t + t, 0))
    onorm, onorm_spec = _resident(onorm)
    return pl.pallas_call(
        _gla_kernel,
        grid=(batch, n_t),
        in_specs=[token_rows(zg.shape[1]), token_rows(la.shape[1]), onorm_spec],
        out_specs=token_rows(d_gla),
        out_shape=jax.ShapeDtypeStruct((rows_total, d_gla), BF16),
        scratch_shapes=[pltpu.VMEM((GLA_HEADS // 2, 2 * d_gla // GLA_HEADS, LANES), F32)],
        compiler_params=pltpu.CompilerParams(dimension_semantics=("arbitrary", "arbitrary"),
                                             vmem_limit_bytes=VMEM_LIMIT_BYTES),
        name="gla_mixer",
    )(zg, la, onorm)


def _dense_kernel(*refs, mix, proj, d_s5, d_ff):
    refs = list(refs)
    x_ref = refs.pop(0)
    n_groups = x_ref.shape[0] // DENSE_GROUP
    groups = [pl.ds(s * DENSE_GROUP, DENSE_GROUP) for s in range(n_groups)]
    xs = [x_ref[r, :] for r in groups]
    if mix:
        ys5_ref, ygla_ref, wout_ref, nffn_ref, wfi_ref, wfo_ref = refs[:6]
        refs = refs[6:]
        xs = [x + _dot(ys5_ref[r, :].astype(BF16), wout_ref[0:d_s5, :]) + _dot(ygla_ref[r, :], wout_ref[d_s5:, :])
              for x, r in zip(xs, groups)]
        hs = [_rms(x, nffn_ref[...]).astype(BF16) for x in xs]
        tiles = d_ff // MXU_DIM
        bounds = [MXU_DIM * ((tiles * c + FFN_CHUNKS - 1) // FFN_CHUNKS) for c in range(FFN_CHUNKS + 1)]
        acts = [[] for _ in range(n_groups)]
        for lo, hi in zip(bounds[:-1], bounds[1:]):
            for s in range(n_groups):
                gate = _dot(hs[s], wfi_ref[:, lo:hi])
                up = _dot(hs[s], wfi_ref[:, d_ff + lo:d_ff + hi])
                acts[s].append((jax.nn.silu(gate) * up).astype(BF16))
        xs = [x + _dot(jnp.concatenate(a, axis=1), wfo_ref[...]) for x, a in zip(xs, acts)]
    norm_ref = refs.pop(0)
    hs = [_rms(x, norm_ref[...]) for x in xs]
    if not proj:
        (out_ref,) = refs
        for h, r in zip(hs, groups):
            out_ref[r, :] = h
        return
    win_ref, bgate_ref = refs[:2]
    u_ref, zg_ref, la_ref = refs[-3:]
    o_a = d_s5 + zg_ref.shape[1]
    for x, h, r in zip(xs, hs, groups):
        if mix:
            refs[2][r, :] = x
        z = _dot(h.astype(BF16), win_ref[...])
        u_ref[r, :] = z[:, :d_s5]
        zg_ref[r, :] = z[:, d_s5:o_a].astype(BF16)
        la_ref[r, :] = _log_sigmoid(z[:, o_a:] + bgate_ref[...]) * (1.0 / GLA_GATE_NORM)


def _dense(x, mix_args, norm, proj_args, *, d_s5):
    rows_total, d_model = x.shape
    tm = DENSE_PROJ_ROWS if mix_args is None else DENSE_ROWS if proj_args is not None else DENSE_FINAL_ROWS
    token_rows = lambda w: pl.BlockSpec((tm, w), lambda r: (r, 0))
    mix, proj = mix_args is not None, proj_args is not None
    args, in_specs = [x], [token_rows(d_model)]
    d_ff = 0
    if mix:
        y_s5, y_gla, w_out, norm_ffn, w_fi, w_fo = mix_args
        d_ff = w_fo.shape[0]
        assert d_ff % MXU_DIM == 0
        args += [y_s5, y_gla]
        in_specs += [token_rows(d_s5), token_rows(y_gla.shape[1])]
        resident = [w_out, norm_ffn, w_fi, w_fo, norm]
    else:
        resident = [norm]
    if proj:
        w_in, b_gate = proj_args
        la_w = b_gate.shape[1]
        zg_w = w_in.shape[1] - d_s5 - la_w
        resident += [w_in, b_gate]
        out_specs = [token_rows(d_model), token_rows(d_s5), token_rows(zg_w), token_rows(la_w)]
        out_shape = [jax.ShapeDtypeStruct((rows_total, d_model), F32),
                     jax.ShapeDtypeStruct((rows_total, d_s5), F32),
                     jax.ShapeDtypeStruct((rows_total, zg_w), BF16),
                     jax.ShapeDtypeStruct((rows_total, la_w), F32)]
        if not mix:
            out_specs, out_shape = out_specs[1:], out_shape[1:]
    else:
        out_specs = token_rows(d_model)
        out_shape = jax.ShapeDtypeStruct((rows_total, d_model), F32)
    for operand, spec in map(_resident, resident):
        args.append(operand)
        in_specs.append(spec)
    return pl.pallas_call(
        functools.partial(_dense_kernel, mix=mix, proj=proj, d_s5=d_s5, d_ff=d_ff),
        grid=(rows_total // tm,),
        in_specs=in_specs,
        out_specs=out_specs,
        out_shape=out_shape,
        compiler_params=pltpu.CompilerParams(dimension_semantics=("arbitrary",),
                                             vmem_limit_bytes=VMEM_LIMIT_BYTES),
        name="dense_" + ("mix" if mix else "") + ("proj" if proj else "final"),
    )(*args)


def kernel(x, norm_mix, w_in, s5_lam_re, s5_lam_im, s5_b_re, s5_b_im, s5_c_re, s5_c_im, s5_d, s5_log_step,
           s5_w_glu, s5_b_glu, s5_out_norm, gla_w_gate, gla_b_gate, gla_out_norm, w_out, norm_ffn, w_ffn_in,
           w_ffn_out, norm_final):
    batch, seq, d_model = x.shape
    depth = w_in.shape[0]
    d_s5 = s5_d.shape[1]
    d_gla = gla_out_norm.shape[1]
    assert (batch * seq) % DENSE_PROJ_ROWS == 0 and DENSE_PROJ_ROWS % DENSE_FINAL_ROWS == 0 and DENSE_FINAL_ROWS % DENSE_ROWS == 0 and seq % GLA_CHUNK == 0 and seq % (S5_CHUNK * S5_CHUNKS) == 0

    s5_m, s5_e, s5_f, s5_apow = _s5_prep(s5_lam_re, s5_lam_im, s5_log_step, s5_b_re, s5_b_im, s5_c_re, s5_c_im)
    w_in_eff = _gate_fold(w_in, gla_w_gate)
    w_out_b, w_fi_b, w_fo_b, w_glu_b = (w.astype(BF16) for w in (w_out, w_ffn_in, w_ffn_out, s5_w_glu))
    layer = _Layer
    vec = lambda v, i: _Layer(v[:, None, :], i)

    xf = x.reshape(batch * seq, d_model)
    u, zg, la = _dense(xf, None, vec(norm_mix, 0), (layer(w_in_eff, 0), vec(gla_b_gate, 0)), d_s5=d_s5)
    for i in range(depth):
        y_s5 = _s5_mixer(u.reshape(batch, seq, d_s5), layer(s5_m, i), layer(s5_e, i), layer(s5_f, i),
                         layer(s5_apow, i), vec(s5_d, i), layer(w_glu_b, i), vec(s5_b_glu, i), vec(s5_out_norm, i))
        y_gla = _gla_mixer(zg, la, vec(gla_out_norm, i), batch=batch, d_gla=d_gla)
        mix_args = (y_s5.reshape(batch * seq, d_s5), y_gla, layer(w_out_b, i), vec(norm_ffn, i),
                    layer(w_fi_b, i), layer(w_fo_b, i))
        if i + 1 < depth:
            xf, u, zg, la = _dense(xf, mix_args, vec(norm_mix, i + 1),
                                   (layer(w_in_eff, i + 1), vec(gla_b_gate, i + 1)), d_s5=d_s5)
        else:
            xf = _dense(xf, mix_args, norm_final[None, :], None, d_s5=d_s5)
    return xf.reshape(batch, seq, d_model)
```
